```python
import jax, jax.numpy as jnp
from jax import lax
import numpy as np

D_MODEL = 2048
BATCH = 16
SEQ = 2048
DEPTH = 1
DEC_BATCH = 32
DEC_SEQ = 8
PAST_LEN = 16384
PAGE_SIZE = 128

N_HEADS = 16
N_KV = 4
HPG = N_HEADS // N_KV
HEAD_DIM = 64
ROPE_DIM = HEAD_DIM // 4
ROPE_THETA = 500000.0
BLOCK = 64
N_SEL = 16
WINDOW = 512
Q_BLOCK = 16
RW_HEADS = 16
RW_HEAD = 64
RW_DIM = RW_HEADS * RW_HEAD
LORA_W = 64
LORA_A = 64
LN_X_EPS = 64e-5
D_FF = 5632
CONV_W = 3
NORM_EPS = 1e-6

ATT_DIM = N_HEADS * HEAD_DIM
KV_COLS = 2 * N_KV * HEAD_DIM
SHIFT_COLS = 3 * RW_DIM + LORA_W + LORA_A
PROJ_SIZES = (ATT_DIM, KV_COLS, KV_COLS, KV_COLS, 3 * N_HEADS, SHIFT_COLS, 2 * D_MODEL)
RWKV_SIZES = (RW_DIM, RW_DIM, RW_DIM, LORA_W, LORA_A)
PROJ_COLS = ATT_DIM + 3 * KV_COLS + 3 * N_HEADS + SHIFT_COLS + 2 * D_MODEL
NEG = -1e30

kernel_name = 'nsa_rwkv7_convffn_hybrid_step'


def _splits(sizes):
    return [int(v) for v in np.cumsum(np.array(sizes))[:-1]]


def rms_norm(x, gain):
    xf = x.astype(jnp.float32)
    y = xf * lax.rsqrt(jnp.mean(xf * xf, axis=-1, keepdims=True) + NORM_EPS)
    return (y * gain.astype(jnp.float32)).astype(x.dtype)


def partial_rope(x, pos):
    half = ROPE_DIM // 2
    inv = ROPE_THETA ** (-jnp.arange(half, dtype=jnp.float32) * 2.0 / ROPE_DIM)
    ang = pos.astype(jnp.float32)[:, None] * inv[None, :]
    cos = jnp.cos(ang)[None, :, None, :]
    sin = jnp.sin(ang)[None, :, None, :]
    xf = x[..., :ROPE_DIM].astype(jnp.float32)
    x1, x2 = xf[..., :half], xf[..., half:]
    rot = jnp.concatenate([x1 * cos - x2 * sin, x2 * cos + x1 * sin], axis=-1).astype(x.dtype)
    return jnp.concatenate([rot, x[..., ROPE_DIM:]], axis=-1)


def masked_softmax(s, mask, axes):
    s = jnp.where(mask, s.astype(jnp.float32), NEG)
    m = jnp.max(s, axis=axes, keepdims=True)
    e = jnp.where(mask, jnp.exp(s - m), 0.0)
    return e / jnp.maximum(jnp.sum(e, axis=axes, keepdims=True), 1e-30)


def kv_rows(p, gain_k, pos):
    B, T = p.shape[:2]
    kv = p.reshape(B, T, 2, N_KV, HEAD_DIM)
    k = partial_rope(rms_norm(kv[:, :, 0], gain_k), pos)
    return jnp.stack([k, kv[:, :, 1]], axis=2)


def compress(kv_blocks, w_cmp):
    return jnp.einsum('bnlcgd,cld->bncgd', kv_blocks, w_cmp.astype(kv_blocks.dtype))


def mixer_inputs(x, pos, norm_g, w_in, q_gain, k_gains):
    B, T = x.shape[:2]
    p = rms_norm(x, norm_g) @ w_in
    p_q, p_cmp, p_sel, p_win, p_gate, p_shift, p_merge = jnp.split(p, _splits(PROJ_SIZES), axis=-1)
    q = partial_rope(rms_norm(p_q.reshape(B, T, N_HEADS, HEAD_DIM), q_gain), pos)
    kv_cmp = kv_rows(p_cmp, k_gains[0], pos)
    kv_sel = kv_rows(p_sel, k_gains[1], pos)
    kv_win = kv_rows(p_win, k_gains[2], pos)
    gates = jax.nn.sigmoid(p_gate.reshape(B, T, N_HEADS, 3))
    return q, kv_cmp, kv_sel, kv_win, gates, p_shift, p_merge


def nsa_core(q, pos_q, kc, vc, gather_sel, kw, vw, pos_kw, gates):
    B, Q = q.shape[:2]
    scale = HEAD_DIM ** -0.5
    qg = q.reshape(B, Q, N_KV, HPG, HEAD_DIM)
    nb = kc.shape[1]
    blk = jnp.arange(nb, dtype=jnp.int32)
    s_c = jnp.einsum('bqghd,bngd->bqghn', qg, kc) * scale
    vis = ((blk + 1) * BLOCK - 1)[None, :] <= pos_q[:, None]
    p_c = masked_softmax(s_c, vis[None, :, None, None, :], (-1,))
    o_c = jnp.einsum('bqghn,bngd->bqghd', p_c.astype(vc.dtype), vc)
    cur = pos_q // BLOCK
    cand = (blk[None, :] < cur[:, None])[None, :, None, :]
    imp = jnp.where(cand, jnp.sum(p_c, axis=3), -1.0)
    _, top = lax.top_k(imp, min(N_SEL - 1, nb))
    cur_b = jnp.broadcast_to(cur[None, :, None, None], (B, Q, N_KV, 1))
    idx = jnp.concatenate([top.astype(jnp.int32), cur_b], axis=-1)
    valid = jnp.concatenate([top < cur_b, jnp.ones(cur_b.shape, dtype=bool)], axis=-1)
    ks, vs = gather_sel(idx)
    kpos = idx[..., None] * BLOCK + jnp.arange(BLOCK, dtype=jnp.int32)
    m_s = valid[..., None] & (kpos <= pos_q[None, :, None, None, None])
    s_s = jnp.einsum('bqghd,bqgkld->bqghkl', qg, ks) * scale
    p_s = masked_softmax(s_s, m_s[:, :, :, None], (-2, -1))
    o_s = jnp.einsum('bqghkl,bqgkld->bqghd', p_s.astype(vs.dtype), vs)
    dist = pos_q[:, None] - pos_kw[None, :]
    m_w = (dist >= 0) & (dist <= WINDOW) & (pos_kw >= 0)[None, :]
    s_w = jnp.einsum('bqghd,bkgd->bqghk', qg, kw) * scale
    p_w = masked_softmax(s_w, m_w[None, :, None, None, :], (-1,))
    o_w = jnp.einsum('bqghk,bkgd->bqghd', p_w.astype(vw.dtype), vw)
    g = gates.reshape(B, Q, N_KV, HPG, 3)
    o = g[..., 0:1] * o_c + g[..., 1:2] * o_s + g[..., 2:3] * o_w
    return o.reshape(B, Q, ATT_DIM)


def nsa_prompt(q, kv_cmp, kv_sel, kv_win, gates, w_cmp):
    B, T = q.shape[:2]
    nb = T // BLOCK
    blk_shape = (B, nb, BLOCK, 2, N_KV, HEAD_DIM)
    summ = compress(kv_cmp.reshape(blk_shape), w_cmp)
    sel_blocks = kv_sel.reshape(blk_shape)
    bi = jnp.arange(B)[:, None, None, None]
    gi = jnp.arange(N_KV)[None, None, :, None]

    def gather_sel(idx):
        kv = sel_blocks[bi, idx, :, :, gi, :]
        return kv[..., 0, :], kv[..., 1, :]

    win_pad = jnp.pad(kv_win, ((0, 0), (WINDOW, 0), (0, 0), (0, 0), (0, 0)))

    def chunk(c):
        q0 = c * Q_BLOCK
        qc = lax.dynamic_slice_in_dim(q, q0, Q_BLOCK, axis=1)
        gc = lax.dynamic_slice_in_dim(gates, q0, Q_BLOCK, axis=1)
        kwc = lax.dynamic_slice_in_dim(win_pad, q0, WINDOW + Q_BLOCK, axis=1)
        pos_q = q0 + jnp.arange(Q_BLOCK, dtype=jnp.int32)
        pos_kw = q0 - WINDOW + jnp.arange(WINDOW + Q_BLOCK, dtype=jnp.int32)
        return nsa_core(qc, pos_q, summ[:, :, 0], summ[:, :, 1], gather_sel,
                        kwc[:, :, 0], kwc[:, :, 1], pos_kw, gc)

    out = lax.map(chunk, jnp.arange(T // Q_BLOCK, dtype=jnp.int32))
    y = jnp.transpose(out, (1, 0, 2, 3)).reshape(B, T, ATT_DIM)
    return y, kv_win[:, T - min(WINDOW, T):]


def nsa_sample(q, kv_cmp, kv_sel, kv_win, gates, pos, pool_cmp, pool_sel, page_table, win_buf, w_cmp, layer):
    DB, T = q.shape[:2]
    nbp = PAST_LEN // BLOCK
    bpp = PAGE_SIZE // BLOCK
    past_c = pool_cmp[layer, page_table].reshape(DB, nbp, BLOCK, 2, N_KV, HEAD_DIM)
    nfull = T // BLOCK
    new_c = kv_cmp[:, :nfull * BLOCK].reshape(DB, nfull, BLOCK, 2, N_KV, HEAD_DIM)
    summ = jnp.concatenate([compress(past_c, w_cmp), compress(new_c.astype(past_c.dtype), w_cmp)], axis=1)
    nbn = -(-T // BLOCK)
    new_s = jnp.pad(kv_sel, ((0, 0), (0, nbn * BLOCK - T), (0, 0), (0, 0), (0, 0)))
    new_s = new_s.reshape(DB, nbn, BLOCK, 2, N_KV, HEAD_DIM)
    pool = pool_sel.reshape(pool_sel.shape[0], pool_sel.shape[1], bpp, BLOCK, 2, N_KV, HEAD_DIM)
    bi = jnp.arange(DB)[:, None, None, None]
    gi = jnp.arange(N_KV)[None, None, :, None]

    def gather_sel(idx):
        pidx = jnp.minimum(idx, nbp - 1)
        phys = page_table[bi, pidx // bpp]
        past = pool[layer, phys, pidx % bpp, :, :, gi, :]
        new = new_s[bi, jnp.clip(idx - nbp, 0, nbn - 1), :, :, gi, :]
        kv = jnp.where((idx < nbp)[..., None, None, None], past, new.astype(past.dtype))
        return kv[..., 0, :], kv[..., 1, :]

    keep = win_buf.shape[1]
    kw = jnp.concatenate([win_buf.astype(kv_win.dtype), kv_win], axis=1)
    pos_kw = PAST_LEN - keep + jnp.arange(keep + T, dtype=jnp.int32)
    y = nsa_core(q, pos, summ[:, :, 0], summ[:, :, 1], gather_sel, kw[:, :, 0], kw[:, :, 1], pos_kw, gates)
    return y, kw[:, T:]


def wkv_scan(r, decay, k, v, kk, a, s0):
    def step(S, inp):
        r_t, w_t, k_t, v_t, kk_t, a_t = inp
        sa = jnp.einsum('bhvk,bhk->bhv', S, kk_t)
        S = (S * w_t[:, :, None, :] - sa[..., None] * (kk_t * a_t)[:, :, None, :]
             + v_t[..., None] * k_t[:, :, None, :])
        return S, jnp.einsum('bhvk,bhk->bhv', S, r_t)
    xs = tuple(jnp.swapaxes(t, 0, 1) for t in (r, decay, k, v, kk, a))
    S, ys = lax.scan(step, s0, xs)
    return jnp.swapaxes(ys, 0, 1), S


def rwkv_branch(p_shift, shift_prev, s0, mu, w0, w_lora_w, a0, w_lora_a, k_k, k_a, r_k, ln_w, ln_b):
    B, T = p_shift.shape[:2]
    f32 = jnp.float32
    prev = jnp.concatenate([shift_prev[:, None].astype(p_shift.dtype), p_shift[:, :-1]], axis=1)
    z = p_shift + (prev - p_shift) * mu
    r, k, v, xw, xa = jnp.split(z, _splits(RWKV_SIZES), axis=-1)
    wl = -jax.nn.softplus(-(w0 + jnp.tanh(xw) @ w_lora_w).astype(f32)) - 0.5
    decay = jnp.exp(-jnp.exp(wl))
    a = jax.nn.sigmoid((a0 + xa @ w_lora_a).astype(f32))

    def heads(t):
        return t.astype(f32).reshape(t.shape[:-1] + (RW_HEADS, RW_HEAD))

    r, k, v, decay, a = heads(r), heads(k), heads(v), heads(decay), heads(a)
    kk = k * heads(k_k)
    kk = kk * lax.rsqrt(jnp.maximum(jnp.sum(kk * kk, axis=-1, keepdims=True), 1e-24))
    k = k * (1.0 + (a - 1.0) * heads(k_a))
    y, s_new = wkv_scan(r, decay, k, v, kk, a, s0.astype(f32))
    mean = jnp.mean(y, axis=-1, keepdims=True)
    var = jnp.mean(jnp.square(y - mean), axis=-1, keepdims=True)
    y = (y - mean) * lax.rsqrt(var + LN_X_EPS) * heads(ln_w) + heads(ln_b)
    y = y + jnp.sum(r * k * heads(r_k), axis=-1, keepdims=True) * v
    return y.reshape(B, T, RW_DIM).astype(p_shift.dtype), s_new.astype(s0.dtype), p_shift[:, -1]


def merge_out(x, o_att, y_rw, p_merge, w_a, w_b, w_o):
    g_a, g_b = jnp.split(p_merge, 2, axis=-1)
    m = jax.nn.sigmoid(g_a) * (o_att @ w_a) + jax.nn.sigmoid(g_b) * (y_rw @ w_b)
    return x + m @ w_o


def conv_ffn(h, conv_prev, norm_g, w_up, conv_w, conv_b, w_down):
    T = h.shape[1]
    u = rms_norm(h, norm_g) @ w_up
    up = jnp.concatenate([conv_prev.astype(u.dtype), u], axis=1)
    c = conv_b
    for j in range(CONV_W):
        c = c + conv_w[j] * up[:, j:j + T]
    gate, val = jnp.split(c, 2, axis=-1)
    return h + (jax.nn.silu(gate) * val) @ w_down, up[:, T:]


def setup_inputs(seed: int = 0) -> dict:
    key = jax.random.key(seed)
    ks = jax.random.split(key, 40)
    f32 = jnp.float32

    def nrm(i, shape, scale):
        return jax.random.normal(ks[i], shape, f32) * scale

    def gain(i, shape):
        return 1.0 + 0.1 * jax.random.normal(ks[i], shape, f32)

    n_pages = PAST_LEN // PAGE_SIZE
    n_used = DEC_BATCH * n_pages
    n_pool = n_used + (n_used + 3) // 4
    page_table = jax.random.permutation(ks[4], n_pool)[:n_used].reshape(DEC_BATCH, n_pages).astype(jnp.int32)
    win_keep = min(WINDOW, PAST_LEN)
    pool_shape = (DEPTH, n_pool, PAGE_SIZE, 2, N_KV, HEAD_DIM)
    L = (DEPTH,)
    return {
        'x_prompt': nrm(0, (BATCH, SEQ, D_MODEL), 1.0),
        'x_sample': nrm(1, (DEC_BATCH, DEC_SEQ, D_MODEL), 1.0),
        'cache_kv_cmp': nrm(2, pool_shape, 1.0),
        'cache_kv_sel': nrm(3, pool_shape, 1.0),
        'page_table': page_table,
        'cache_kv_win': nrm(5, L + (DEC_BATCH, win_keep, 2, N_KV, HEAD_DIM), 1.0),
        'state_wkv': nrm(6, L + (DEC_BATCH, RW_HEADS, RW_HEAD, RW_HEAD), 0.3),
        'state_shift': nrm(7, L + (DEC_BATCH, SHIFT_COLS), 1.0),
        'state_conv': nrm(8, L + (DEC_BATCH, CONV_W - 1, 2 * D_FF), 1.0),
        'norm1_g': gain(9, L + (D_MODEL,)),
        'w_in': nrm(10, L + (D_MODEL, PROJ_COLS), D_MODEL ** -0.5),
        'q_gain': gain(11, L + (HEAD_DIM,)),
        'k_gains': gain(12, L + (3, HEAD_DIM)),
        'w_cmp': (1.0 + nrm(13, L + (2, BLOCK, HEAD_DIM), 0.2)) * BLOCK ** -0.5,
        'mu_shift': jax.random.uniform(ks[14], L + (SHIFT_COLS,), f32),
        'w0': -2.5 + nrm(15, L + (RW_DIM,), 1.5),
        'w_lora_w': nrm(16, L + (LORA_W, RW_DIM), 0.5 * LORA_W ** -0.5),
        'a0': nrm(17, L + (RW_DIM,), 0.1),
        'w_lora_a': nrm(18, L + (LORA_A, RW_DIM), 0.5 * LORA_A ** -0.5),
        'k_k': 0.85 + nrm(19, L + (RW_DIM,), 0.05),
        'k_a': 1.0 + nrm(20, L + (RW_DIM,), 0.05),
        'r_k': nrm(21, L + (RW_DIM,), 0.1),
        'ln_x_w': gain(22, L + (RW_DIM,)),
        'ln_x_b': nrm(23, L + (RW_DIM,), 0.02),
        'w_branch_a': nrm(24, L + (ATT_DIM, D_MODEL), ATT_DIM ** -0.5),
        'w_branch_b': nrm(25, L + (RW_DIM, D_MODEL), RW_DIM ** -0.5),
        'w_out': nrm(26, L + (D_MODEL, D_MODEL), D_MODEL ** -0.5),
        'norm2_g': gain(27, L + (D_MODEL,)),
        'w_up': nrm(28, L + (D_MODEL, 2 * D_FF), D_MODEL ** -0.5),
        'conv_w': nrm(29, L + (CONV_W, 2 * D_FF), CONV_W ** -0.5),
        'conv_b': nrm(30, L + (2 * D_FF,), 0.02),
        'w_down': nrm(31, L + (D_FF, D_MODEL), D_FF ** -0.5),
    }


def reference(x_prompt, x_sample, cache_kv_cmp, cache_kv_sel, page_table, cache_kv_win, state_wkv,
              state_shift, state_conv, norm1_g, w_in, q_gain, k_gains, w_cmp, mu_shift, w0, w_lora_w,
              a0, w_lora_a, k_k, k_a, r_k, ln_x_w, ln_x_b, w_branch_a, w_branch_b, w_out, norm2_g,
              w_up, conv_w, conv_b, w_down):
    B, T = x_prompt.shape[:2]
    DB, TS = x_sample.shape[:2]
    pos_p = jnp.arange(T, dtype=jnp.int32)
    pos_s = PAST_LEN + jnp.arange(TS, dtype=jnp.int32)
    h_p, h_s = x_prompt, x_sample
    kvc_p, kvc_s, kvs_p, kvs_s, kvw_p, kvw_s = [], [], [], [], [], []
    wkv_p, wkv_s, sh_p, sh_s, cv_p, cv_s = [], [], [], [], [], []
    for l in range(DEPTH):
        rw = (mu_shift[l], w0[l], w_lora_w[l], a0[l], w_lora_a[l], k_k[l], k_a[l], r_k[l], ln_x_w[l], ln_x_b[l])
        q, kv_cmp, kv_sel, kv_win, gates, p_shift, p_merge = mixer_inputs(h_p, pos_p, norm1_g[l], w_in[l], q_gain[l], k_gains[l])
        o_att, win_new = nsa_prompt(q, kv_cmp, kv_sel, kv_win, gates, w_cmp[l])
        y_rw, s_new, sh_new = rwkv_branch(p_shift, jnp.zeros((B, SHIFT_COLS), p_shift.dtype),
                                          jnp.zeros((B, RW_HEADS, RW_HEAD, RW_HEAD), jnp.float32), *rw)
        h_p = merge_out(h_p, o_att, y_rw, p_merge, w_branch_a[l], w_branch_b[l], w_out[l])
        h_p, cv_new = conv_ffn(h_p, jnp.zeros((B, CONV_W - 1, 2 * D_FF), h_p.dtype), norm2_g[l],
                               w_up[l], conv_w[l], conv_b[l], w_down[l])
        kvc_p.append(kv_cmp)
        kvs_p.append(kv_sel)
        kvw_p.append(win_new)
        wkv_p.append(s_new)
        sh_p.append(sh_new)
        cv_p.append(cv_new)
        q, kv_cmp, kv_sel, kv_win, gates, p_shift, p_merge = mixer_inputs(h_s, pos_s, norm1_g[l], w_in[l], q_gain[l], k_gains[l])
        o_att, win_new = nsa_sample(q, kv_cmp, kv_sel, kv_win, gates, pos_s, cache_kv_cmp, cache_kv_sel,
                                    page_table, cache_kv_win[l], w_cmp[l], l)
        y_rw, s_new, sh_new = rwkv_branch(p_shift, state_shift[l], state_wkv[l], *rw)
        h_s = merge_out(h_s, o_att, y_rw, p_merge, w_branch_a[l], w_branch_b[l], w_out[l])
        h_s, cv_new = conv_ffn(h_s, state_conv[l], norm2_g[l], w_up[l], conv_w[l], conv_b[l], w_down[l])
        kvc_s.append(kv_cmp)
        kvs_s.append(kv_sel)
        kvw_s.append(win_new)
        wkv_s.append(s_new)
        sh_s.append(sh_new)
        cv_s.append(cv_new)
    return (h_p, h_s, jnp.stack(kvc_p), jnp.stack(kvc_s), jnp.stack(kvs_p), jnp.stack(kvs_s),
            jnp.stack(kvw_p), jnp.stack(kvw_s), jnp.stack(wkv_p), jnp.stack(wkv_s),
            jnp.stack(sh_p), jnp.stack(sh_s), jnp.stack(cv_p), jnp.stack(cv_s))
```

```python
import functools

import numpy as np
import jax
import jax.numpy as jnp
from jax import lax
from jax.experimental import pallas as pl
from jax.experimental.pallas import tpu as pltpu

f32 = jnp.float32
bf16 = jnp.bfloat16

N_HEADS = 16
N_KV = 4
HPG = N_HEADS // N_KV
HEAD_DIM = 64
ROPE_DIM = HEAD_DIM // 4
ROPE_THETA = 500000.0
BLOCK = 64
N_SEL = 16
WINDOW = 512
RW_HEADS = 16
RW_HEAD = 64
RW_DIM = RW_HEADS * RW_HEAD
LORA_W = 64
LORA_A = 64
LN_X_EPS = 64e-5
CONV_W = 3
NORM_EPS = 1e-6
ATT_DIM = N_HEADS * HEAD_DIM
KV_COLS = 2 * N_KV * HEAD_DIM
K_COLS = N_KV * HEAD_DIM
SHIFT_COLS = 3 * RW_DIM + LORA_W + LORA_A
NEG = -1e30

LANE = 128
VMEM_LIMIT = 56 * 1024 * 1024
QKV_COLS = ATT_DIM + 3 * KV_COLS
ATT_PROJ_COLS = QKV_COLS + LANE
N_NORM_HEADS = QKV_COLS // HEAD_DIM
HIGHEST = lax.Precision.HIGHEST


def _cparams(sem):
    return pltpu.CompilerParams(dimension_semantics=sem, vmem_limit_bytes=VMEM_LIMIT)


def _pick_tile(n, cap, mult):
    best = None
    for t in range(mult, min(n, cap) + 1, mult):
        if n % t == 0:
            best = t
    assert best is not None, (n, cap, mult)
    return best


def _norm_matmul_body(x_ref, g_ref, w_ref, o_ref, xn_ref):
    @pl.when(pl.program_id(1) == 0)
    def _():
        x = x_ref[...]
        ms = jnp.mean(x * x, axis=-1, keepdims=True)
        xn_ref[...] = (x * lax.rsqrt(ms + NORM_EPS) * g_ref[...]).astype(bf16)

    o_ref[...] = jnp.dot(xn_ref[...], w_ref[...], preferred_element_type=f32)


def _norm_matmul(x, gain, w):
    M, D = x.shape
    N = w.shape[1]
    tm = _pick_tile(M, 1024, 8)
    tn = _pick_tile(N, 1024, LANE)
    return pl.pallas_call(
        _norm_matmul_body,
        grid=(M // tm, N // tn),
        in_specs=[pl.BlockSpec((tm, D), lambda i, j: (i, 0)),
                  pl.BlockSpec((1, D), lambda i, j: (0, 0)),
                  pl.BlockSpec((D, tn), lambda i, j: (0, j))],
        out_specs=pl.BlockSpec((tm, tn), lambda i, j: (i, j)),
        out_shape=jax.ShapeDtypeStruct((M, N), f32),
        scratch_shapes=[pltpu.VMEM((tm, D), bf16)],
        compiler_params=_cparams(("parallel", "arbitrary")),
        name="norm_matmul",
    )(x, gain.reshape(1, D), w)


def _qk_post_body(p_ref, cos_ref, sa_ref, sb_ref, gvec_ref, isk_ref, seg_ref, segt_ref, wc_ref,
                  q_ref, kvc_ref, kvs_ref, kvw_ref, gate_ref, *summ_ref):
    y = p_ref[:, :QKV_COLS]
    isk = isk_ref[...] > 0.5
    ss = jnp.dot(y * y, seg_ref[...], precision=HIGHEST, preferred_element_type=f32) * (1.0 / HEAD_DIM)
    rs = lax.rsqrt(ss + NORM_EPS)
    rb = jnp.dot(rs, segt_ref[...], precision=HIGHEST, preferred_element_type=f32)
    yn = jnp.where(isk, y * rb * gvec_ref[...], y)
    reps = QKV_COLS // LANE
    cos = jnp.where(isk, jnp.concatenate([cos_ref[...]] * reps, axis=1), 1.0)
    sa = jnp.where(isk, jnp.concatenate([sa_ref[...]] * reps, axis=1), 0.0)
    sb = jnp.where(isk, jnp.concatenate([sb_ref[...]] * reps, axis=1), 0.0)
    half = ROPE_DIM // 2
    out = (yn * cos + pltpu.roll(yn, QKV_COLS - half, axis=1) * sa + pltpu.roll(yn, half, axis=1) * sb)
    q_ref[...] = (out[:, :ATT_DIM] * (HEAD_DIM ** -0.5)).astype(bf16)
    kvc = out[:, ATT_DIM:ATT_DIM + KV_COLS]
    kvc_ref[...] = kvc
    kvs_ref[...] = out[:, ATT_DIM + KV_COLS:ATT_DIM + 2 * KV_COLS]
    kvw_ref[...] = out[:, ATT_DIM + 2 * KV_COLS:]
    gate_ref[...] = jax.nn.sigmoid(p_ref[:, QKV_COLS:])
    if summ_ref:
        tm = kvc.shape[0]
        blk = kvc.reshape(tm // BLOCK, BLOCK, KV_COLS) * wc_ref[...][None]
        summ_ref[0][0] = jnp.sum(blk, axis=1)


def _qk_post(p_att, tabs, consts, wc, with_summ):
    M = p_att.shape[0]
    cos_t, sa_t, sb_t = tabs
    Tt = cos_t.shape[0]
    tm = _pick_tile(Tt, 256, BLOCK if with_summ else 8)
    nt = Tt // tm
    gvec, isk, seg, segt = consts
    row = lambda i: (i, 0)
    tab = lambda i: (i % nt, 0)
    const = lambda i: (0, 0)
    out_shape = [jax.ShapeDtypeStruct((M, ATT_DIM), bf16),
                 jax.ShapeDtypeStruct((M, KV_COLS), f32),
                 jax.ShapeDtypeStruct((M, KV_COLS), f32),
                 jax.ShapeDtypeStruct((M, KV_COLS), f32),
                 jax.ShapeDtypeStruct((M, LANE), f32)]
    out_specs = [pl.BlockSpec((tm, ATT_DIM), row), pl.BlockSpec((tm, KV_COLS), row),
                 pl.BlockSpec((tm, KV_COLS), row), pl.BlockSpec((tm, KV_COLS), row),
                 pl.BlockSpec((tm, LANE), row)]
    if with_summ:
        out_shape.append(jax.ShapeDtypeStruct((M // tm, tm // BLOCK, KV_COLS), f32))
        out_specs.append(pl.BlockSpec((1, tm // BLOCK, KV_COLS), lambda i: (i, 0, 0)))
    return pl.pallas_call(
        _qk_post_body,
        grid=(M // tm,),
        in_specs=[pl.BlockSpec((tm, ATT_PROJ_COLS), row),
                  pl.BlockSpec((tm, LANE), tab), pl.BlockSpec((tm, LANE), tab), pl.BlockSpec((tm, LANE), tab),
                  pl.BlockSpec((1, QKV_COLS), const), pl.BlockSpec((1, QKV_COLS), const),
                  pl.BlockSpec((QKV_COLS, N_NORM_HEADS), const), pl.BlockSpec((N_NORM_HEADS, QKV_COLS), const),
                  pl.BlockSpec((BLOCK, KV_COLS), const)],
        out_specs=out_specs,
        out_shape=out_shape,
        compiler_params=_cparams(("parallel",)),
        name="qk_post",
    )(p_att, cos_t, sa_t, sb_t, gvec, isk, seg, segt, wc)


def _rope_tables(pos):
    half = ROPE_DIM // 2
    inv = ROPE_THETA ** (-jnp.arange(half, dtype=f32) * 2.0 / ROPE_DIM)
    ang = pos.astype(f32)[:, None] * inv[None, :]
    cos, sin = jnp.cos(ang), jnp.sin(ang)
    n = pos.shape[0]
    ones = jnp.ones((n, HEAD_DIM - ROPE_DIM), f32)
    zeros = jnp.zeros((n, HEAD_DIM - half), f32)
    cos_h = jnp.concatenate([cos, cos, ones], axis=1)
    sa_h = jnp.concatenate([-sin, zeros], axis=1)
    sb_h = jnp.concatenate([jnp.zeros((n, half), f32), sin, jnp.zeros((n, HEAD_DIM - ROPE_DIM), f32)], axis=1)
    rep = LANE // HEAD_DIM
    return tuple(jnp.concatenate([t] * rep, axis=1) for t in (cos_h, sa_h, sb_h))


def _qk_consts(q_gain, k_gains):
    ones_v = jnp.ones((K_COLS,), f32)
    gvec = jnp.concatenate([jnp.tile(q_gain, N_HEADS)]
                           + [t for s in range(3) for t in (jnp.tile(k_gains[s], N_KV), ones_v)])
    isk_np = np.concatenate([np.ones(ATT_DIM)] + [np.ones(K_COLS), np.zeros(K_COLS)] * 3).astype(np.float32)
    seg_np = (np.arange(QKV_COLS)[:, None] // HEAD_DIM == np.arange(N_NORM_HEADS)[None, :]).astype(np.float32)
    seg_np = seg_np * isk_np[:, None]
    return (gvec.reshape(1, QKV_COLS), jnp.asarray(isk_np).reshape(1, QKV_COLS),
            jnp.asarray(seg_np), jnp.asarray(seg_np.T))


def _compress_weights(w_cmp):
    return jnp.concatenate([jnp.tile(w_cmp[c], (1, N_KV)) for c in range(2)], axis=1)


def _select_blocks(imp, n_pick):
    nb = imp.shape[-1]
    lane = lax.broadcasted_iota(jnp.int32, imp.shape, imp.ndim - 1).astype(f32)
    sel = jnp.zeros(imp.shape, f32)
    for _ in range(min(n_pick, nb)):
        mx = jnp.max(imp, axis=-1, keepdims=True)
        idx = jnp.min(jnp.where(imp == mx, lane, float(nb)), axis=-1, keepdims=True)
        hit = (lane == idx) & (mx >= 0.0)
        sel = jnp.where(hit, 1.0, sel)
        imp = jnp.where(lane == idx, -2.0, imp)
    return sel


def _softmax_parts(s, mask):
    s = jnp.where(mask, s, NEG)
    m = jnp.max(s, axis=-1, keepdims=True)
    e = jnp.where(mask, jnp.exp(s - m), 0.0)
    return m, e


def _nt_dot(a, b):
    return lax.dot_general(a, b, (((1,), (1,)), ((), ())), preferred_element_type=f32)


def _nsa_prompt_body(q_ref, kc_ref, vc_ref, ks_ref, vs_ref, kw_ref, vw_ref, g_ref, o_ref, *, tq, T, kc_tile):
    i = pl.program_id(2)
    q0 = i * tq
    R = HPG * tq
    nb = T // BLOCK
    q = q_ref[0].reshape(R, HEAD_DIM)
    row = lax.broadcasted_iota(jnp.int32, (R, 1), 0)
    tpos = q0 + row % tq

    blk = lax.broadcasted_iota(jnp.int32, (R, nb), 1)
    s_c = _nt_dot(q, kc_ref[0, 0].astype(bf16))
    vis = (blk + 1) * BLOCK - 1 <= tpos
    _, e_c = _softmax_parts(s_c, vis)
    p_c = e_c / jnp.maximum(jnp.sum(e_c, axis=-1, keepdims=True), 1e-30)
    o_c = jnp.dot(p_c.astype(bf16), vc_ref[0, 0].astype(bf16), preferred_element_type=f32)

    imp = p_c[0:tq]
    for hh in range(1, HPG):
        imp = imp + p_c[hh * tq:(hh + 1) * tq]
    blk_q = lax.broadcasted_iota(jnp.int32, (tq, nb), 1)
    cur_q = (q0 + lax.broadcasted_iota(jnp.int32, (tq, 1), 0)) // BLOCK
    imp = jnp.where(blk_q < cur_q, imp, -1.0)
    sel = _select_blocks(imp, N_SEL - 1)
    sel = jnp.where(blk_q == cur_q, 1.0, sel)
    sel = jnp.concatenate([sel] * HPG, axis=0).astype(bf16)

    def sel_step(c, carry):
        m, l, acc = carry
        k0 = pl.multiple_of(c * kc_tile, kc_tile)
        k = ks_ref[0, 0, pl.ds(k0, kc_tile), :]
        v = vs_ref[0, 0, pl.ds(k0, kc_tile), :]
        s = _nt_dot(q, k)
        kpos = k0 + lax.broadcasted_iota(jnp.int32, (1, kc_tile), 1)
        expand = (lax.broadcasted_iota(jnp.int32, (nb, kc_tile), 0)
                  == (k0 + lax.broadcasted_iota(jnp.int32, (nb, kc_tile), 1)) // BLOCK)
        selk = jnp.dot(sel, jnp.where(expand, 1.0, 0.0).astype(bf16), preferred_element_type=f32)
        mask = (selk > 0.5) & (kpos <= tpos)
        s = jnp.where(mask, s, NEG)
        m_new = jnp.maximum(m, jnp.max(s, axis=-1, keepdims=True))
        alpha = jnp.exp(m - m_new)
        e = jnp.where(mask, jnp.exp(s - m_new), 0.0)
        l = alpha * l + jnp.sum(e, axis=-1, keepdims=True)
        acc = alpha * acc + jnp.dot(e.astype(bf16), v, preferred_element_type=f32)
        return m_new, l, acc

    n_chunks = (q0 + tq + kc_tile - 1) // kc_tile
    init = (jnp.full((R, 1), NEG, f32), jnp.zeros((R, 1), f32), jnp.zeros((R, HEAD_DIM), f32))
    _, l_s, acc_s = lax.fori_loop(0, n_chunks, sel_step, init)
    o_s = acc_s / jnp.maximum(l_s, 1e-30)

    slab = min(T, WINDOW + tq)
    w0 = pl.multiple_of(jnp.clip(q0 + tq - slab, 0, T - slab), tq)
    kw = kw_ref[0, 0, pl.ds(w0, slab), :]
    vw = vw_ref[0, 0, pl.ds(w0, slab), :]
    dist = tpos - (w0 + lax.broadcasted_iota(jnp.int32, (1, slab), 1))
    _, e_w = _softmax_parts(_nt_dot(q, kw), (dist >= 0) & (dist <= WINDOW))
    o_w = (jnp.dot(e_w.astype(bf16), vw, preferred_element_type=f32)
           / jnp.maximum(jnp.sum(e_w, axis=-1, keepdims=True), 1e-30))

    g = g_ref[0].reshape(R, 3)
    o = g[:, 0:1] * o_c + g[:, 1:2] * o_s + g[:, 2:3] * o_w
    o_ref[0] = jnp.concatenate([o[hh * tq:(hh + 1) * tq] for hh in range(HPG)], axis=1).astype(o_ref.dtype)


def _nsa_prompt(q_hm, kc, vc, ks, vs, kw, vw, gates_hm):
    B, _, T, _ = q_hm.shape
    tq = _pick_tile(T, 128, 16)
    kc_tile = _pick_tile(T, 512, tq)
    nb = T // BLOCK
    kv_spec = pl.BlockSpec((1, 1, T, HEAD_DIM), lambda b, g, i: (b, g, 0, 0))
    c_spec = pl.BlockSpec((1, 1, nb, HEAD_DIM), lambda b, g, i: (b, g, 0, 0))
    return pl.pallas_call(
        functools.partial(_nsa_prompt_body, tq=tq, T=T, kc_tile=kc_tile),
        grid=(B, N_KV, T // tq),
        in_specs=[pl.BlockSpec((1, HPG, tq, HEAD_DIM), lambda b, g, i: (b, g, i, 0)),
                  c_spec, c_spec, kv_spec, kv_spec, kv_spec, kv_spec,
                  pl.BlockSpec((1, HPG, tq, 3), lambda b, g, i: (b, g, i, 0))],
        out_specs=pl.BlockSpec((1, tq, HPG * HEAD_DIM), lambda b, g, i: (b, i, g)),
        out_shape=jax.ShapeDtypeStruct((B, T, ATT_DIM), bf16),
        compiler_params=_cparams(("parallel", "parallel", "arbitrary")),
        name="nsa_prompt",
    )(q_hm, kc, vc, ks, vs, kw, vw, gates_hm)


def _compress_pool_body(pt_ref, *refs, pps):
    wc = refs[pps][...]
    out_ref = refs[pps + 1]
    bpp = out_ref.shape[2]
    for p in range(pps):
        page = refs[p][0]
        out_ref[0, p] = jnp.sum(page.reshape(bpp, BLOCK, KV_COLS) * wc[None], axis=1)


def _compress_pool(pool, page_table, wc, pps):
    DB, NP = page_table.shape
    page = pool.shape[1]
    bpp = page // BLOCK

    def page_spec(p):
        return pl.BlockSpec((1, page, KV_COLS), lambda b, s, pt: (pt[b, s * pps + p], 0, 0))

    return pl.pallas_call(
        functools.partial(_compress_pool_body, pps=pps),
        grid_spec=pltpu.PrefetchScalarGridSpec(
            num_scalar_prefetch=1,
            grid=(DB, NP // pps),
            in_specs=[page_spec(p) for p in range(pps)]
            + [pl.BlockSpec((BLOCK, KV_COLS), lambda b, s, pt: (0, 0))],
            out_specs=pl.BlockSpec((1, pps, bpp, KV_COLS), lambda b, s, pt: (b, s, 0, 0))),
        out_shape=jax.ShapeDtypeStruct((DB, NP, bpp, KV_COLS), f32),
        compiler_params=_cparams(("parallel", "arbitrary")),
        name="compress_pool",
    )(page_table, *([pool] * pps), wc)


def _nsa_sample_body(pt_ref, *refs, pps, past_len, ts):
    pages = refs[:pps]
    (q_ref, g_ref, summ_ref, ns_ref, wb_ref, nw_ref, o_ref,
     sel_ref, m_ref, l_ref, acc_ref, oc_ref) = refs[pps:]
    s_id = pl.program_id(1)
    n_steps = pl.num_programs(1)
    R = HPG * ts
    nbp = summ_ref.shape[1]
    page = pages[0].shape[1]
    kt = pps * page
    row = lax.broadcasted_iota(jnp.int32, (R, 1), 0)
    tpos = past_len + row % ts

    @pl.when(s_id == 0)
    def _():
        blk = lax.broadcasted_iota(jnp.int32, (R, nbp), 1)
        vis = (blk + 1) * BLOCK - 1 <= tpos
        for g in range(N_KV):
            q = q_ref[0, g]
            kc = summ_ref[0, :, g * HEAD_DIM:(g + 1) * HEAD_DIM].astype(bf16)
            vc = summ_ref[0, :, K_COLS + g * HEAD_DIM:K_COLS + (g + 1) * HEAD_DIM].astype(bf16)
            _, e_c = _softmax_parts(_nt_dot(q, kc), vis)
            p_c = e_c / jnp.maximum(jnp.sum(e_c, axis=-1, keepdims=True), 1e-30)
            oc_ref[g] = jnp.dot(p_c.astype(bf16), vc, preferred_element_type=f32)
            imp = p_c[0:ts]
            for hh in range(1, HPG):
                imp = imp + p_c[hh * ts:(hh + 1) * ts]
            imp = jnp.where(blk[0:ts] < tpos[0:ts] // BLOCK, imp, -1.0)
            sel = _select_blocks(imp, N_SEL - 1)
            sel_ref[g] = jnp.concatenate([sel] * HPG, axis=0)
        m_ref[...] = jnp.full(m_ref.shape, NEG, f32)
        l_ref[...] = jnp.zeros(l_ref.shape, f32)
        acc_ref[...] = jnp.zeros(acc_ref.shape, f32)

    def online_update(g, s, mask, v):
        s = jnp.where(mask, s, NEG)
        m_old = m_ref[g]
        m_new = jnp.maximum(m_old, jnp.max(s, axis=-1, keepdims=True))
        alpha = jnp.exp(m_old - m_new)
        e = jnp.where(mask, jnp.exp(s - m_new), 0.0)
        l_ref[g] = alpha * l_ref[g] + jnp.sum(e, axis=-1, keepdims=True)
        acc_ref[g] = alpha * acc_ref[g] + jnp.dot(e.astype(bf16), v, preferred_element_type=f32)
        m_ref[g] = m_new

    k0 = s_id * kt
    kpos = k0 + lax.broadcasted_iota(jnp.int32, (1, kt), 1)
    expand = (lax.broadcasted_iota(jnp.int32, (nbp, kt), 0)
              == (k0 + lax.broadcasted_iota(jnp.int32, (nbp, kt), 1)) // BLOCK)
    expand = jnp.where(expand, 1.0, 0.0).astype(bf16)
    for g in range(N_KV):
        q = q_ref[0, g]
        k = jnp.concatenate([pages[p][0, :, g * HEAD_DIM:(g + 1) * HEAD_DIM] for p in range(pps)],
                            axis=0).astype(bf16)
        v = jnp.concatenate([pages[p][0, :, K_COLS + g * HEAD_DIM:K_COLS + (g + 1) * HEAD_DIM]
                             for p in range(pps)], axis=0).astype(bf16)
        selk = jnp.dot(sel_ref[g].astype(bf16), expand, preferred_element_type=f32)
        online_update(g, _nt_dot(q, k), (selk > 0.5) & (kpos <= tpos), v)

    @pl.when(s_id == n_steps - 1)
    def _():
        npos = past_len + lax.broadcasted_iota(jnp.int32, (1, ts), 1)
        keep = wb_ref.shape[1]
        wpos = past_len - keep + lax.broadcasted_iota(jnp.int32, (1, keep), 1)
        for g in range(N_KV):
            q = q_ref[0, g]
            ksl = slice(g * HEAD_DIM, (g + 1) * HEAD_DIM)
            vsl = slice(K_COLS + g * HEAD_DIM, K_COLS + (g + 1) * HEAD_DIM)
            online_update(g, _nt_dot(q, ns_ref[0, :, ksl].astype(bf16)), npos <= tpos,
                          ns_ref[0, :, vsl].astype(bf16))
            o_s = acc_ref[g] / jnp.maximum(l_ref[g], 1e-30)
            d_old = tpos - wpos
            d_new = tpos - npos
            s_old = jnp.where((d_old >= 0) & (d_old <= WINDOW) & (wpos >= 0),
                              _nt_dot(q, wb_ref[0, :, ksl].astype(bf16)), NEG)
            mk_new = (d_new >= 0) & (d_new <= WINDOW)
            s_new = jnp.where(mk_new, _nt_dot(q, nw_ref[0, :, ksl].astype(bf16)), NEG)
            m = jnp.maximum(jnp.max(s_old, axis=-1, keepdims=True), jnp.max(s_new, axis=-1, keepdims=True))
            e_old = jnp.where((d_old >= 0) & (d_old <= WINDOW) & (wpos >= 0), jnp.exp(s_old - m), 0.0)
            e_new = jnp.where(mk_new, jnp.exp(s_new - m), 0.0)
            den = jnp.sum(e_old, axis=-1, keepdims=True) + jnp.sum(e_new, axis=-1, keepdims=True)
            o_w = (jnp.dot(e_old.astype(bf16), wb_ref[0, :, vsl].astype(bf16), preferred_element_type=f32)
                   + jnp.dot(e_new.astype(bf16), nw_ref[0, :, vsl].astype(bf16), preferred_element_type=f32)
                   ) / jnp.maximum(den, 1e-30)
            gt = g_ref[0, g]
            o_ref[0, g] = gt[:, 0:1] * oc_ref[g] + gt[:, 1:2] * o_s + gt[:, 2:3] * o_w


def _nsa_sample(q_g, gates_g, summ, new_sel, win_buf, new_win, pool_sel, page_table, pps, past_len):
    DB, NP = page_table.shape
    ts = new_sel.shape[1]
    R = HPG * ts
    nbp = summ.shape[1]
    page = pool_sel.shape[1]
    keep = win_buf.shape[1]

    def page_spec(p):
        return pl.BlockSpec((1, page, KV_COLS), lambda b, s, pt: (pt[b, s * pps + p], 0, 0))

    per_b4 = lambda b, s, pt: (b, 0, 0, 0)
    per_b3 = lambda b, s, pt: (b, 0, 0)
    return pl.pallas_call(
        functools.partial(_nsa_sample_body, pps=pps, past_len=past_len, ts=ts),
        grid_spec=pltpu.PrefetchScalarGridSpec(
            num_scalar_prefetch=1,
            grid=(DB, NP // pps),
            in_specs=[page_spec(p) for p in range(pps)]
            + [pl.BlockSpec((1, N_KV, R, HEAD_DIM), per_b4),
               pl.BlockSpec((1, N_KV, R, 3), per_b4),
               pl.BlockSpec((1, nbp, KV_COLS), per_b3),
               pl.BlockSpec((1, ts, KV_COLS), per_b3),
               pl.BlockSpec((1, keep, KV_COLS), per_b3),
               pl.BlockSpec((1, ts, KV_COLS), per_b3)],
            out_specs=pl.BlockSpec((1, N_KV, R, HEAD_DIM), per_b4),
            scratch_shapes=[pltpu.VMEM((N_KV, R, nbp), f32),
                            pltpu.VMEM((N_KV, R, 1), f32),
                            pltpu.VMEM((N_KV, R, 1), f32),
                            pltpu.VMEM((N_KV, R, HEAD_DIM), f32),
                            pltpu.VMEM((N_KV, R, HEAD_DIM), f32)]),
        out_shape=jax.ShapeDtypeStruct((DB, N_KV, R, HEAD_DIM), f32),
        compiler_params=_cparams(("parallel", "arbitrary")),
        name="nsa_sample",
    )(page_table, *([pool_sel] * pps), q_g, gates_g, summ, new_sel, win_buf, new_win)


def _mm(a, b, dims=(((1,), (0,)), ((), ()))):
    return lax.dot_general(a.astype(bf16), b.astype(bf16), dims, preferred_element_type=f32)


_NT = (((1,), (1,)), ((), ()))
_TN = (((0,), (0,)), ((), ()))


def _unit_lower_solve(L, rhs, C, bs):
    ri = lax.broadcasted_iota(jnp.int32, (C, C), 0)
    ci = lax.broadcasted_iota(jnp.int32, (C, C), 1)
    same = (ri // bs) == (ci // bs)
    eye = jnp.where(ri == ci, 1.0, 0.0)
    D = jnp.where(same, L, 0.0)
    T = eye - D
    P = D
    n = 2
    while n < bs:
        P = _mm(P, P)
        T = T + _mm(T, P)
        n *= 2
    x = _mm(T, rhs)
    nblk = C // bs
    if nblk == 1:
        return x
    Mb = _mm(T, jnp.where(same, 0.0, L))
    factors = []
    Pm = Mb
    n = 2
    while n < nblk:
        Pm = _mm(Pm, Pm)
        factors.append(Pm)
        n *= 2
    for Pm in reversed(factors):
        x = x + _mm(Pm, x)
    return x - _mm(Mb, x)


def _rwkv_body(ps_ref, sp_ref, s0_ref, mu_ref, w0_ref, ww_ref, a0_ref, wa_ref, kk_ref, ka_ref, rk_ref,
               lnw_ref, lnb_ref, y_ref, so_ref, carry_ref, state_ref, *, C, bs):
    c = pl.program_id(1)

    @pl.when(c == 0)
    def _():
        carry_ref[...] = sp_ref[0]
        state_ref[...] = s0_ref[0]

    ps = ps_ref[0]
    rowi = lax.broadcasted_iota(jnp.int32, (C, 1), 0)
    prev = jnp.where(rowi == 0, carry_ref[...], pltpu.roll(ps, 1, axis=0))
    carry_ref[...] = ps[C - 1:C, :]
    z = ps + (prev - ps) * mu_ref[...]
    r = z[:, 0:RW_DIM]
    k = z[:, RW_DIM:2 * RW_DIM]
    v = z[:, 2 * RW_DIM:3 * RW_DIM]
    xw = z[:, 3 * RW_DIM:3 * RW_DIM + LORA_W]
    xa = z[:, 3 * RW_DIM + LORA_W:]
    u = -(w0_ref[...] + jnp.dot(jnp.tanh(xw).astype(bf16), ww_ref[...], preferred_element_type=f32))
    softplus = jnp.maximum(u, 0.0) + jnp.log(1.0 + jnp.exp(-jnp.abs(u)))
    lw = -jnp.exp(-softplus - 0.5)
    a = jax.nn.sigmoid(a0_ref[...] + jnp.dot(xa.astype(bf16), wa_ref[...], preferred_element_type=f32))
    kk = k * kk_ref[...]
    k2 = k * (1.0 + (a - 1.0) * ka_ref[...])
    rk = r * k2 * rk_ref[...]

    ri = lax.broadcasted_iota(jnp.int32, (C, C), 0)
    ci = lax.broadcasted_iota(jnp.int32, (C, C), 1)
    tril = jnp.where(ci <= ri, 1.0, 0.0)
    G = jnp.dot(tril, lw, precision=HIGHEST, preferred_element_type=f32)
    g_end = G[C - 1:C, :]
    e_in = jnp.exp(G)
    e_prev = jnp.exp(G - lw)
    e_neg = jnp.exp(-G)
    e_end = jnp.exp(g_end - G)
    e_tot = jnp.exp(g_end)
    strict = ci < ri
    incl = ci <= ri

    for h in range(RW_HEADS):
        sl = slice(h * RW_HEAD, (h + 1) * RW_HEAD)
        kk_h = kk[:, sl]
        kk_h = kk_h * lax.rsqrt(jnp.maximum(jnp.sum(kk_h * kk_h, axis=-1, keepdims=True), 1e-24))
        b_h = kk_h * a[:, sl]
        k_h, v_h, r_h = k2[:, sl], v[:, sl], r[:, sl]
        x1 = jnp.concatenate([kk_h * e_prev[:, sl], r_h * e_in[:, sl]], axis=0)
        x2 = jnp.concatenate([k_h * e_neg[:, sl], b_h * e_neg[:, sl]], axis=0)
        A = _mm(x1, x2, _NT)
        S = state_ref[h]
        P = _mm(x1, S, _NT)
        a_kk = jnp.where(strict, A[:C, :C], 0.0)
        a_kb = jnp.where(strict, A[:C, C:], 0.0)
        rhs = P[:C] + _mm(a_kk, v_h)
        sa = _unit_lower_solve(a_kb, rhs, C, bs)
        a_r = jnp.concatenate([jnp.where(incl, A[C:, :C], 0.0), jnp.where(incl, -A[C:, C:], 0.0)], axis=1)
        vs = jnp.concatenate([v_h, sa], axis=0)
        y = P[C:] + _mm(a_r, vs)
        kb = jnp.concatenate([k_h * e_end[:, sl], -(b_h * e_end[:, sl])], axis=0)
        state_ref[h] = S * e_tot[:, sl] + _mm(vs, kb, _TN)
        mean = jnp.mean(y, axis=-1, keepdims=True)
        var = jnp.mean(jnp.square(y - mean), axis=-1, keepdims=True)
        yn = (y - mean) * lax.rsqrt(var + LN_X_EPS) * lnw_ref[:, sl] + lnb_ref[:, sl]
        y_ref[0, :, sl] = (yn + jnp.sum(rk[:, sl], axis=-1, keepdims=True) * v_h).astype(y_ref.dtype)

    so_ref[0] = state_ref[...]


def _rwkv(p_shift, shift_prev, s0, rw):
    mu, w0, w_lora_w, a0, w_lora_a, k_k, k_a, r_k, ln_w, ln_b = rw
    B, T, _ = p_shift.shape
    C = _pick_tile(T, 64, 8)
    bs = min(16, C)
    vec = lambda n: pl.BlockSpec((1, n), lambda b, c: (0, 0))
    row = lambda t: t.reshape(1, -1)
    y, s_new = pl.pallas_call(
        functools.partial(_rwkv_body, C=C, bs=bs),
        grid=(B, T // C),
        in_specs=[pl.BlockSpec((1, C, SHIFT_COLS), lambda b, c: (b, c, 0)),
                  pl.BlockSpec((1, 1, SHIFT_COLS), lambda b, c: (b, 0, 0)),
                  pl.BlockSpec((1, RW_HEADS, RW_HEAD, RW_HEAD), lambda b, c: (b, 0, 0, 0)),
                  vec(SHIFT_COLS), vec(RW_DIM),
                  pl.BlockSpec((LORA_W, RW_DIM), lambda b, c: (0, 0)),
                  vec(RW_DIM),
                  pl.BlockSpec((LORA_A, RW_DIM), lambda b, c: (0, 0)),
                  vec(RW_DIM), vec(RW_DIM), vec(RW_DIM), vec(RW_DIM), vec(RW_DIM)],
        out_specs=[pl.BlockSpec((1, C, RW_DIM), lambda b, c: (b, c, 0)),
                   pl.BlockSpec((1, RW_HEADS, RW_HEAD, RW_HEAD), lambda b, c: (b, 0, 0, 0))],
        out_shape=[jax.ShapeDtypeStruct((B, T, RW_DIM), bf16),
                   jax.ShapeDtypeStruct((B, RW_HEADS, RW_HEAD, RW_HEAD), f32)],
        scratch_shapes=[pltpu.VMEM((1, SHIFT_COLS), f32),
                        pltpu.VMEM((RW_HEADS, RW_HEAD, RW_HEAD), f32)],
        compiler_params=_cparams(("parallel", "arbitrary")),
        name="rwkv",
    )(p_shift, shift_prev.reshape(B, 1, SHIFT_COLS), s0, row(mu), row(w0), w_lora_w.astype(bf16), row(a0),
      w_lora_a.astype(bf16), row(k_k), row(k_a), row(r_k), row(ln_w), row(ln_b))
    return y, s_new


def _merge_body(x_ref, oa_ref, yr_ref, ga_ref, gb_ref, wa_ref, wb_ref, wo_ref, h_ref):
    ma = jnp.dot(oa_ref[...], wa_ref[...], preferred_element_type=f32)
    mb = jnp.dot(yr_ref[...], wb_ref[...], preferred_element_type=f32)
    m = jax.nn.sigmoid(ga_ref[...]) * ma + jax.nn.sigmoid(gb_ref[...]) * mb
    h_ref[...] = x_ref[...] + jnp.dot(m.astype(bf16), wo_ref[...], preferred_element_type=f32)


def _merge(x, o_att, y_rw, p_merge, w_a, w_b, w_o):
    M, D = x.shape
    tm = _pick_tile(M, 256, 8)
    row = lambda i: (i, 0)
    const = lambda i: (0, 0)
    return pl.pallas_call(
        _merge_body,
        grid=(M // tm,),
        in_specs=[pl.BlockSpec((tm, D), row),
                  pl.BlockSpec((tm, ATT_DIM), row),
                  pl.BlockSpec((tm, RW_DIM), row),
                  pl.BlockSpec((tm, D), lambda i: (i, 0)),
                  pl.BlockSpec((tm, D), lambda i: (i, 1)),
                  pl.BlockSpec((ATT_DIM, D), const),
                  pl.BlockSpec((RW_DIM, D), const),
                  pl.BlockSpec((D, D), const)],
        out_specs=pl.BlockSpec((tm, D), row),
        out_shape=jax.ShapeDtypeStruct((M, D), f32),
        compiler_params=_cparams(("parallel",)),
        name="merge",
    )(x, o_att, y_rw, p_merge, p_merge, w_a, w_b, w_o)


def _conv_ffn_body(h_ref, g_ref, wug_ref, wuv_ref, cwg_ref, cwv_ref, cbg_ref, cbv_ref, wd_ref,
                   pg_ref, pv_ref, y_ref, tg_ref, tv_ref, hn_ref, cg_ref, cv_ref, *, tm, seq_rows, tail):
    i = pl.program_id(1)
    j = pl.program_id(2)
    carried = tm <= seq_rows

    @pl.when(j == 0)
    def _():
        h = h_ref[0]
        ms = jnp.mean(h * h, axis=-1, keepdims=True)
        hn_ref[...] = (h * lax.rsqrt(ms + NORM_EPS) * g_ref[...]).astype(bf16)
        y_ref[0] = h

    hn = hn_ref[...]
    rowi = lax.broadcasted_iota(jnp.int32, (tm, 1), 0)
    t_in = rowi % seq_rows

    def conv(u, cw_ref, cb_ref, prev_ref, carry_ref):
        if carried:
            @pl.when(i == 0)
            def _():
                carry_ref[j] = prev_ref[0]
            p2 = carry_ref[j, 0:1, :]
            p1 = carry_ref[j, 1:2, :]
            u1 = jnp.where(rowi == 0, p1, pltpu.roll(u, 1, axis=0))
            u2 = jnp.where(rowi == 0, p2, jnp.where(rowi == 1, p1, pltpu.roll(u, 2, axis=0)))
            carry_ref[j] = u[tm - 2:tm, :]
        else:
            pr = prev_ref[0]
            u1 = jnp.where(t_in == 0, pltpu.roll(pr, tm - 1, axis=0), pltpu.roll(u, 1, axis=0))
            u2 = jnp.where(t_in == 0, pr, jnp.where(t_in == 1, pr, pltpu.roll(u, 2, axis=0)))
        return cb_ref[...] + cw_ref[0:1, :] * u2 + cw_ref[1:2, :] * u1 + cw_ref[2:3, :] * u

    ug = jnp.dot(hn, wug_ref[...], preferred_element_type=f32)
    uv = jnp.dot(hn, wuv_ref[...], preferred_element_type=f32)
    tg_ref[0] = ug[tm - tail:tm, :]
    tv_ref[0] = uv[tm - tail:tm, :]
    gate = conv(ug, cwg_ref, cbg_ref, pg_ref, cg_ref)
    val = conv(uv, cwv_ref, cbv_ref, pv_ref, cv_ref)
    act = (gate * jax.nn.sigmoid(gate) * val).astype(bf16)
    y_ref[0] += jnp.dot(act, wd_ref[...], preferred_element_type=f32)


def _conv_ffn(h, conv_prev, norm_g, w_up, conv_w, conv_b, w_down, *, fold):
    B, T, D = h.shape
    dff = w_down.shape[0]
    tf = _pick_tile(dff, 512, LANE)
    nf = dff // tf
    if not fold:
        nb_, tm = B, _pick_tile(T, 512, 8)
        tail = 8
        hh = h
        prev = conv_prev
        prev_spec_g = pl.BlockSpec((1, CONV_W - 1, tf), lambda b, i, j: (b, 0, j))
        prev_spec_v = pl.BlockSpec((1, CONV_W - 1, tf), lambda b, i, j: (b, 0, nf + j))
    else:
        nb_, tm = 1, B * T
        tail = tm
        hh = h.reshape(1, B * T, D)
        assert T >= CONV_W - 1
        prev = jnp.concatenate([conv_prev, jnp.zeros((B, T - (CONV_W - 1), 2 * dff), f32)],
                               axis=1).reshape(1, B * T, 2 * dff)
        prev_spec_g = pl.BlockSpec((1, tm, tf), lambda b, i, j: (0, 0, j))
        prev_spec_v = pl.BlockSpec((1, tm, tf), lambda b, i, j: (0, 0, nf + j))
    nt = hh.shape[1] // tm
    body = functools.partial(_conv_ffn_body, tm=tm, seq_rows=T, tail=tail)
    cw = conv_w
    cb = conv_b.reshape(1, 2 * dff)
    tail_spec = pl.BlockSpec((1, tail, tf), lambda b, i, j: (b, 0, j))
    y, ug, uv = pl.pallas_call(
        body,
        grid=(nb_, nt, nf),
        in_specs=[pl.BlockSpec((1, tm, D), lambda b, i, j: (b, i, 0)),
                  pl.BlockSpec((1, D), lambda b, i, j: (0, 0)),
                  pl.BlockSpec((D, tf), lambda b, i, j: (0, j)),
                  pl.BlockSpec((D, tf), lambda b, i, j: (0, nf + j)),
                  pl.BlockSpec((CONV_W, tf), lambda b, i, j: (0, j)),
                  pl.BlockSpec((CONV_W, tf), lambda b, i, j: (0, nf + j)),
                  pl.BlockSpec((1, tf), lambda b, i, j: (0, j)),
                  pl.BlockSpec((1, tf), lambda b, i, j: (0, nf + j)),
                  pl.BlockSpec((tf, D), lambda b, i, j: (j, 0)),
                  prev_spec_g, prev_spec_v],
        out_specs=[pl.BlockSpec((1, tm, D), lambda b, i, j: (b, i, 0)), tail_spec, tail_spec],
        out_shape=[jax.ShapeDtypeStruct(hh.shape, f32),
                   jax.ShapeDtypeStruct((nb_, tail, dff), f32),
                   jax.ShapeDtypeStruct((nb_, tail, dff), f32)],
        scratch_shapes=[pltpu.VMEM((tm, D), bf16),
                        pltpu.VMEM((nf, CONV_W - 1, tf), f32),
                        pltpu.VMEM((nf, CONV_W - 1, tf), f32)],
        compiler_params=_cparams(("parallel", "arbitrary", "arbitrary")),
        name="conv_ffn",
    )(hh, norm_g.reshape(1, D), w_up, w_up, cw, cw, cb, cb, w_down, prev, prev)
    return y, ug, uv


def _split_w_in(w_in):
    o = 0
    parts = []
    for n in (ATT_DIM, KV_COLS, KV_COLS, KV_COLS, 3 * N_HEADS, SHIFT_COLS, 2 * w_in.shape[0]):
        parts.append(w_in[:, o:o + n])
        o += n
    wq, wc, ws, ww, wg, wsh, wm = parts
    wg = jnp.pad(wg, ((0, 0), (0, LANE - 3 * N_HEADS)))
    w_att = jnp.concatenate([wq, wc, ws, ww, wg], axis=1).astype(bf16)
    return w_att, wsh.astype(bf16), wm.astype(bf16)


def _head_major(x, n):
    B, T, _ = x.shape
    return x.reshape(B, T, n, HEAD_DIM).transpose(0, 2, 1, 3)


def _mixer_inputs(x2d, pos_tab, norm_g, w_parts, consts, wc, with_summ):
    w_att, w_sh, w_mg = w_parts
    p_att = _norm_matmul(x2d, norm_g, w_att)
    p_shift = _norm_matmul(x2d, norm_g, w_sh)
    p_merge = _norm_matmul(x2d, norm_g, w_mg)
    post = _qk_post(p_att, pos_tab, consts, wc, with_summ)
    return post, p_shift, p_merge


def kernel(x_prompt, x_sample, cache_kv_cmp, cache_kv_sel, page_table, cache_kv_win, state_wkv, state_shift, state_conv, norm1_g, w_in, q_gain, k_gains, w_cmp, mu_shift, w0, w_lora_w, a0, w_lora_a, k_k, k_a, r_k, ln_x_w, ln_x_b, w_branch_a, w_branch_b, w_out, norm2_g, w_up, conv_w, conv_b, w_down):
    B, T, D = x_prompt.shape
    DB, TS, _ = x_sample.shape
    depth = w_in.shape[0]
    assert depth == 1, "single-layer trunk"
    l = 0
    page = cache_kv_cmp.shape[2]
    n_pages = page_table.shape[1]
    past_len = n_pages * page
    assert past_len % BLOCK == 0 and TS <= BLOCK and page % BLOCK == 0 and T % BLOCK == 0
    dff = w_down.shape[1]

    w_parts = _split_w_in(w_in[l])
    consts = _qk_consts(q_gain[l], k_gains[l])
    wc = _compress_weights(w_cmp[l])
    rw = (mu_shift[l], w0[l], w_lora_w[l], a0[l], w_lora_a[l], k_k[l], k_a[l], r_k[l], ln_x_w[l], ln_x_b[l])
    w_a = w_branch_a[l].astype(bf16)
    w_b = w_branch_b[l].astype(bf16)
    w_o = w_out[l].astype(bf16)
    w_u = w_up[l].astype(bf16)
    w_d = w_down[l].astype(bf16)

    xp = x_prompt.reshape(B * T, D)
    tabs_p = _rope_tables(jnp.arange(T, dtype=jnp.int32))
    (q, kvc, kvs, kvw, gates, summ), p_shift, p_merge = _mixer_inputs(xp, tabs_p, norm1_g[l], w_parts, consts, wc, True)
    nb = T // BLOCK
    summ = summ.reshape(B, nb, KV_COLS)
    q_hm = _head_major(q.reshape(B, T, ATT_DIM), N_HEADS)
    gates_hm = gates[:, :3 * N_HEADS].reshape(B, T, N_HEADS, 3).transpose(0, 2, 1, 3)
    kvs3 = kvs.reshape(B, T, KV_COLS)
    kvw3 = kvw.reshape(B, T, KV_COLS)
    o_att = _nsa_prompt(q_hm,
                        _head_major(summ[:, :, :K_COLS], N_KV), _head_major(summ[:, :, K_COLS:], N_KV),
                        _head_major(kvs3[:, :, :K_COLS].astype(bf16), N_KV),
                        _head_major(kvs3[:, :, K_COLS:].astype(bf16), N_KV),
                        _head_major(kvw3[:, :, :K_COLS].astype(bf16), N_KV),
                        _head_major(kvw3[:, :, K_COLS:].astype(bf16), N_KV),
                        gates_hm)
    p_shift3 = p_shift.reshape(B, T, SHIFT_COLS)
    y_rw, wkv_p = _rwkv(p_shift3, jnp.zeros((B, SHIFT_COLS), f32),
                        jnp.zeros((B, RW_HEADS, RW_HEAD, RW_HEAD), f32), rw)
    h_p = _merge(xp, o_att.reshape(B * T, ATT_DIM), y_rw.reshape(B * T, RW_DIM), p_merge, w_a, w_b, w_o)
    y_p, ug, uv = _conv_ffn(h_p.reshape(B, T, D), jnp.zeros((B, CONV_W - 1, 2 * dff), f32), norm2_g[l],
                            w_u, conv_w[l], conv_b[l], w_d, fold=False)
    assert T >= CONV_W - 1
    conv_p = jnp.concatenate([ug[:, -(CONV_W - 1):], uv[:, -(CONV_W - 1):]], axis=-1)
    kv_shape_p = (1, B, T, 2, N_KV, HEAD_DIM)
    keep_p = min(WINDOW, T)
    outs_p = (y_p,
              kvc.reshape(kv_shape_p), kvs.reshape(kv_shape_p),
              kvw3[:, T - keep_p:].reshape(1, B, keep_p, 2, N_KV, HEAD_DIM),
              wkv_p[None], p_shift3[:, -1][None], conv_p[None])

    xs = x_sample.reshape(DB * TS, D)
    pos_s = past_len + jnp.arange(TS, dtype=jnp.int32)
    tabs_s = tuple(jnp.tile(t, (DB, 1)) for t in _rope_tables(pos_s))
    (q, kvc_s, kvs_s, kvw_s, gates), p_shift, p_merge = _mixer_inputs(xs, tabs_s, norm1_g[l], w_parts, consts, wc, False)
    pps = _pick_tile(n_pages, 8, 1)
    pool_c = cache_kv_cmp[l].reshape(-1, page, KV_COLS)
    pool_s = cache_kv_sel[l].reshape(-1, page, KV_COLS)
    summ_s = _compress_pool(pool_c, page_table, wc, pps).reshape(DB, past_len // BLOCK, KV_COLS)
    R = HPG * TS
    q_g = _head_major(q.reshape(DB, TS, ATT_DIM), N_HEADS).reshape(DB, N_KV, R, HEAD_DIM)
    gates_g = (gates[:, :3 * N_HEADS].reshape(DB, TS, N_HEADS, 3).transpose(0, 2, 1, 3)
               .reshape(DB, N_KV, R, 3))
    win_buf = cache_kv_win[l].reshape(DB, -1, KV_COLS)
    keep = win_buf.shape[1]
    kvw_s3 = kvw_s.reshape(DB, TS, KV_COLS)
    o_g = _nsa_sample(q_g, gates_g, summ_s, kvs_s.reshape(DB, TS, KV_COLS), win_buf, kvw_s3,
                      pool_s, page_table, pps, past_len)
    o_att_s = (o_g.reshape(DB, N_HEADS, TS, HEAD_DIM).transpose(0, 2, 1, 3)
               .reshape(DB * TS, ATT_DIM).astype(bf16))
    p_shift3s = p_shift.reshape(DB, TS, SHIFT_COLS)
    y_rw_s, wkv_s = _rwkv(p_shift3s, state_shift[l], state_wkv[l], rw)
    h_s = _merge(xs, o_att_s, y_rw_s.reshape(DB * TS, RW_DIM), p_merge, w_a, w_b, w_o)
    y_s, ug, uv = _conv_ffn(h_s.reshape(DB, TS, D), state_conv[l], norm2_g[l],
                            w_u, conv_w[l], conv_b[l], w_d, fold=True)
    up_s = jnp.concatenate([state_conv[l],
                            jnp.concatenate([ug.reshape(DB, TS, dff), uv.reshape(DB, TS, dff)], axis=-1)], axis=1)
    conv_s = up_s[:, TS:]
    win_s = jnp.concatenate([win_buf, kvw_s3], axis=1)[:, TS:]
    kv_shape_s = (1, DB, TS, 2, N_KV, HEAD_DIM)

    return (outs_p[0], y_s.reshape(DB, TS, D),
            outs_p[1], kvc_s.reshape(kv_shape_s),
            outs_p[2], kvs_s.reshape(kv_shape_s),
            outs_p[3], win_s.reshape(1, DB, keep, 2, N_KV, HEAD_DIM),
            outs_p[4], wkv_s[None],
            outs_p[5], p_shift3s[:, -1][None],
            outs_p[6], conv_s[None])
```

```python
import functools

import numpy as np
import jax
import jax.numpy as jnp
from jax import lax
from jax.experimental import pallas as pl
from jax.experimental.pallas import tpu as pltpu

f32 = jnp.float32
bf16 = jnp.bfloat16

N_HEADS = 16
N_KV = 4
HPG = N_HEADS // N_KV
HEAD_DIM = 64
ROPE_DIM = HEAD_DIM // 4
ROPE_THETA = 500000.0
BLOCK = 64
N_SEL = 16
WINDOW = 512
RW_HEADS = 16
RW_HEAD = 64
RW_DIM = RW_HEADS * RW_HEAD
LORA_W = 64
LORA_A = 64
LN_X_EPS = 64e-5
CONV_W = 3
NORM_EPS = 1e-6
ATT_DIM = N_HEADS * HEAD_DIM
KV_COLS = 2 * N_KV * HEAD_DIM
K_COLS = N_KV * HEAD_DIM
SHIFT_COLS = 3 * RW_DIM + LORA_W + LORA_A
NEG = -1e30

LANE = 128
VMEM_LIMIT = 56 * 1024 * 1024
QKV_COLS = ATT_DIM + 3 * KV_COLS
ATT_PROJ_COLS = QKV_COLS + LANE
N_NORM_HEADS = QKV_COLS // HEAD_DIM
HIGHEST = lax.Precision.HIGHEST
RWKV_BATCH_PER_STEP = 2


def _cparams(sem):
    return pltpu.CompilerParams(dimension_semantics=sem, vmem_limit_bytes=VMEM_LIMIT)


def _pick_tile(n, cap, mult):
    best = None
    for t in range(mult, min(n, cap) + 1, mult):
        if n % t == 0:
            best = t
    assert best is not None, (n, cap, mult)
    return best


def _norm_matmul_body(x_ref, g_ref, w_ref, o_ref, xn_ref):
    @pl.when(pl.program_id(1) == 0)
    def _():
        x = x_ref[...]
        ms = jnp.mean(x * x, axis=-1, keepdims=True)
        xn_ref[...] = (x * lax.rsqrt(ms + NORM_EPS) * g_ref[...]).astype(bf16)

    o_ref[...] = jnp.dot(xn_ref[...], w_ref[...], preferred_element_type=f32)


def _norm_matmul(x, gain, w):
    M, D = x.shape
    N = w.shape[1]
    tm = _pick_tile(M, 1024, 8)
    tn = _pick_tile(N, 1024, LANE)
    return pl.pallas_call(
        _norm_matmul_body,
        grid=(M // tm, N // tn),
        in_specs=[pl.BlockSpec((tm, D), lambda i, j: (i, 0)),
                  pl.BlockSpec((1, D), lambda i, j: (0, 0)),
                  pl.BlockSpec((D, tn), lambda i, j: (0, j))],
        out_specs=pl.BlockSpec((tm, tn), lambda i, j: (i, j)),
        out_shape=jax.ShapeDtypeStruct((M, N), f32),
        scratch_shapes=[pltpu.VMEM((tm, D), bf16)],
        compiler_params=_cparams(("parallel", "arbitrary")),
        name="norm_matmul",
    )(x, gain.reshape(1, D), w)


def _qk_post_body(p_ref, cos_ref, sa_ref, sb_ref, gvec_ref, isk_ref, seg_ref, segt_ref, wc_ref,
                  q_ref, kvc_ref, kvs_ref, kvw_ref, gate_ref, *prompt_refs, nt):
    y = p_ref[:, :QKV_COLS]
    isk = isk_ref[...] > 0.5
    ss = jnp.dot(y * y, seg_ref[...], precision=HIGHEST, preferred_element_type=f32) * (1.0 / HEAD_DIM)
    rs = lax.rsqrt(ss + NORM_EPS)
    rb = jnp.dot(rs, segt_ref[...], precision=HIGHEST, preferred_element_type=f32)
    yn = jnp.where(isk, y * rb * gvec_ref[...], y)
    reps = QKV_COLS // LANE
    cos = jnp.where(isk, jnp.concatenate([cos_ref[...]] * reps, axis=1), 1.0)
    sa = jnp.where(isk, jnp.concatenate([sa_ref[...]] * reps, axis=1), 0.0)
    sb = jnp.where(isk, jnp.concatenate([sb_ref[...]] * reps, axis=1), 0.0)
    half = ROPE_DIM // 2
    out = (yn * cos + pltpu.roll(yn, QKV_COLS - half, axis=1) * sa + pltpu.roll(yn, half, axis=1) * sb)
    qs = out[:, :ATT_DIM] * (HEAD_DIM ** -0.5)
    kvc = out[:, ATT_DIM:ATT_DIM + KV_COLS]
    kvs = out[:, ATT_DIM + KV_COLS:ATT_DIM + 2 * KV_COLS]
    kvw = out[:, ATT_DIM + 2 * KV_COLS:]
    kvc_ref[...] = kvc
    kvs_ref[...] = kvs
    kvw_ref[...] = kvw
    gate_ref[...] = jax.nn.sigmoid(p_ref[:, QKV_COLS:])
    if not prompt_refs:
        q_ref[...] = qs.astype(bf16)
        return
    summ_ref, ksa_ref, vsp_ref, kwp_ref, vwp_ref = prompt_refs
    tm = kvc.shape[0]
    blk = kvc.reshape(tm // BLOCK, BLOCK, KV_COLS) * wc_ref[...][None]
    summ_ref[0] = jnp.sum(blk, axis=1)
    zeros = jnp.zeros((tm, LANE - HEAD_DIM), f32)
    t0 = (pl.program_id(0) % nt) * tm
    blk_of_row = (t0 + lax.broadcasted_iota(jnp.int32, zeros.shape, 0)) // BLOCK
    onehot = jnp.where(lax.broadcasted_iota(jnp.int32, zeros.shape, 1) == blk_of_row, 1.0, 0.0)
    hd = lambda x, h: x[:, h * HEAD_DIM:(h + 1) * HEAD_DIM]
    pad = lambda x, tail: jnp.concatenate([x, tail], axis=1).astype(bf16)
    for h in range(N_HEADS):
        q_ref[0, h] = pad(hd(qs, h), zeros)
    for g in range(N_KV):
        ksa_ref[0, g] = pad(hd(kvs, g), onehot)
        vsp_ref[0, g] = pad(hd(kvs, N_KV + g), zeros)
        kwp_ref[0, g] = pad(hd(kvw, g), zeros)
        vwp_ref[0, g] = pad(hd(kvw, N_KV + g), zeros)


def _qk_post(p_att, tabs, consts, wc, prompt):
    M = p_att.shape[0]
    cos_t, sa_t, sb_t = tabs
    Tt = cos_t.shape[0]
    tm = _pick_tile(Tt, 256, BLOCK if prompt else 8)
    nt = Tt // tm
    gvec, isk, seg, segt = consts
    row = lambda i: (i, 0)
    tab = lambda i: (i % nt, 0)
    const = lambda i: (0, 0)
    kv_shape = jax.ShapeDtypeStruct((M, KV_COLS), f32)
    kv_spec = pl.BlockSpec((tm, KV_COLS), row)
    out_shape = [jax.ShapeDtypeStruct((M, ATT_DIM), bf16), kv_shape, kv_shape, kv_shape,
                 jax.ShapeDtypeStruct((M, LANE), f32)]
    out_specs = [pl.BlockSpec((tm, ATT_DIM), row), kv_spec, kv_spec, kv_spec, pl.BlockSpec((tm, LANE), row)]
    if prompt:
        assert Tt // BLOCK <= LANE - HEAD_DIM, "one-hot block lanes"
        B = M // Tt
        hm = lambda n: jax.ShapeDtypeStruct((B, n, Tt, LANE), bf16)
        hm_spec = lambda n: pl.BlockSpec((1, n, tm, LANE), lambda i: (i // nt, 0, i % nt, 0))
        out_shape[0], out_specs[0] = hm(N_HEADS), hm_spec(N_HEADS)
        out_shape += [jax.ShapeDtypeStruct((M // tm, tm // BLOCK, KV_COLS), f32)] + [hm(N_KV)] * 4
        out_specs += [pl.BlockSpec((1, tm // BLOCK, KV_COLS), lambda i: (i, 0, 0))] + [hm_spec(N_KV)] * 4
    return pl.pallas_call(
        functools.partial(_qk_post_body, nt=nt),
        grid=(M // tm,),
        in_specs=[pl.BlockSpec((tm, ATT_PROJ_COLS), row),
                  pl.BlockSpec((tm, LANE), tab), pl.BlockSpec((tm, LANE), tab), pl.BlockSpec((tm, LANE), tab),
                  pl.BlockSpec((1, QKV_COLS), const), pl.BlockSpec((1, QKV_COLS), const),
                  pl.BlockSpec((QKV_COLS, N_NORM_HEADS), const), pl.BlockSpec((N_NORM_HEADS, QKV_COLS), const),
                  pl.BlockSpec((BLOCK, KV_COLS), const)],
        out_specs=out_specs,
        out_shape=out_shape,
        compiler_params=_cparams(("parallel",)),
        name="qk_post",
    )(p_att, cos_t, sa_t, sb_t, gvec, isk, seg, segt, wc)


def _rope_tables(pos):
    half = ROPE_DIM // 2
    inv = ROPE_THETA ** (-jnp.arange(half, dtype=f32) * 2.0 / ROPE_DIM)
    ang = pos.astype(f32)[:, None] * inv[None, :]
    cos, sin = jnp.cos(ang), jnp.sin(ang)
    n = pos.shape[0]
    ones = jnp.ones((n, HEAD_DIM - ROPE_DIM), f32)
    zeros = jnp.zeros((n, HEAD_DIM - half), f32)
    cos_h = jnp.concatenate([cos, cos, ones], axis=1)
    sa_h = jnp.concatenate([-sin, zeros], axis=1)
    sb_h = jnp.concatenate([jnp.zeros((n, half), f32), sin, jnp.zeros((n, HEAD_DIM - ROPE_DIM), f32)], axis=1)
    rep = LANE // HEAD_DIM
    return tuple(jnp.concatenate([t] * rep, axis=1) for t in (cos_h, sa_h, sb_h))


def _qk_consts(q_gain, k_gains):
    ones_v = jnp.ones((K_COLS,), f32)
    gvec = jnp.concatenate([jnp.tile(q_gain, N_HEADS)]
                           + [t for s in range(3) for t in (jnp.tile(k_gains[s], N_KV), ones_v)])
    isk_np = np.concatenate([np.ones(ATT_DIM)] + [np.ones(K_COLS), np.zeros(K_COLS)] * 3).astype(np.float32)
    seg_np = (np.arange(QKV_COLS)[:, None] // HEAD_DIM == np.arange(N_NORM_HEADS)[None, :]).astype(np.float32)
    seg_np = seg_np * isk_np[:, None]
    return (gvec.reshape(1, QKV_COLS), jnp.asarray(isk_np).reshape(1, QKV_COLS),
            jnp.asarray(seg_np), jnp.asarray(seg_np.T))


def _compress_weights(w_cmp):
    return jnp.concatenate([jnp.tile(w_cmp[c], (1, N_KV)) for c in range(2)], axis=1)


def _select_blocks(imp, n_pick):
    nb = imp.shape[-1]
    lane = lax.broadcasted_iota(jnp.int32, imp.shape, imp.ndim - 1).astype(f32)
    sel = jnp.zeros(imp.shape, f32)
    for _ in range(min(n_pick, nb)):
        mx = jnp.max(imp, axis=-1, keepdims=True)
        idx = jnp.min(jnp.where(imp == mx, lane, float(nb)), axis=-1, keepdims=True)
        hit = (lane == idx) & (mx >= 0.0)
        sel = jnp.where(hit, 1.0, sel)
        imp = jnp.where(lane == idx, -2.0, imp)
    return sel


def _softmax_parts(s, mask):
    s = jnp.where(mask, s, NEG)
    m = jnp.max(s, axis=-1, keepdims=True)
    e = jnp.where(mask, jnp.exp(s - m), 0.0)
    return m, e


def _nt_dot(a, b):
    return lax.dot_general(a, b, (((1,), (1,)), ((), ())), preferred_element_type=f32)


def _nsa_prompt_body(q_ref, kc_ref, vc_ref, ks_ref, vs_ref, kw_ref, vw_ref, g_ref, gx_ref, place_ref,
                     sb_ref, wb_ref, o_ref, *, tq, T, kc_tile, slab):
    i = pl.program_id(2)
    q0 = i * tq
    R = HPG * tq
    nb = T // BLOCK
    tn_dims = (((0,), (0,)), ((), ()))
    q = q_ref[0].reshape(R, LANE)

    tpos = q0 + lax.broadcasted_iota(jnp.int32, (1, R), 1) % tq
    blk = lax.broadcasted_iota(jnp.int32, (nb, R), 0)
    vis = (blk + 1) * BLOCK - 1 <= tpos
    s_c = jnp.where(vis, _nt_dot(kc_ref[0, 0], q), NEG)
    e_c = jnp.where(vis, jnp.exp(s_c - jnp.max(s_c, axis=0, keepdims=True)), 0.0)
    p_c = e_c / jnp.maximum(jnp.sum(e_c, axis=0, keepdims=True), 1e-30)
    o_c = lax.dot_general(p_c.astype(bf16), vc_ref[0, 0], tn_dims, preferred_element_type=f32)

    imp = p_c[:, 0:tq]
    for hh in range(1, HPG):
        imp = imp + p_c[:, hh * tq:(hh + 1) * tq]
    blk_q = lax.broadcasted_iota(jnp.int32, (nb, tq), 0)
    cur_q = (q0 + lax.broadcasted_iota(jnp.int32, (1, tq), 1)) // BLOCK
    cand = blk_q < cur_q
    imp = jnp.where(cand, imp, -1.0)
    ahead = jnp.zeros((nb, tq), f32)
    for m in range(nb):
        row_m = imp[m:m + 1, :]
        tie = jnp.where(blk_q > m, 1.0, 0.0)
        ahead = ahead + jnp.where(row_m > imp, 1.0, jnp.where(row_m == imp, tie, 0.0))
    sel = jnp.where(cand, jnp.where(ahead < N_SEL - 1, 1.0, 0.0), jnp.where(blk_q == cur_q, 1.0, 0.0))
    sel_l = lax.dot_general(sel.astype(bf16), place_ref[...], tn_dims, preferred_element_type=f32)
    lane = lax.broadcasted_iota(jnp.int32, (1, LANE), 1)
    in_blk_lanes = jnp.where((lane >= HEAD_DIM) & (lane < HEAD_DIM + nb), 1.0, 0.0)
    q_off = ((in_blk_lanes - sel_l) * NEG).astype(bf16)
    q_sel = q + jnp.concatenate([q_off] * HPG, axis=0)

    def attend(carry, qq, k, v, bias):
        m, l, acc = carry
        s = _nt_dot(qq, k)
        if bias is not None:
            s = (s.reshape(HPG, tq, s.shape[-1]) + bias[None]).reshape(s.shape)
        m_new = jnp.maximum(m, jnp.max(s, axis=-1, keepdims=True))
        alpha = jnp.exp(m - m_new)
        e = jnp.exp(s - m_new)
        l = alpha * l + jnp.sum(e, axis=-1, keepdims=True)
        acc = alpha * acc + jnp.dot(e.astype(bf16), v, preferred_element_type=f32)
        return m_new, l, acc

    init = (jnp.full((R, 1), NEG, f32), jnp.zeros((R, 1), f32), jnp.zeros((R, LANE), f32))

    def sel_step(c, carry):
        k0 = pl.multiple_of(c * kc_tile, kc_tile)
        return attend(carry, q_sel, ks_ref[0, 0, pl.ds(k0, kc_tile), :], vs_ref[0, 0, pl.ds(k0, kc_tile), :],
                      None)

    n_full = q0 // kc_tile
    carry = lax.fori_loop(0, n_full, sel_step, init)
    kd = pl.multiple_of(n_full * kc_tile, kc_tile)
    _, l_s, acc_s = attend(carry, q_sel, ks_ref[0, 0, pl.ds(kd, kc_tile), :],
                           vs_ref[0, 0, pl.ds(kd, kc_tile), :], sb_ref[0])
    o_s = acc_s / jnp.maximum(l_s, 1e-30)

    w0 = pl.multiple_of(jnp.clip(q0 + tq - slab, 0, T - slab), tq)
    _, l_w, acc_w = attend(init, q, kw_ref[0, 0, pl.ds(w0, slab), :], vw_ref[0, 0, pl.ds(w0, slab), :],
                           wb_ref[0])
    o_w = acc_w / jnp.maximum(l_w, 1e-30)

    gt = g_ref[0]
    g_hi = gt.astype(bf16)
    g_lo = (gt - g_hi.astype(f32)).astype(bf16)
    G = (jnp.dot(g_hi, gx_ref[0], preferred_element_type=f32)
         + jnp.dot(g_lo, gx_ref[0], preferred_element_type=f32))
    outs = []
    for hh in range(HPG):
        rs = slice(hh * tq, (hh + 1) * tq)
        gcol = lambda j: G[:, (hh * 3 + j) * LANE:(hh * 3 + j + 1) * LANE]
        o_h = gcol(0) * o_c[rs] + gcol(1) * o_s[rs] + gcol(2) * o_w[rs]
        outs.append(o_h[:, :HEAD_DIM])
    o_ref[0] = jnp.concatenate(outs, axis=1).astype(o_ref.dtype)


def _nsa_prompt(q_pad, kc_pad, vc_pad, ks_aug, vs_pad, kw_pad, vw_pad, gates):
    B, _, T, _ = q_pad.shape
    tq = _pick_tile(T, 128, 16)
    kc_tile = _pick_tile(T, 512, tq)
    slab = min(T, WINDOW + tq)
    nb = T // BLOCK
    nq = T // tq
    place = np.zeros((nb, LANE), np.float32)
    place[np.arange(nb), HEAD_DIM + np.arange(nb)] = 1.0
    gx = np.zeros((N_KV, LANE, 3 * HPG, LANE), np.float32)
    for g in range(N_KV):
        for c in range(3 * HPG):
            gx[g, g * 3 * HPG + c, c, :] = 1.0
    gx = gx.reshape(N_KV, LANE, 3 * HPG * LANE)
    r = np.arange(tq)[:, None]
    nrel = kc_tile // tq
    sel_bias = np.stack([np.where(np.arange(kc_tile)[None, :] <= rel * tq + r, 0.0, NEG) for rel in range(nrel)])
    n_wb = min(nq, WINDOW // tq + 1) if slab == WINDOW + tq else nq
    win_bias = []
    for i in range(n_wb):
        w0 = min(max(i * tq + tq - slab, 0), T - slab)
        dist = (i * tq + r) - (w0 + np.arange(slab)[None, :])
        win_bias.append(np.where((dist >= 0) & (dist <= WINDOW), 0.0, NEG))
    win_bias = np.stack(win_bias)
    kv_spec = pl.BlockSpec((1, 1, T, LANE), lambda b, g, i: (b, g, 0, 0))
    c_spec = pl.BlockSpec((1, 1, nb, LANE), lambda b, g, i: (b, g, 0, 0))
    return pl.pallas_call(
        functools.partial(_nsa_prompt_body, tq=tq, T=T, kc_tile=kc_tile, slab=slab),
        grid=(B, N_KV, nq),
        in_specs=[pl.BlockSpec((1, HPG, tq, LANE), lambda b, g, i: (b, g, i, 0)),
                  c_spec, c_spec, kv_spec, kv_spec, kv_spec, kv_spec,
                  pl.BlockSpec((1, tq, LANE), lambda b, g, i: (b, i, 0)),
                  pl.BlockSpec((1, LANE, 3 * HPG * LANE), lambda b, g, i: (g, 0, 0)),
                  pl.BlockSpec((nb, LANE), lambda b, g, i: (0, 0)),
                  pl.BlockSpec((1, tq, kc_tile), lambda b, g, i: (i % nrel, 0, 0)),
                  pl.BlockSpec((1, tq, slab), lambda b, g, i: (jnp.minimum(i, n_wb - 1), 0, 0))],
        out_specs=pl.BlockSpec((1, tq, HPG * HEAD_DIM), lambda b, g, i: (b, i, g)),
        out_shape=jax.ShapeDtypeStruct((B, T, ATT_DIM), bf16),
        compiler_params=_cparams(("parallel", "parallel", "arbitrary")),
        name="nsa_prompt",
    )(q_pad, kc_pad, vc_pad, ks_aug, vs_pad, kw_pad, vw_pad, gates,
      jnp.asarray(gx, bf16), jnp.asarray(place, bf16), jnp.asarray(sel_bias, f32), jnp.asarray(win_bias, f32))


def _compress_pool_body(pt_ref, *refs, pps):
    wc = refs[pps][...]
    out_ref = refs[pps + 1]
    bpp = out_ref.shape[2]
    for p in range(pps):
        page = refs[p][0]
        out_ref[0, p] = jnp.sum(page.reshape(bpp, BLOCK, KV_COLS) * wc[None], axis=1)


def _compress_pool(pool, page_table, wc, pps):
    DB, NP = page_table.shape
    page = pool.shape[1]
    bpp = page // BLOCK

    def page_spec(p):
        return pl.BlockSpec((1, page, KV_COLS), lambda b, s, pt: (pt[b, s * pps + p], 0, 0))

    return pl.pallas_call(
        functools.partial(_compress_pool_body, pps=pps),
        grid_spec=pltpu.PrefetchScalarGridSpec(
            num_scalar_prefetch=1,
            grid=(DB, NP // pps),
            in_specs=[page_spec(p) for p in range(pps)]
            + [pl.BlockSpec((BLOCK, KV_COLS), lambda b, s, pt: (0, 0))],
            out_specs=pl.BlockSpec((1, pps, bpp, KV_COLS), lambda b, s, pt: (b, s, 0, 0))),
        out_shape=jax.ShapeDtypeStruct((DB, NP, bpp, KV_COLS), f32),
        compiler_params=_cparams(("parallel", "arbitrary")),
        name="compress_pool",
    )(page_table, *([pool] * pps), wc)


def _nsa_sample_body(pt_ref, *refs, pps, past_len, ts):
    pages = refs[:pps]
    (q_ref, g_ref, summ_ref, ns_ref, wb_ref, nw_ref, o_ref,
     sel_ref, m_ref, l_ref, acc_ref, oc_ref) = refs[pps:]
    s_id = pl.program_id(1)
    n_steps = pl.num_programs(1)
    R = HPG * ts
    nbp = summ_ref.shape[1]
    page = pages[0].shape[1]
    kt = pps * page
    row = lax.broadcasted_iota(jnp.int32, (R, 1), 0)
    tpos = past_len + row % ts

    @pl.when(s_id == 0)
    def _():
        blk = lax.broadcasted_iota(jnp.int32, (R, nbp), 1)
        vis = (blk + 1) * BLOCK - 1 <= tpos
        for g in range(N_KV):
            q = q_ref[0, g]
            kc = summ_ref[0, :, g * HEAD_DIM:(g + 1) * HEAD_DIM].astype(bf16)
            vc = summ_ref[0, :, K_COLS + g * HEAD_DIM:K_COLS + (g + 1) * HEAD_DIM].astype(bf16)
            _, e_c = _softmax_parts(_nt_dot(q, kc), vis)
            p_c = e_c / jnp.maximum(jnp.sum(e_c, axis=-1, keepdims=True), 1e-30)
            oc_ref[g] = jnp.dot(p_c.astype(bf16), vc, preferred_element_type=f32)
            imp = p_c[0:ts]
            for hh in range(1, HPG):
                imp = imp + p_c[hh * ts:(hh + 1) * ts]
            imp = jnp.where(blk[0:ts] < tpos[0:ts] // BLOCK, imp, -1.0)
            sel = _select_blocks(imp, N_SEL - 1)
            sel_ref[g] = jnp.concatenate([sel] * HPG, axis=0)
        m_ref[...] = jnp.full(m_ref.shape, NEG, f32)
        l_ref[...] = jnp.zeros(l_ref.shape, f32)
        acc_ref[...] = jnp.zeros(acc_ref.shape, f32)

    def online_update(g, s, mask, v):
        s = jnp.where(mask, s, NEG)
        m_old = m_ref[g]
        m_new = jnp.maximum(m_old, jnp.max(s, axis=-1, keepdims=True))
        alpha = jnp.exp(m_old - m_new)
        e = jnp.where(mask, jnp.exp(s - m_new), 0.0)
        l_ref[g] = alpha * l_ref[g] + jnp.sum(e, axis=-1, keepdims=True)
        acc_ref[g] = alpha * acc_ref[g] + jnp.dot(e.astype(bf16), v, preferred_element_type=f32)
        m_ref[g] = m_new

    k0 = s_id * kt
    kpos = k0 + lax.broadcasted_iota(jnp.int32, (1, kt), 1)
    expand = (lax.broadcasted_iota(jnp.int32, (nbp, kt), 0)
              == (k0 + lax.broadcasted_iota(jnp.int32, (nbp, kt), 1)) // BLOCK)
    expand = jnp.where(expand, 1.0, 0.0).astype(bf16)
    for g in range(N_KV):
        q = q_ref[0, g]
        k = jnp.concatenate([pages[p][0, :, g * HEAD_DIM:(g + 1) * HEAD_DIM] for p in range(pps)],
                            axis=0).astype(bf16)
        v = jnp.concatenate([pages[p][0, :, K_COLS + g * HEAD_DIM:K_COLS + (g + 1) * HEAD_DIM]
                             for p in range(pps)], axis=0).astype(bf16)
        selk = jnp.dot(sel_ref[g].astype(bf16), expand, preferred_element_type=f32)
        online_update(g, _nt_dot(q, k), (selk > 0.5) & (kpos <= tpos), v)

    @pl.when(s_id == n_steps - 1)
    def _():
        npos = past_len + lax.broadcasted_iota(jnp.int32, (1, ts), 1)
        keep = wb_ref.shape[1]
        wpos = past_len - keep + lax.broadcasted_iota(jnp.int32, (1, keep), 1)
        for g in range(N_KV):
            q = q_ref[0, g]
            ksl = slice(g * HEAD_DIM, (g + 1) * HEAD_DIM)
            vsl = slice(K_COLS + g * HEAD_DIM, K_COLS + (g + 1) * HEAD_DIM)
            online_update(g, _nt_dot(q, ns_ref[0, :, ksl].astype(bf16)), npos <= tpos,
                          ns_ref[0, :, vsl].astype(bf16))
            o_s = acc_ref[g] / jnp.maximum(l_ref[g], 1e-30)
            d_old = tpos - wpos
            d_new = tpos - npos
            s_old = jnp.where((d_old >= 0) & (d_old <= WINDOW) & (wpos >= 0),
                              _nt_dot(q, wb_ref[0, :, ksl].astype(bf16)), NEG)
            mk_new = (d_new >= 0) & (d_new <= WINDOW)
            s_new = jnp.where(mk_new, _nt_dot(q, nw_ref[0, :, ksl].astype(bf16)), NEG)
            m = jnp.maximum(jnp.max(s_old, axis=-1, keepdims=True), jnp.max(s_new, axis=-1, keepdims=True))
            e_old = jnp.where((d_old >= 0) & (d_old <= WINDOW) & (wpos >= 0), jnp.exp(s_old - m), 0.0)
            e_new = jnp.where(mk_new, jnp.exp(s_new - m), 0.0)
            den = jnp.sum(e_old, axis=-1, keepdims=True) + jnp.sum(e_new, axis=-1, keepdims=True)
            o_w = (jnp.dot(e_old.astype(bf16), wb_ref[0, :, vsl].astype(bf16), preferred_element_type=f32)
                   + jnp.dot(e_new.astype(bf16), nw_ref[0, :, vsl].astype(bf16), preferred_element_type=f32)
                   ) / jnp.maximum(den, 1e-30)
            gt = g_ref[0, g]
            o_ref[0, g] = gt[:, 0:1] * oc_ref[g] + gt[:, 1:2] * o_s + gt[:, 2:3] * o_w


def _nsa_sample(q_g, gates_g, summ, new_sel, win_buf, new_win, pool_sel, page_table, pps, past_len):
    DB, NP = page_table.shape
    ts = new_sel.shape[1]
    R = HPG * ts
    nbp = summ.shape[1]
    page = pool_sel.shape[1]
    keep = win_buf.shape[1]

    def page_spec(p):
        return pl.BlockSpec((1, page, KV_COLS), lambda b, s, pt: (pt[b, s * pps + p], 0, 0))

    per_b4 = lambda b, s, pt: (b, 0, 0, 0)
    per_b3 = lambda b, s, pt: (b, 0, 0)
    return pl.pallas_call(
        functools.partial(_nsa_sample_body, pps=pps, past_len=past_len, ts=ts),
        grid_spec=pltpu.PrefetchScalarGridSpec(
            num_scalar_prefetch=1,
            grid=(DB, NP // pps),
            in_specs=[page_spec(p) for p in range(pps)]
            + [pl.BlockSpec((1, N_KV, R, HEAD_DIM), per_b4),
               pl.BlockSpec((1, N_KV, R, 3), per_b4),
               pl.BlockSpec((1, nbp, KV_COLS), per_b3),
               pl.BlockSpec((1, ts, KV_COLS), per_b3),
               pl.BlockSpec((1, keep, KV_COLS), per_b3),
               pl.BlockSpec((1, ts, KV_COLS), per_b3)],
            out_specs=pl.BlockSpec((1, N_KV, R, HEAD_DIM), per_b4),
            scratch_shapes=[pltpu.VMEM((N_KV, R, nbp), f32),
                            pltpu.VMEM((N_KV, R, 1), f32),
                            pltpu.VMEM((N_KV, R, 1), f32),
                            pltpu.VMEM((N_KV, R, HEAD_DIM), f32),
                            pltpu.VMEM((N_KV, R, HEAD_DIM), f32)]),
        out_shape=jax.ShapeDtypeStruct((DB, N_KV, R, HEAD_DIM), f32),
        compiler_params=_cparams(("parallel", "arbitrary")),
        name="nsa_sample",
    )(page_table, *([pool_sel] * pps), q_g, gates_g, summ, new_sel, win_buf, new_win)


def _bmm(spec, a, b):
    return jnp.einsum(spec, a.astype(bf16), b.astype(bf16), preferred_element_type=f32)


def _unit_lower_solve(L, rhs, C, bs):
    _mm = functools.partial(_bmm, "hij,hjk->hik")
    ri = lax.broadcasted_iota(jnp.int32, (1, C, C), 1)
    ci = lax.broadcasted_iota(jnp.int32, (1, C, C), 2)
    same = (ri // bs) == (ci // bs)
    eye = jnp.where(ri == ci, 1.0, 0.0)
    D = jnp.where(same, L, 0.0)
    T = eye - D
    P = D
    n = 2
    while n < bs:
        P = _mm(P, P)
        T = T + _mm(T, P)
        n *= 2
    x = _mm(T, rhs)
    nblk = C // bs
    if nblk == 1:
        return x
    Mb = _mm(T, jnp.where(same, 0.0, L))
    factors = []
    Pm = Mb
    n = 2
    while n < nblk:
        Pm = _mm(Pm, Pm)
        factors.append(Pm)
        n *= 2
    for Pm in reversed(factors):
        x = x + _mm(Pm, x)
    return x - _mm(Mb, x)


def _rwkv_body(ps_ref, sp_ref, s0_ref, mu_ref, w0_ref, ww_ref, a0_ref, wa_ref, kk_ref, ka_ref, rk_ref,
               lnw_ref, lnb_ref, seg_ref, segt_ref, y_ref, so_ref, carry_ref, state_ref, *, C, bs):
    c = pl.program_id(1)
    nbat = ps_ref.shape[0]
    H = RW_HEADS

    @pl.when(c == 0)
    def _():
        carry_ref[...] = sp_ref[...]
        state_ref[...] = s0_ref[...].reshape(state_ref.shape)

    def heads(x):
        return [x[:, h * RW_HEAD:(h + 1) * RW_HEAD] for h in range(H)]

    ri = lax.broadcasted_iota(jnp.int32, (C, C), 0)
    ci = lax.broadcasted_iota(jnp.int32, (C, C), 1)
    tril = jnp.where(ci <= ri, 1.0, 0.0)
    rowi = lax.broadcasted_iota(jnp.int32, (C, 1), 0)

    x1_l, x2_l, kb_l, v_l, rk_l, etot_l = [], [], [], [], [], []
    for n in range(nbat):
        ps = ps_ref[n]
        prev = jnp.where(rowi == 0, carry_ref[n], pltpu.roll(ps, 1, axis=0))
        carry_ref[n] = ps[C - 1:C, :]
        z = ps + (prev - ps) * mu_ref[...]
        r = z[:, 0:RW_DIM]
        k = z[:, RW_DIM:2 * RW_DIM]
        v = z[:, 2 * RW_DIM:3 * RW_DIM]
        xw = z[:, 3 * RW_DIM:3 * RW_DIM + LORA_W]
        xa = z[:, 3 * RW_DIM + LORA_W:]
        u = -(w0_ref[...] + jnp.dot(jnp.tanh(xw).astype(bf16), ww_ref[...], preferred_element_type=f32))
        softplus = jnp.maximum(u, 0.0) + jnp.log(1.0 + jnp.exp(-jnp.abs(u)))
        lw = -jnp.exp(-softplus - 0.5)
        a = jax.nn.sigmoid(a0_ref[...] + jnp.dot(xa.astype(bf16), wa_ref[...], preferred_element_type=f32))
        kk = k * kk_ref[...]
        ss = jnp.dot(kk * kk, seg_ref[...], precision=HIGHEST, preferred_element_type=f32)
        kk = kk * jnp.dot(lax.rsqrt(jnp.maximum(ss, 1e-24)), segt_ref[...], precision=HIGHEST,
                          preferred_element_type=f32)
        bb = kk * a
        k2 = k * (1.0 + (a - 1.0) * ka_ref[...])
        G = jnp.dot(tril, lw, precision=HIGHEST, preferred_element_type=f32)
        g_end = G[C - 1:C, :]
        e_neg = jnp.exp(-G)
        e_end = jnp.exp(g_end - G)
        x1_l.append(heads(jnp.concatenate([kk * jnp.exp(G - lw), r * jnp.exp(G)], axis=0).astype(bf16)))
        x2_l.append(heads(jnp.concatenate([k2 * e_neg, bb * e_neg], axis=0).astype(bf16)))
        kb_l.append(heads(jnp.concatenate([k2 * e_end, -(bb * e_end)], axis=0).astype(bf16)))
        v_l.append(heads(v))
        rk_l.append(heads(r * k2 * rk_ref[...]))
        etot_l.append(heads(jnp.exp(g_end)))

    stack = lambda lst: jnp.stack([t for per_b in lst for t in per_b], axis=0)
    X1, X2, KB = stack(x1_l), stack(x2_l), stack(kb_l)
    V, RK, ETOT = stack(v_l), stack(rk_l), stack(etot_l)

    strict = (ci < ri)[None]
    incl = (ci <= ri)[None]
    S = state_ref[...]
    A = _bmm("hck,hdk->hcd", X1, X2)
    P = _bmm("hck,hvk->hcv", X1, S)
    a_kk = jnp.where(strict, A[:, :C, :C], 0.0)
    a_kb = jnp.where(strict, A[:, :C, C:], 0.0)
    rhs = P[:, :C] + _bmm("hcd,hdv->hcv", a_kk, V)
    sa = _unit_lower_solve(a_kb, rhs, C, bs)
    a_r = jnp.concatenate([jnp.where(incl, A[:, C:, :C], 0.0), jnp.where(incl, -A[:, C:, C:], 0.0)], axis=2)
    vs = jnp.concatenate([V, sa], axis=1)
    y = P[:, C:] + _bmm("hcd,hdv->hcv", a_r, vs)
    state_ref[...] = S * ETOT + _bmm("hcv,hck->hvk", vs, KB)
    mean = jnp.mean(y, axis=-1, keepdims=True)
    var = jnp.mean(jnp.square(y - mean), axis=-1, keepdims=True)
    yn = (y - mean) * lax.rsqrt(var + LN_X_EPS)
    bonus = jnp.sum(RK, axis=-1, keepdims=True) * V
    for n in range(nbat):
        for h in range(H):
            sl = slice(h * RW_HEAD, (h + 1) * RW_HEAD)
            i = n * H + h
            y_ref[n, :, sl] = (yn[i] * lnw_ref[:, sl] + lnb_ref[:, sl] + bonus[i]).astype(y_ref.dtype)

    so_ref[...] = state_ref[...].reshape(so_ref.shape)


def _rwkv(p_shift, shift_prev, s0, rw):
    mu, w0, w_lora_w, a0, w_lora_a, k_k, k_a, r_k, ln_w, ln_b = rw
    B, T, _ = p_shift.shape
    C = _pick_tile(T, 64, 8)
    bs = min(16, C)
    nbat = _pick_tile(B, RWKV_BATCH_PER_STEP, 1)
    seg_np = (np.arange(RW_DIM)[:, None] // RW_HEAD == np.arange(RW_HEADS)[None, :]).astype(np.float32)
    vec = lambda n: pl.BlockSpec((1, n), lambda b, c: (0, 0))
    row = lambda t: t.reshape(1, -1)
    y, s_new = pl.pallas_call(
        functools.partial(_rwkv_body, C=C, bs=bs),
        grid=(B // nbat, T // C),
        in_specs=[pl.BlockSpec((nbat, C, SHIFT_COLS), lambda b, c: (b, c, 0)),
                  pl.BlockSpec((nbat, 1, SHIFT_COLS), lambda b, c: (b, 0, 0)),
                  pl.BlockSpec((nbat, RW_HEADS, RW_HEAD, RW_HEAD), lambda b, c: (b, 0, 0, 0)),
                  vec(SHIFT_COLS), vec(RW_DIM),
                  pl.BlockSpec((LORA_W, RW_DIM), lambda b, c: (0, 0)),
                  vec(RW_DIM),
                  pl.BlockSpec((LORA_A, RW_DIM), lambda b, c: (0, 0)),
                  vec(RW_DIM), vec(RW_DIM), vec(RW_DIM), vec(RW_DIM), vec(RW_DIM),
                  pl.BlockSpec((RW_DIM, RW_HEADS), lambda b, c: (0, 0)),
                  pl.BlockSpec((RW_HEADS, RW_DIM), lambda b, c: (0, 0))],
        out_specs=[pl.BlockSpec((nbat, C, RW_DIM), lambda b, c: (b, c, 0)),
                   pl.BlockSpec((nbat, RW_HEADS, RW_HEAD, RW_HEAD), lambda b, c: (b, 0, 0, 0))],
        out_shape=[jax.ShapeDtypeStruct((B, T, RW_DIM), bf16),
                   jax.ShapeDtypeStruct((B, RW_HEADS, RW_HEAD, RW_HEAD), f32)],
        scratch_shapes=[pltpu.VMEM((nbat, 1, SHIFT_COLS), f32),
                        pltpu.VMEM((nbat * RW_HEADS, RW_HEAD, RW_HEAD), f32)],
        compiler_params=_cparams(("parallel", "arbitrary")),
        name="rwkv",
    )(p_shift, shift_prev.reshape(B, 1, SHIFT_COLS), s0, row(mu), row(w0), w_lora_w.astype(bf16), row(a0),
      w_lora_a.astype(bf16), row(k_k), row(k_a), row(r_k), row(ln_w), row(ln_b),
      jnp.asarray(seg_np), jnp.asarray(seg_np.T))
    return y, s_new


def _merge_body(x_ref, oa_ref, yr_ref, ga_ref, gb_ref, wa_ref, wb_ref, wo_ref, h_ref):
    ma = jnp.dot(oa_ref[...], wa_ref[...], preferred_element_type=f32)
    mb = jnp.dot(yr_ref[...], wb_ref[...], preferred_element_type=f32)
    m = jax.nn.sigmoid(ga_ref[...]) * ma + jax.nn.sigmoid(gb_ref[...]) * mb
    h_ref[...] = x_ref[...] + jnp.dot(m.astype(bf16), wo_ref[...], preferred_element_type=f32)


def _merge(x, o_att, y_rw, p_merge, w_a, w_b, w_o):
    M, D = x.shape
    tm = _pick_tile(M, 256, 8)
    row = lambda i: (i, 0)
    const = lambda i: (0, 0)
    return pl.pallas_call(
        _merge_body,
        grid=(M // tm,),
        in_specs=[pl.BlockSpec((tm, D), row),
                  pl.BlockSpec((tm, ATT_DIM), row),
                  pl.BlockSpec((tm, RW_DIM), row),
                  pl.BlockSpec((tm, D), lambda i: (i, 0)),
                  pl.BlockSpec((tm, D), lambda i: (i, 1)),
                  pl.BlockSpec((ATT_DIM, D), const),
                  pl.BlockSpec((RW_DIM, D), const),
                  pl.BlockSpec((D, D), const)],
        out_specs=pl.BlockSpec((tm, D), row),
        out_shape=jax.ShapeDtypeStruct((M, D), f32),
        compiler_params=_cparams(("parallel",)),
        name="merge",
    )(x, o_att, y_rw, p_merge, p_merge, w_a, w_b, w_o)


def _conv_ffn_body(h_ref, g_ref, wug_ref, wuv_ref, cwg_ref, cwv_ref, cbg_ref, cbv_ref, wd_ref,
                   pg_ref, pv_ref, y_ref, tg_ref, tv_ref, hn_ref, cg_ref, cv_ref, *, tm, seq_rows, tail):
    i = pl.program_id(1)
    j = pl.program_id(2)
    carried = tm <= seq_rows

    @pl.when(j == 0)
    def _():
        h = h_ref[0]
        ms = jnp.mean(h * h, axis=-1, keepdims=True)
        hn_ref[...] = (h * lax.rsqrt(ms + NORM_EPS) * g_ref[...]).astype(bf16)
        y_ref[0] = h

    hn = hn_ref[...]
    rowi = lax.broadcasted_iota(jnp.int32, (tm, 1), 0)
    t_in = rowi % seq_rows

    def conv(u, cw_ref, cb_ref, prev_ref, carry_ref):
        if carried:
            @pl.when(i == 0)
            def _():
                carry_ref[j] = prev_ref[0]
            p2 = carry_ref[j, 0:1, :]
            p1 = carry_ref[j, 1:2, :]
            u1 = jnp.where(rowi == 0, p1, pltpu.roll(u, 1, axis=0))
            u2 = jnp.where(rowi == 0, p2, jnp.where(rowi == 1, p1, pltpu.roll(u, 2, axis=0)))
            carry_ref[j] = u[tm - 2:tm, :]
        else:
            pr = prev_ref[0]
            u1 = jnp.where(t_in == 0, pltpu.roll(pr, tm - 1, axis=0), pltpu.roll(u, 1, axis=0))
            u2 = jnp.where(t_in == 0, pr, jnp.where(t_in == 1, pr, pltpu.roll(u, 2, axis=0)))
        return cb_ref[...] + cw_ref[0:1, :] * u2 + cw_ref[1:2, :] * u1 + cw_ref[2:3, :] * u

    ug = jnp.dot(hn, wug_ref[...], preferred_element_type=f32)
    uv = jnp.dot(hn, wuv_ref[...], preferred_element_type=f32)
    tg_ref[0, 0] = ug[tm - tail:tm, :]
    tv_ref[0, 0] = uv[tm - tail:tm, :]
    gate = conv(ug, cwg_ref, cbg_ref, pg_ref, cg_ref)
    val = conv(uv, cwv_ref, cbv_ref, pv_ref, cv_ref)
    act = (gate * jax.nn.sigmoid(gate) * val).astype(bf16)
    y_ref[0] += jnp.dot(act, wd_ref[...], preferred_element_type=f32)


def _conv_ffn(h, conv_prev, norm_g, w_up, conv_w, conv_b, w_down, *, fold):
    B, T, D = h.shape
    dff = w_down.shape[0]
    tf = _pick_tile(dff, 512, LANE)
    nf = dff // tf
    if not fold:
        nb_, tm = B, _pick_tile(T, 512, 8)
        tail = 8
        hh = h
        prev = conv_prev
        prev_spec_g = pl.BlockSpec((1, CONV_W - 1, tf), lambda b, i, j: (b, 0, j))
        prev_spec_v = pl.BlockSpec((1, CONV_W - 1, tf), lambda b, i, j: (b, 0, nf + j))
    else:
        nb_, tm = 1, B * T
        tail = tm
        hh = h.reshape(1, B * T, D)
        assert T >= CONV_W - 1
        prev = jnp.concatenate([conv_prev, jnp.zeros((B, T - (CONV_W - 1), 2 * dff), f32)],
                               axis=1).reshape(1, B * T, 2 * dff)
        prev_spec_g = pl.BlockSpec((1, tm, tf), lambda b, i, j: (0, 0, j))
        prev_spec_v = pl.BlockSpec((1, tm, tf), lambda b, i, j: (0, 0, nf + j))
    nt = hh.shape[1] // tm
    body = functools.partial(_conv_ffn_body, tm=tm, seq_rows=T, tail=tail)
    cw = conv_w
    cb = conv_b.reshape(1, 2 * dff)
    tail_spec = pl.BlockSpec((1, 1, tail, tf), lambda b, i, j: (b, i, 0, j))
    tail_shape = jax.ShapeDtypeStruct((nb_, nt, tail, dff), f32)
    y, ug, uv = pl.pallas_call(
        body,
        grid=(nb_, nt, nf),
        in_specs=[pl.BlockSpec((1, tm, D), lambda b, i, j: (b, i, 0)),
                  pl.BlockSpec((1, D), lambda b, i, j: (0, 0)),
                  pl.BlockSpec((D, tf), lambda b, i, j: (0, j)),
                  pl.BlockSpec((D, tf), lambda b, i, j: (0, nf + j)),
                  pl.BlockSpec((CONV_W, tf), lambda b, i, j: (0, j)),
                  pl.BlockSpec((CONV_W, tf), lambda b, i, j: (0, nf + j)),
                  pl.BlockSpec((1, tf), lambda b, i, j: (0, j)),
                  pl.BlockSpec((1, tf), lambda b, i, j: (0, nf + j)),
                  pl.BlockSpec((tf, D), lambda b, i, j: (j, 0)),
                  prev_spec_g, prev_spec_v],
        out_specs=[pl.BlockSpec((1, tm, D), lambda b, i, j: (b, i, 0)), tail_spec, tail_spec],
        out_shape=[jax.ShapeDtypeStruct(hh.shape, f32), tail_shape, tail_shape],
        scratch_shapes=[pltpu.VMEM((tm, D), bf16),
                        pltpu.VMEM((nf, CONV_W - 1, tf), f32),
                        pltpu.VMEM((nf, CONV_W - 1, tf), f32)],
        compiler_params=_cparams(("parallel", "arbitrary", "arbitrary")),
        name="conv_ffn",
    )(hh, norm_g.reshape(1, D), w_up, w_up, cw, cw, cb, cb, w_down, prev, prev)
    return y, ug[:, -1], uv[:, -1]


def _split_w_in(w_in):
    o = 0
    parts = []
    for n in (ATT_DIM, KV_COLS, KV_COLS, KV_COLS, 3 * N_HEADS, SHIFT_COLS, 2 * w_in.shape[0]):
        parts.append(w_in[:, o:o + n])
        o += n
    wq, wc, ws, ww, wg, wsh, wm = parts
    wg = jnp.pad(wg, ((0, 0), (0, LANE - 3 * N_HEADS)))
    w_att = jnp.concatenate([wq, wc, ws, ww, wg], axis=1).astype(bf16)
    return w_att, wsh.astype(bf16), wm.astype(bf16)


def _head_major(x, n):
    B, T, _ = x.shape
    return x.reshape(B, T, n, HEAD_DIM).transpose(0, 2, 1, 3)


def _mixer_inputs(x2d, pos_tab, norm_g, w_parts, consts, wc, with_summ):
    w_att, w_sh, w_mg = w_parts
    p_att = _norm_matmul(x2d, norm_g, w_att)
    p_shift = _norm_matmul(x2d, norm_g, w_sh)
    p_merge = _norm_matmul(x2d, norm_g, w_mg)
    post = _qk_post(p_att, pos_tab, consts, wc, with_summ)
    return post, p_shift, p_merge


def kernel(x_prompt, x_sample, cache_kv_cmp, cache_kv_sel, page_table, cache_kv_win, state_wkv, state_shift, state_conv, norm1_g, w_in, q_gain, k_gains, w_cmp, mu_shift, w0, w_lora_w, a0, w_lora_a, k_k, k_a, r_k, ln_x_w, ln_x_b, w_branch_a, w_branch_b, w_out, norm2_g, w_up, conv_w, conv_b, w_down):
    B, T, D = x_prompt.shape
    DB, TS, _ = x_sample.shape
    depth = w_in.shape[0]
    assert depth == 1, "single-layer trunk"
    l = 0
    page = cache_kv_cmp.shape[2]
    n_pages = page_table.shape[1]
    past_len = n_pages * page
    assert past_len % BLOCK == 0 and TS <= BLOCK and page % BLOCK == 0 and T % BLOCK == 0
    dff = w_down.shape[1]

    w_parts = _split_w_in(w_in[l])
    consts = _qk_consts(q_gain[l], k_gains[l])
    wc = _compress_weights(w_cmp[l])
    rw = (mu_shift[l], w0[l], w_lora_w[l], a0[l], w_lora_a[l], k_k[l], k_a[l], r_k[l], ln_x_w[l], ln_x_b[l])
    w_a = w_branch_a[l].astype(bf16)
    w_b = w_branch_b[l].astype(bf16)
    w_o = w_out[l].astype(bf16)
    w_u = w_up[l].astype(bf16)
    w_d = w_down[l].astype(bf16)

    xp = x_prompt.reshape(B * T, D)
    tabs_p = _rope_tables(jnp.arange(T, dtype=jnp.int32))
    ((q_pad, kvc, kvs, kvw, gates, summ, ks_aug, vs_pad, kw_pad, vw_pad),
     p_shift, p_merge) = _mixer_inputs(xp, tabs_p, norm1_g[l], w_parts, consts, wc, True)
    nb = T // BLOCK
    summ = summ.reshape(B, nb, KV_COLS)
    lane_pad = lambda x: jnp.pad(x, ((0, 0), (0, 0), (0, 0), (0, LANE - HEAD_DIM))).astype(bf16)
    kvw3 = kvw.reshape(B, T, KV_COLS)
    o_att = _nsa_prompt(q_pad,
                        lane_pad(_head_major(summ[:, :, :K_COLS], N_KV)),
                        lane_pad(_head_major(summ[:, :, K_COLS:], N_KV)),
                        ks_aug, vs_pad, kw_pad, vw_pad, gates.reshape(B, T, LANE))
    p_shift3 = p_shift.reshape(B, T, SHIFT_COLS)
    y_rw, wkv_p = _rwkv(p_shift3, jnp.zeros((B, SHIFT_COLS), f32),
                        jnp.zeros((B, RW_HEADS, RW_HEAD, RW_HEAD), f32), rw)
    h_p = _merge(xp, o_att.reshape(B * T, ATT_DIM), y_rw.reshape(B * T, RW_DIM), p_merge, w_a, w_b, w_o)
    y_p, ug, uv = _conv_ffn(h_p.reshape(B, T, D), jnp.zeros((B, CONV_W - 1, 2 * dff), f32), norm2_g[l],
                            w_u, conv_w[l], conv_b[l], w_d, fold=False)
    assert T >= CONV_W - 1
    conv_p = jnp.concatenate([ug[:, -(CONV_W - 1):], uv[:, -(CONV_W - 1):]], axis=-1)
    kv_shape_p = (1, B, T, 2, N_KV, HEAD_DIM)
    keep_p = min(WINDOW, T)
    outs_p = (y_p,
              kvc.reshape(kv_shape_p), kvs.reshape(kv_shape_p),
              kvw3[:, T - keep_p:].reshape(1, B, keep_p, 2, N_KV, HEAD_DIM),
              wkv_p[None], p_shift3[:, -1][None], conv_p[None])

    xs = x_sample.reshape(DB * TS, D)
    pos_s = past_len + jnp.arange(TS, dtype=jnp.int32)
    tabs_s = tuple(jnp.tile(t, (DB, 1)) for t in _rope_tables(pos_s))
    (q, kvc_s, kvs_s, kvw_s, gates), p_shift, p_merge = _mixer_inputs(xs, tabs_s, norm1_g[l], w_parts, consts, wc, False)
    pps = _pick_tile(n_pages, 8, 1)
    pool_c = cache_kv_cmp[l].reshape(-1, page, KV_COLS)
    pool_s = cache_kv_sel[l].reshape(-1, page, KV_COLS)
    summ_s = _compress_pool(pool_c, page_table, wc, pps).reshape(DB, past_len // BLOCK, KV_COLS)
    R = HPG * TS
    q_g = _head_major(q.reshape(DB, TS, ATT_DIM), N_HEADS).reshape(DB, N_KV, R, HEAD_DIM)
    gates_g = (gates[:, :3 * N_HEADS].reshape(DB, TS, N_HEADS, 3).transpose(0, 2, 1, 3)
               .reshape(DB, N_KV, R, 3))
    win_buf = cache_kv_win[l].reshape(DB, -1, KV_COLS)
    keep = win_buf.shape[1]
    kvw_s3 = kvw_s.reshape(DB, TS, KV_COLS)
    o_g = _nsa_sample(q_g, gates_g, summ_s, kvs_s.reshape(DB, TS, KV_COLS), win_buf, kvw_s3,
                      pool_s, page_table, pps, past_len)
    o_att_s = (o_g.reshape(DB, N_HEADS, TS, HEAD_DIM).transpose(0, 2, 1, 3)
               .reshape(DB * TS, ATT_DIM).astype(bf16))
    p_shift3s = p_shift.reshape(DB, TS, SHIFT_COLS)
    y_rw_s, wkv_s = _rwkv(p_shift3s, state_shift[l], state_wkv[l], rw)
    h_s = _merge(xs, o_att_s, y_rw_s.reshape(DB * TS, RW_DIM), p_merge, w_a, w_b, w_o)
    y_s, ug, uv = _conv_ffn(h_s.reshape(DB, TS, D), state_conv[l], norm2_g[l],
                            w_u, conv_w[l], conv_b[l], w_d, fold=True)
    up_s = jnp.concatenate([state_conv[l],
                            jnp.concatenate([ug.reshape(DB, TS, dff), uv.reshape(DB, TS, dff)], axis=-1)], axis=1)
    conv_s = up_s[:, TS:]
    win_s = jnp.concatenate([win_buf, kvw_s3], axis=1)[:, TS:]
    kv_shape_s = (1, DB, TS, 2, N_KV, HEAD_DIM)

    return (outs_p[0], y_s.reshape(DB, TS, D),
            outs_p[1], kvc_s.reshape(kv_shape_s),
            outs_p[2], kvs_s.reshape(kv_shape_s),
            outs_p[3], win_s.reshape(1, DB, keep, 2, N_KV, HEAD_DIM),
            outs_p[4], wkv_s[None],
            outs_p[5], p_shift3s[:, -1][None],
            outs_p[6], conv_s[None])
```

```python
import functools

import numpy as np
import jax
import jax.numpy as jnp
from jax import lax
from jax.experimental import pallas as pl
from jax.experimental.pallas import tpu as pltpu

f32 = jnp.float32
bf16 = jnp.bfloat16

N_HEADS = 16
N_KV = 4
HPG = N_HEADS // N_KV
HEAD_DIM = 64
ROPE_DIM = HEAD_DIM // 4
ROPE_THETA = 500000.0
BLOCK = 64
N_SEL = 16
WINDOW = 512
RW_HEADS = 16
RW_HEAD = 64
RW_DIM = RW_HEADS * RW_HEAD
LORA_W = 64
LORA_A = 64
LN_X_EPS = 64e-5
CONV_W = 3
NORM_EPS = 1e-6
ATT_DIM = N_HEADS * HEAD_DIM
KV_COLS = 2 * N_KV * HEAD_DIM
K_COLS = N_KV * HEAD_DIM
SHIFT_COLS = 3 * RW_DIM + LORA_W + LORA_A
NEG = -1e30
LOG2E = 1.4426950408889634

LANE = 128
VMEM_LIMIT = 56 * 1024 * 1024
QKV_COLS = ATT_DIM + 3 * KV_COLS
ATT_PROJ_COLS = QKV_COLS + LANE
N_NORM_HEADS = QKV_COLS // HEAD_DIM
HIGHEST = lax.Precision.HIGHEST
RWKV_BATCH_PER_STEP = 2
FFN_ROW_SUBBLOCKS = 2


def _cparams(sem):
    return pltpu.CompilerParams(dimension_semantics=sem, vmem_limit_bytes=VMEM_LIMIT)


def _pick_tile(n, cap, mult):
    best = None
    for t in range(mult, min(n, cap) + 1, mult):
        if n % t == 0:
            best = t
    assert best is not None, (n, cap, mult)
    return best


def _norm_matmul_body(x_ref, g_ref, w_ref, o_ref, xn_ref):
    @pl.when(pl.program_id(1) == 0)
    def _():
        x = x_ref[...]
        ms = jnp.mean(x * x, axis=-1, keepdims=True)
        xn_ref[...] = (x * lax.rsqrt(ms + NORM_EPS) * g_ref[...]).astype(bf16)

    o_ref[...] = jnp.dot(xn_ref[...], w_ref[...], preferred_element_type=f32)


def _norm_matmul(x, gain, w):
    M, D = x.shape
    N = w.shape[1]
    tm = _pick_tile(M, 1024, 8)
    tn = _pick_tile(N, 1024, LANE)
    return pl.pallas_call(
        _norm_matmul_body,
        grid=(M // tm, N // tn),
        in_specs=[pl.BlockSpec((tm, D), lambda i, j: (i, 0)),
                  pl.BlockSpec((1, D), lambda i, j: (0, 0)),
                  pl.BlockSpec((D, tn), lambda i, j: (0, j))],
        out_specs=pl.BlockSpec((tm, tn), lambda i, j: (i, j)),
        out_shape=jax.ShapeDtypeStruct((M, N), f32),
        scratch_shapes=[pltpu.VMEM((tm, D), bf16)],
        compiler_params=_cparams(("parallel", "arbitrary")),
        name="norm_matmul",
    )(x, gain.reshape(1, D), w)


def _qk_post_body(p_ref, cos_ref, sa_ref, sb_ref, gvec_ref, isk_ref, seg_ref, segt_ref, wc_ref,
                  q_ref, kvc_ref, kvs_ref, kvw_ref, gate_ref, *prompt_refs, nt):
    y = p_ref[:, :QKV_COLS]
    isk = isk_ref[...] > 0.5
    sq_hi, sq_lo = _split_bf16(y * y)
    ss = (jnp.dot(sq_hi, seg_ref[...], preferred_element_type=f32)
          + jnp.dot(sq_lo, seg_ref[...], preferred_element_type=f32)) * (1.0 / HEAD_DIM)
    rs_hi, rs_lo = _split_bf16(lax.rsqrt(ss + NORM_EPS))
    rb = (jnp.dot(rs_hi, segt_ref[...], preferred_element_type=f32)
          + jnp.dot(rs_lo, segt_ref[...], preferred_element_type=f32))
    yn = jnp.where(isk, y * rb * gvec_ref[...], y)
    reps = QKV_COLS // LANE
    cos = jnp.where(isk, jnp.concatenate([cos_ref[...]] * reps, axis=1), 1.0)
    sa = jnp.where(isk, jnp.concatenate([sa_ref[...]] * reps, axis=1), 0.0)
    sb = jnp.where(isk, jnp.concatenate([sb_ref[...]] * reps, axis=1), 0.0)
    half = ROPE_DIM // 2
    out = (yn * cos + pltpu.roll(yn, QKV_COLS - half, axis=1) * sa + pltpu.roll(yn, half, axis=1) * sb)
    qs = out[:, :ATT_DIM] * (HEAD_DIM ** -0.5 * (LOG2E if prompt_refs else 1.0))
    kvc = out[:, ATT_DIM:ATT_DIM + KV_COLS]
    kvs = out[:, ATT_DIM + KV_COLS:ATT_DIM + 2 * KV_COLS]
    kvw = out[:, ATT_DIM + 2 * KV_COLS:]
    kvc_ref[...] = kvc
    kvs_ref[...] = kvs
    kvw_ref[...] = kvw
    gate_ref[...] = jax.nn.sigmoid(p_ref[:, QKV_COLS:])
    if not prompt_refs:
        q_ref[...] = qs.astype(bf16)
        return
    summ_ref, ksa_ref, vsp_ref, kwp_ref, vwp_ref = prompt_refs
    tm = kvc.shape[0]
    blk = kvc.reshape(tm // BLOCK, BLOCK, KV_COLS) * wc_ref[...][None]
    summ_ref[0] = jnp.sum(blk, axis=1)
    zeros = jnp.zeros((tm, LANE - HEAD_DIM), f32)
    t0 = (pl.program_id(0) % nt) * tm
    blk_of_row = (t0 + lax.broadcasted_iota(jnp.int32, zeros.shape, 0)) // BLOCK
    onehot = jnp.where(lax.broadcasted_iota(jnp.int32, zeros.shape, 1) == blk_of_row, 1.0, 0.0)
    hd = lambda x, h: x[:, h * HEAD_DIM:(h + 1) * HEAD_DIM]
    pad = lambda x, tail: jnp.concatenate([x, tail], axis=1).astype(bf16)
    for h in range(N_HEADS):
        q_ref[0, h] = pad(hd(qs, h), zeros)
    for g in range(N_KV):
        ksa_ref[0, g] = pad(hd(kvs, g), onehot)
        vsp_ref[0, g] = pad(hd(kvs, N_KV + g), zeros)
        kwp_ref[0, g] = pad(hd(kvw, g), zeros)
        vwp_ref[0, g] = pad(hd(kvw, N_KV + g), zeros)


def _qk_post(p_att, tabs, consts, wc, prompt):
    M = p_att.shape[0]
    cos_t, sa_t, sb_t = tabs
    Tt = cos_t.shape[0]
    tm = _pick_tile(Tt, 256, BLOCK if prompt else 8)
    nt = Tt // tm
    gvec, isk, seg, segt = consts
    row = lambda i: (i, 0)
    tab = lambda i: (i % nt, 0)
    const = lambda i: (0, 0)
    kv_shape = jax.ShapeDtypeStruct((M, KV_COLS), f32)
    kv_spec = pl.BlockSpec((tm, KV_COLS), row)
    out_shape = [jax.ShapeDtypeStruct((M, ATT_DIM), bf16), kv_shape, kv_shape, kv_shape,
                 jax.ShapeDtypeStruct((M, LANE), f32)]
    out_specs = [pl.BlockSpec((tm, ATT_DIM), row), kv_spec, kv_spec, kv_spec, pl.BlockSpec((tm, LANE), row)]
    if prompt:
        assert Tt // BLOCK <= LANE - HEAD_DIM, "one-hot block lanes"
        B = M // Tt
        hm = lambda n: jax.ShapeDtypeStruct((B, n, Tt, LANE), bf16)
        hm_spec = lambda n: pl.BlockSpec((1, n, tm, LANE), lambda i: (i // nt, 0, i % nt, 0))
        out_shape[0], out_specs[0] = hm(N_HEADS), hm_spec(N_HEADS)
        out_shape += [jax.ShapeDtypeStruct((M // tm, tm // BLOCK, KV_COLS), f32)] + [hm(N_KV)] * 4
        out_specs += [pl.BlockSpec((1, tm // BLOCK, KV_COLS), lambda i: (i, 0, 0))] + [hm_spec(N_KV)] * 4
    return pl.pallas_call(
        functools.partial(_qk_post_body, nt=nt),
        grid=(M // tm,),
        in_specs=[pl.BlockSpec((tm, ATT_PROJ_COLS), row),
                  pl.BlockSpec((tm, LANE), tab), pl.BlockSpec((tm, LANE), tab), pl.BlockSpec((tm, LANE), tab),
                  pl.BlockSpec((1, QKV_COLS), const), pl.BlockSpec((1, QKV_COLS), const),
                  pl.BlockSpec((QKV_COLS, N_NORM_HEADS), const), pl.BlockSpec((N_NORM_HEADS, QKV_COLS), const),
                  pl.BlockSpec((BLOCK, KV_COLS), const)],
        out_specs=out_specs,
        out_shape=out_shape,
        compiler_params=_cparams(("parallel",)),
        name="qk_post",
    )(p_att, cos_t, sa_t, sb_t, gvec, isk, seg, segt, wc)


def _rope_tables(pos):
    half = ROPE_DIM // 2
    inv = ROPE_THETA ** (-jnp.arange(half, dtype=f32) * 2.0 / ROPE_DIM)
    ang = pos.astype(f32)[:, None] * inv[None, :]
    cos, sin = jnp.cos(ang), jnp.sin(ang)
    n = pos.shape[0]
    ones = jnp.ones((n, HEAD_DIM - ROPE_DIM), f32)
    zeros = jnp.zeros((n, HEAD_DIM - half), f32)
    cos_h = jnp.concatenate([cos, cos, ones], axis=1)
    sa_h = jnp.concatenate([-sin, zeros], axis=1)
    sb_h = jnp.concatenate([jnp.zeros((n, half), f32), sin, jnp.zeros((n, HEAD_DIM - ROPE_DIM), f32)], axis=1)
    rep = LANE // HEAD_DIM
    return tuple(jnp.concatenate([t] * rep, axis=1) for t in (cos_h, sa_h, sb_h))


def _qk_consts(q_gain, k_gains):
    ones_v = jnp.ones((K_COLS,), f32)
    gvec = jnp.concatenate([jnp.tile(q_gain, N_HEADS)]
                           + [t for s in range(3) for t in (jnp.tile(k_gains[s], N_KV), ones_v)])
    isk_np = np.concatenate([np.ones(ATT_DIM)] + [np.ones(K_COLS), np.zeros(K_COLS)] * 3).astype(np.float32)
    seg_np = (np.arange(QKV_COLS)[:, None] // HEAD_DIM == np.arange(N_NORM_HEADS)[None, :]).astype(np.float32)
    seg_np = seg_np * isk_np[:, None]
    return (gvec.reshape(1, QKV_COLS), jnp.asarray(isk_np).reshape(1, QKV_COLS),
            jnp.asarray(seg_np, bf16), jnp.asarray(seg_np.T, bf16))


def _compress_weights(w_cmp):
    return jnp.concatenate([jnp.tile(w_cmp[c], (1, N_KV)) for c in range(2)], axis=1)


def _select_blocks(imp, n_pick):
    nb = imp.shape[-1]
    lane = lax.broadcasted_iota(jnp.int32, imp.shape, imp.ndim - 1).astype(f32)
    sel = jnp.zeros(imp.shape, f32)
    for _ in range(min(n_pick, nb)):
        mx = jnp.max(imp, axis=-1, keepdims=True)
        idx = jnp.min(jnp.where(imp == mx, lane, float(nb)), axis=-1, keepdims=True)
        hit = (lane == idx) & (mx >= 0.0)
        sel = jnp.where(hit, 1.0, sel)
        imp = jnp.where(lane == idx, -2.0, imp)
    return sel


def _softmax_parts(s, mask):
    s = jnp.where(mask, s, NEG)
    m = jnp.max(s, axis=-1, keepdims=True)
    e = jnp.where(mask, jnp.exp(s - m), 0.0)
    return m, e


def _nt_dot(a, b):
    return lax.dot_general(a, b, (((1,), (1,)), ((), ())), preferred_element_type=f32)


def _nsa_prompt_body(q_ref, kc_ref, vc_ref, ks_ref, vs_ref, kw_ref, vw_ref, g_ref, gx_ref, place_ref,
                     sb_ref, wb_ref, o_ref, *, tq, T, kc_tile, slab):
    i = pl.program_id(2)
    q0 = i * tq
    R = HPG * tq
    nb = T // BLOCK
    tn_dims = (((0,), (0,)), ((), ()))
    q = q_ref[0].reshape(R, LANE)

    tpos = q0 + lax.broadcasted_iota(jnp.int32, (1, R), 1) % tq
    blk = lax.broadcasted_iota(jnp.int32, (nb, R), 0)
    vis = (blk + 1) * BLOCK - 1 <= tpos
    s_c = jnp.where(vis, _nt_dot(kc_ref[0, 0], q), NEG)
    e_c = jnp.where(vis, jnp.exp2(s_c - jnp.max(s_c, axis=0, keepdims=True)), 0.0)
    p_c = e_c / jnp.maximum(jnp.sum(e_c, axis=0, keepdims=True), 1e-30)
    o_c = lax.dot_general(p_c.astype(bf16), vc_ref[0, 0], tn_dims, preferred_element_type=f32)

    imp = p_c[:, 0:tq]
    for hh in range(1, HPG):
        imp = imp + p_c[:, hh * tq:(hh + 1) * tq]
    blk_q = lax.broadcasted_iota(jnp.int32, (nb, tq), 0)
    cur_q = (q0 + lax.broadcasted_iota(jnp.int32, (1, tq), 1)) // BLOCK
    cand = blk_q < cur_q
    imp = jnp.where(cand, imp, -1.0)
    ahead = jnp.zeros((nb, tq), f32)
    for m in range(nb):
        row_m = imp[m:m + 1, :]
        tie = jnp.where(blk_q > m, 1.0, 0.0)
        ahead = ahead + jnp.where(row_m > imp, 1.0, jnp.where(row_m == imp, tie, 0.0))
    sel = jnp.where(cand, jnp.where(ahead < N_SEL - 1, 1.0, 0.0), jnp.where(blk_q == cur_q, 1.0, 0.0))
    sel_l = lax.dot_general(sel.astype(bf16), place_ref[...], tn_dims, preferred_element_type=f32)
    lane = lax.broadcasted_iota(jnp.int32, (1, LANE), 1)
    in_blk_lanes = jnp.where((lane >= HEAD_DIM) & (lane < HEAD_DIM + nb), 1.0, 0.0)
    q_off = ((in_blk_lanes - sel_l) * NEG).astype(bf16)
    q_sel = q + jnp.concatenate([q_off] * HPG, axis=0)

    def attend(carry, qq, k, v, bias):
        m, l, acc = carry
        s = _nt_dot(qq, k)
        if bias is not None:
            s = (s.reshape(HPG, tq, s.shape[-1]) + bias[None]).reshape(s.shape)
        m_new = jnp.maximum(m, jnp.max(s, axis=-1, keepdims=True))
        alpha = jnp.exp2(m - m_new)
        e = jnp.exp2(s - m_new)
        l = alpha * l + jnp.sum(e, axis=-1, keepdims=True)
        acc = alpha * acc + jnp.dot(e.astype(bf16), v, preferred_element_type=f32)
        return m_new, l, acc

    init = (jnp.full((R, 1), NEG, f32), jnp.zeros((R, 1), f32), jnp.zeros((R, LANE), f32))

    def sel_step(c, carry):
        k0 = pl.multiple_of(c * kc_tile, kc_tile)
        return attend(carry, q_sel, ks_ref[0, 0, pl.ds(k0, kc_tile), :], vs_ref[0, 0, pl.ds(k0, kc_tile), :],
                      None)

    n_full = q0 // kc_tile
    carry = lax.fori_loop(0, n_full, sel_step, init)
    kd = pl.multiple_of(n_full * kc_tile, kc_tile)
    _, l_s, acc_s = attend(carry, q_sel, ks_ref[0, 0, pl.ds(kd, kc_tile), :],
                           vs_ref[0, 0, pl.ds(kd, kc_tile), :], sb_ref[0])
    o_s = acc_s / jnp.maximum(l_s, 1e-30)

    w0 = pl.multiple_of(jnp.clip(q0 + tq - slab, 0, T - slab), tq)
    _, l_w, acc_w = attend(init, q, kw_ref[0, 0, pl.ds(w0, slab), :], vw_ref[0, 0, pl.ds(w0, slab), :],
                           wb_ref[0])
    o_w = acc_w / jnp.maximum(l_w, 1e-30)

    gt = g_ref[0]
    g_hi = gt.astype(bf16)
    g_lo = (gt - g_hi.astype(f32)).astype(bf16)
    G = (jnp.dot(g_hi, gx_ref[0], preferred_element_type=f32)
         + jnp.dot(g_lo, gx_ref[0], preferred_element_type=f32))
    outs = []
    for hh in range(HPG):
        rs = slice(hh * tq, (hh + 1) * tq)
        gcol = lambda j: G[:, (hh * 3 + j) * LANE:(hh * 3 + j + 1) * LANE]
        o_h = gcol(0) * o_c[rs] + gcol(1) * o_s[rs] + gcol(2) * o_w[rs]
        outs.append(o_h[:, :HEAD_DIM])
    o_ref[0] = jnp.concatenate(outs, axis=1).astype(o_ref.dtype)


def _nsa_prompt(q_pad, kc_pad, vc_pad, ks_aug, vs_pad, kw_pad, vw_pad, gates):
    B, _, T, _ = q_pad.shape
    tq = _pick_tile(T, 128, 16)
    kc_tile = _pick_tile(T, 512, tq)
    slab = min(T, WINDOW + tq)
    nb = T // BLOCK
    nq = T // tq
    place = np.zeros((nb, LANE), np.float32)
    place[np.arange(nb), HEAD_DIM + np.arange(nb)] = 1.0
    gx = np.zeros((N_KV, LANE, 3 * HPG, LANE), np.float32)
    for g in range(N_KV):
        for c in range(3 * HPG):
            gx[g, g * 3 * HPG + c, c, :] = 1.0
    gx = gx.reshape(N_KV, LANE, 3 * HPG * LANE)
    r = np.arange(tq)[:, None]
    nrel = kc_tile // tq
    sel_bias = np.stack([np.where(np.arange(kc_tile)[None, :] <= rel * tq + r, 0.0, NEG) for rel in range(nrel)])
    n_wb = min(nq, WINDOW // tq + 1) if slab == WINDOW + tq else nq
    win_bias = []
    for i in range(n_wb):
        w0 = min(max(i * tq + tq - slab, 0), T - slab)
        dist = (i * tq + r) - (w0 + np.arange(slab)[None, :])
        win_bias.append(np.where((dist >= 0) & (dist <= WINDOW), 0.0, NEG))
    win_bias = np.stack(win_bias)
    kv_spec = pl.BlockSpec((1, 1, T, LANE), lambda b, g, i: (b, g, 0, 0))
    c_spec = pl.BlockSpec((1, 1, nb, LANE), lambda b, g, i: (b, g, 0, 0))
    return pl.pallas_call(
        functools.partial(_nsa_prompt_body, tq=tq, T=T, kc_tile=kc_tile, slab=slab),
        grid=(B, N_KV, nq),
        in_specs=[pl.BlockSpec((1, HPG, tq, LANE), lambda b, g, i: (b, g, i, 0)),
                  c_spec, c_spec, kv_spec, kv_spec, kv_spec, kv_spec,
                  pl.BlockSpec((1, tq, LANE), lambda b, g, i: (b, i, 0)),
                  pl.BlockSpec((1, LANE, 3 * HPG * LANE), lambda b, g, i: (g, 0, 0)),
                  pl.BlockSpec((nb, LANE), lambda b, g, i: (0, 0)),
                  pl.BlockSpec((1, tq, kc_tile), lambda b, g, i: (i % nrel, 0, 0)),
                  pl.BlockSpec((1, tq, slab), lambda b, g, i: (jnp.minimum(i, n_wb - 1), 0, 0))],
        out_specs=pl.BlockSpec((1, tq, HPG * HEAD_DIM), lambda b, g, i: (b, i, g)),
        out_shape=jax.ShapeDtypeStruct((B, T, ATT_DIM), bf16),
        compiler_params=_cparams(("parallel", "parallel", "arbitrary")),
        name="nsa_prompt",
    )(q_pad, kc_pad, vc_pad, ks_aug, vs_pad, kw_pad, vw_pad, gates,
      jnp.asarray(gx, bf16), jnp.asarray(place, bf16), jnp.asarray(sel_bias, f32), jnp.asarray(win_bias, f32))


def _pages_row_minor(pool):
    return jnp.transpose(pool, (0, 2, 3, 4, 1))


def _split_bf16(x):
    hi = x.astype(bf16)
    return hi, (x - hi.astype(f32)).astype(bf16)


def _compress_pool_body(pt_ref, *refs, pps):
    wt = refs[pps][...]
    seg = refs[pps + 1][...]
    out_ref = refs[pps + 2]
    for p in range(pps):
        page = refs[p][0]
        x = (page * wt).reshape(KV_COLS, page.shape[-1])
        hi, lo = _split_bf16(x)
        out_ref[0, p] = _nt_dot(seg, hi) + _nt_dot(seg, lo)


def _compress_pool(pool_t, page_table, w_cmp, pps):
    DB, NP = page_table.shape
    page = pool_t.shape[-1]
    bpp = page // BLOCK
    wt = jnp.tile(jnp.transpose(w_cmp, (0, 2, 1)), (1, 1, bpp))[:, None]
    seg = jnp.asarray(np.arange(page)[None, :] // BLOCK == np.arange(bpp)[:, None], bf16)

    def page_spec(p):
        return pl.BlockSpec((1, 2, N_KV, HEAD_DIM, page), lambda b, s, pt: (pt[b, s * pps + p], 0, 0, 0, 0))

    return pl.pallas_call(
        functools.partial(_compress_pool_body, pps=pps),
        grid_spec=pltpu.PrefetchScalarGridSpec(
            num_scalar_prefetch=1,
            grid=(DB, NP // pps),
            in_specs=[page_spec(p) for p in range(pps)]
            + [pl.BlockSpec((2, 1, HEAD_DIM, page), lambda b, s, pt: (0, 0, 0, 0)),
               pl.BlockSpec((bpp, page), lambda b, s, pt: (0, 0))],
            out_specs=pl.BlockSpec((1, pps, bpp, KV_COLS), lambda b, s, pt: (b, s, 0, 0))),
        out_shape=jax.ShapeDtypeStruct((DB, NP, bpp, KV_COLS), f32),
        compiler_params=_cparams(("parallel", "arbitrary")),
        name="compress_pool",
    )(page_table, *([pool_t] * pps), wt, seg)


def _nsa_sample_body(pt_ref, *refs, pps, past_len, ts):
    pages = refs[:pps]
    (q_ref, g_ref, summ_ref, ns_ref, wb_ref, nw_ref, ex_ref, o_ref,
     selq_ref, m_ref, l_ref, acc_ref, oc_ref) = refs[pps:]
    s_id = pl.program_id(1)
    n_steps = pl.num_programs(1)
    R = HPG * ts
    nbp = summ_ref.shape[1]
    page = pages[0].shape[-1]
    kt = pps * page
    nbs = kt // BLOCK
    row = lax.broadcasted_iota(jnp.int32, (R, 1), 0)
    tpos = past_len + row % ts

    @pl.when(s_id == 0)
    def _():
        blk = lax.broadcasted_iota(jnp.int32, (R, nbp), 1)
        vis = (blk + 1) * BLOCK - 1 <= tpos
        imps = []
        for g in range(N_KV):
            q = q_ref[0, g]
            kc = summ_ref[0, :, g * HEAD_DIM:(g + 1) * HEAD_DIM].astype(bf16)
            vc = summ_ref[0, :, K_COLS + g * HEAD_DIM:K_COLS + (g + 1) * HEAD_DIM].astype(bf16)
            _, e_c = _softmax_parts(_nt_dot(q, kc), vis)
            p_c = e_c / jnp.maximum(jnp.sum(e_c, axis=-1, keepdims=True), 1e-30)
            oc_ref[g] = jnp.dot(p_c.astype(bf16), vc, preferred_element_type=f32)
            imp = p_c[0:ts]
            for hh in range(1, HPG):
                imp = imp + p_c[hh * ts:(hh + 1) * ts]
            imps.append(imp)
        blk_q = lax.broadcasted_iota(jnp.int32, (N_KV * ts, nbp), 1)
        cur_q = (past_len + lax.broadcasted_iota(jnp.int32, (N_KV * ts, 1), 0) % ts) // BLOCK
        imp = jnp.where(blk_q < cur_q, jnp.concatenate(imps, axis=0), -1.0)
        off = ((1.0 - _select_blocks(imp, N_SEL - 1)) * NEG).astype(bf16)
        for g in range(N_KV):
            off_g = jnp.concatenate([off[g * ts:(g + 1) * ts]] * HPG, axis=0)
            for s in range(nbp // nbs):
                selq_ref[s, g] = off_g[:, s * nbs:(s + 1) * nbs]
        m_ref[...] = jnp.full(m_ref.shape, NEG, f32)
        l_ref[...] = jnp.zeros(l_ref.shape, f32)
        acc_ref[...] = jnp.zeros(acc_ref.shape, f32)

    def online_update(g, s, vt, v_rows):
        m_old = m_ref[g]
        m_new = jnp.maximum(m_old, jnp.max(s, axis=-1, keepdims=True))
        alpha = jnp.exp(m_old - m_new)
        e = jnp.exp(s - m_new)
        l_ref[g] = alpha * l_ref[g] + jnp.sum(e, axis=-1, keepdims=True)
        e = e.astype(bf16)
        pv = _nt_dot(e, vt) if vt is not None else jnp.dot(e, v_rows, preferred_element_type=f32)
        acc_ref[g] = alpha * acc_ref[g] + pv
        m_ref[g] = m_new

    k0 = s_id * kt
    kpos = k0 + lax.broadcasted_iota(jnp.int32, (1, kt), 1)
    causal = jnp.where(kpos <= tpos, 0.0, NEG)
    for g in range(N_KV):
        q = q_ref[0, g]
        kt_g = jnp.concatenate([pages[p][0, 0, g] for p in range(pps)], axis=1).astype(bf16)
        vt_g = jnp.concatenate([pages[p][0, 1, g] for p in range(pps)], axis=1).astype(bf16)
        s = (jnp.dot(q, kt_g, preferred_element_type=f32)
             + jnp.dot(selq_ref[s_id, g], ex_ref[...], preferred_element_type=f32) + causal)
        online_update(g, s, vt_g, None)

    @pl.when(s_id == n_steps - 1)
    def _():
        npos = past_len + lax.broadcasted_iota(jnp.int32, (1, ts), 1)
        keep = wb_ref.shape[-1]
        wpos = past_len - keep + lax.broadcasted_iota(jnp.int32, (1, keep), 1)
        d_old = tpos - wpos
        d_new = tpos - npos
        mk_old = (d_old >= 0) & (d_old <= WINDOW) & (wpos >= 0)
        mk_new = (d_new >= 0) & (d_new <= WINDOW)
        for g in range(N_KV):
            q = q_ref[0, g]
            ksl = slice(g * HEAD_DIM, (g + 1) * HEAD_DIM)
            vsl = slice(K_COLS + g * HEAD_DIM, K_COLS + (g + 1) * HEAD_DIM)
            s_cur = jnp.where(npos <= tpos, _nt_dot(q, ns_ref[0, :, ksl].astype(bf16)), NEG)
            online_update(g, s_cur, None, ns_ref[0, :, vsl].astype(bf16))
            o_s = acc_ref[g] / jnp.maximum(l_ref[g], 1e-30)
            s_old = jnp.where(mk_old, jnp.dot(q, wb_ref[0, 0, g].astype(bf16), preferred_element_type=f32), NEG)
            s_new = jnp.where(mk_new, _nt_dot(q, nw_ref[0, :, ksl].astype(bf16)), NEG)
            m = jnp.maximum(jnp.max(s_old, axis=-1, keepdims=True), jnp.max(s_new, axis=-1, keepdims=True))
            e_old = jnp.where(mk_old, jnp.exp(s_old - m), 0.0)
            e_new = jnp.where(mk_new, jnp.exp(s_new - m), 0.0)
            den = jnp.sum(e_old, axis=-1, keepdims=True) + jnp.sum(e_new, axis=-1, keepdims=True)
            o_w = (_nt_dot(e_old.astype(bf16), wb_ref[0, 1, g].astype(bf16))
                   + jnp.dot(e_new.astype(bf16), nw_ref[0, :, vsl].astype(bf16), preferred_element_type=f32)
                   ) / jnp.maximum(den, 1e-30)
            gt = g_ref[0, g]
            o_ref[0, g] = gt[:, 0:1] * oc_ref[g] + gt[:, 1:2] * o_s + gt[:, 2:3] * o_w


def _nsa_sample(q_g, gates_g, summ, new_sel, win_t, new_win, pool_t, page_table, pps, past_len):
    DB, NP = page_table.shape
    ts = new_sel.shape[1]
    R = HPG * ts
    nbp = summ.shape[1]
    page = pool_t.shape[-1]
    keep = win_t.shape[-1]
    kt = pps * page
    nbs = kt // BLOCK
    expand = jnp.asarray(np.arange(kt)[None, :] // BLOCK == np.arange(nbs)[:, None], bf16)

    def page_spec(p):
        return pl.BlockSpec((1, 2, N_KV, HEAD_DIM, page), lambda b, s, pt: (pt[b, s * pps + p], 0, 0, 0, 0))

    per_b4 = lambda b, s, pt: (b, 0, 0, 0)
    per_b3 = lambda b, s, pt: (b, 0, 0)
    return pl.pallas_call(
        functools.partial(_nsa_sample_body, pps=pps, past_len=past_len, ts=ts),
        grid_spec=pltpu.PrefetchScalarGridSpec(
            num_scalar_prefetch=1,
            grid=(DB, NP // pps),
            in_specs=[page_spec(p) for p in range(pps)]
            + [pl.BlockSpec((1, N_KV, R, HEAD_DIM), per_b4),
               pl.BlockSpec((1, N_KV, R, 3), per_b4),
               pl.BlockSpec((1, nbp, KV_COLS), per_b3),
               pl.BlockSpec((1, ts, KV_COLS), per_b3),
               pl.BlockSpec((1, 2, N_KV, HEAD_DIM, keep), lambda b, s, pt: (b, 0, 0, 0, 0)),
               pl.BlockSpec((1, ts, KV_COLS), per_b3),
               pl.BlockSpec((nbs, kt), lambda b, s, pt: (0, 0))],
            out_specs=pl.BlockSpec((1, N_KV, R, HEAD_DIM), per_b4),
            scratch_shapes=[pltpu.VMEM((NP // pps, N_KV, R, nbs), bf16),
                            pltpu.VMEM((N_KV, R, 1), f32),
                            pltpu.VMEM((N_KV, R, 1), f32),
                            pltpu.VMEM((N_KV, R, HEAD_DIM), f32),
                            pltpu.VMEM((N_KV, R, HEAD_DIM), f32)]),
        out_shape=jax.ShapeDtypeStruct((DB, N_KV, R, HEAD_DIM), f32),
        compiler_params=_cparams(("parallel", "arbitrary")),
        name="nsa_sample",
    )(page_table, *([pool_t] * pps), q_g, gates_g, summ, new_sel, win_t, new_win, expand)


def _bmm(spec, a, b):
    return jnp.einsum(spec, a.astype(bf16), b.astype(bf16), preferred_element_type=f32)


def _unit_lower_solve(L, rhs, C, bs):
    _mm = functools.partial(_bmm, "hij,hjk->hik")
    ri = lax.broadcasted_iota(jnp.int32, (1, C, C), 1)
    ci = lax.broadcasted_iota(jnp.int32, (1, C, C), 2)
    same = (ri // bs) == (ci // bs)
    eye = jnp.where(ri == ci, 1.0, 0.0)
    D = jnp.where(same, L, 0.0)
    T = eye - D
    P = D
    n = 2
    while n < bs:
        P = _mm(P, P)
        T = T + _mm(T, P)
        n *= 2
    x = _mm(T, rhs)
    nblk = C // bs
    if nblk == 1:
        return x
    Mb = _mm(T, jnp.where(same, 0.0, L))
    factors = []
    Pm = Mb
    n = 2
    while n < nblk:
        Pm = _mm(Pm, Pm)
        factors.append(Pm)
        n *= 2
    for Pm in reversed(factors):
        x = x + _mm(Pm, x)
    return x - _mm(Mb, x)


def _rwkv_body(ps_ref, sp_ref, s0_ref, mu_ref, w0_ref, ww_ref, a0_ref, wa_ref, kk_ref, ka_ref, rk_ref,
               lnw_ref, lnb_ref, seg_ref, segt_ref, y_ref, so_ref, carry_ref, state_ref, *, C, bs):
    c = pl.program_id(1)
    nbat = ps_ref.shape[0]
    H = RW_HEADS

    @pl.when(c == 0)
    def _():
        carry_ref[...] = sp_ref[...]
        state_ref[...] = s0_ref[...].reshape(state_ref.shape)

    def heads(x):
        return [x[:, h * RW_HEAD:(h + 1) * RW_HEAD] for h in range(H)]

    ri = lax.broadcasted_iota(jnp.int32, (C, C), 0)
    ci = lax.broadcasted_iota(jnp.int32, (C, C), 1)
    tril = jnp.where(ci <= ri, 1.0, 0.0)
    rowi = lax.broadcasted_iota(jnp.int32, (C, 1), 0)

    x1_l, x2_l, kb_l, v_l, rk_l, etot_l = [], [], [], [], [], []
    for n in range(nbat):
        ps = ps_ref[n]
        prev = jnp.where(rowi == 0, carry_ref[n], pltpu.roll(ps, 1, axis=0))
        carry_ref[n] = ps[C - 1:C, :]
        z = ps + (prev - ps) * mu_ref[...]
        r = z[:, 0:RW_DIM]
        k = z[:, RW_DIM:2 * RW_DIM]
        v = z[:, 2 * RW_DIM:3 * RW_DIM]
        xw = z[:, 3 * RW_DIM:3 * RW_DIM + LORA_W]
        xa = z[:, 3 * RW_DIM + LORA_W:]
        u = -(w0_ref[...] + jnp.dot(jnp.tanh(xw).astype(bf16), ww_ref[...], preferred_element_type=f32))
        softplus = jnp.maximum(u, 0.0) + jnp.log(1.0 + jnp.exp(-jnp.abs(u)))
        lw = -jnp.exp(-softplus - 0.5)
        a = jax.nn.sigmoid(a0_ref[...] + jnp.dot(xa.astype(bf16), wa_ref[...], preferred_element_type=f32))
        kk = k * kk_ref[...]
        ss = jnp.dot(kk * kk, seg_ref[...], precision=HIGHEST, preferred_element_type=f32)
        kk = kk * jnp.dot(lax.rsqrt(jnp.maximum(ss, 1e-24)), segt_ref[...], precision=HIGHEST,
                          preferred_element_type=f32)
        bb = kk * a
        k2 = k * (1.0 + (a - 1.0) * ka_ref[...])
        G = jnp.dot(tril, lw, precision=HIGHEST, preferred_element_type=f32)
        g_end = G[C - 1:C, :]
        e_neg = jnp.exp(-G)
        e_end = jnp.exp(g_end - G)
        x1_l.append(heads(jnp.concatenate([kk * jnp.exp(G - lw), r * jnp.exp(G)], axis=0).astype(bf16)))
        x2_l.append(heads(jnp.concatenate([k2 * e_neg, bb * e_neg], axis=0).astype(bf16)))
        kb_l.append(heads(jnp.concatenate([k2 * e_end, -(bb * e_end)], axis=0).astype(bf16)))
        v_l.append(heads(v))
        rk_l.append(heads(r * k2 * rk_ref[...]))
        etot_l.append(heads(jnp.exp(g_end)))

    stack = lambda lst: jnp.stack([t for per_b in lst for t in per_b], axis=0)
    X1, X2, KB = stack(x1_l), stack(x2_l), stack(kb_l)
    V, RK, ETOT = stack(v_l), stack(rk_l), stack(etot_l)

    strict = (ci < ri)[None]
    incl = (ci <= ri)[None]
    S = state_ref[...]
    A = _bmm("hck,hdk->hcd", X1, X2)
    P = _bmm("hck,hvk->hcv", X1, S)
    a_kk = jnp.where(strict, A[:, :C, :C], 0.0)
    a_kb = jnp.where(strict, A[:, :C, C:], 0.0)
    rhs = P[:, :C] + _bmm("hcd,hdv->hcv", a_kk, V)
    sa = _unit_lower_solve(a_kb, rhs, C, bs)
    a_r = jnp.concatenate([jnp.where(incl, A[:, C:, :C], 0.0), jnp.where(incl, -A[:, C:, C:], 0.0)], axis=2)
    vs = jnp.concatenate([V, sa], axis=1)
    y = P[:, C:] + _bmm("hcd,hdv->hcv", a_r, vs)
    state_ref[...] = S * ETOT + _bmm("hcv,hck->hvk", vs, KB)
    mean = jnp.mean(y, axis=-1, keepdims=True)
    var = jnp.mean(jnp.square(y - mean), axis=-1, keepdims=True)
    yn = (y - mean) * lax.rsqrt(var + LN_X_EPS)
    bonus = jnp.sum(RK, axis=-1, keepdims=True) * V
    for n in range(nbat):
        for h in range(H):
            sl = slice(h * RW_HEAD, (h + 1) * RW_HEAD)
            i = n * H + h
            y_ref[n, :, sl] = (yn[i] * lnw_ref[:, sl] + lnb_ref[:, sl] + bonus[i]).astype(y_ref.dtype)

    so_ref[...] = state_ref[...].reshape(so_ref.shape)


def _rwkv(p_shift, shift_prev, s0, rw):
    mu, w0, w_lora_w, a0, w_lora_a, k_k, k_a, r_k, ln_w, ln_b = rw
    B, T, _ = p_shift.shape
    C = _pick_tile(T, 64, 8)
    bs = min(16, C)
    nbat = _pick_tile(B, RWKV_BATCH_PER_STEP, 1)
    seg_np = (np.arange(RW_DIM)[:, None] // RW_HEAD == np.arange(RW_HEADS)[None, :]).astype(np.float32)
    vec = lambda n: pl.BlockSpec((1, n), lambda b, c: (0, 0))
    row = lambda t: t.reshape(1, -1)
    y, s_new = pl.pallas_call(
        functools.partial(_rwkv_body, C=C, bs=bs),
        grid=(B // nbat, T // C),
        in_specs=[pl.BlockSpec((nbat, C, SHIFT_COLS), lambda b, c: (b, c, 0)),
                  pl.BlockSpec((nbat, 1, SHIFT_COLS), lambda b, c: (b, 0, 0)),
                  pl.BlockSpec((nbat, RW_HEADS, RW_HEAD, RW_HEAD), lambda b, c: (b, 0, 0, 0)),
                  vec(SHIFT_COLS), vec(RW_DIM),
                  pl.BlockSpec((LORA_W, RW_DIM), lambda b, c: (0, 0)),
                  vec(RW_DIM),
                  pl.BlockSpec((LORA_A, RW_DIM), lambda b, c: (0, 0)),
                  vec(RW_DIM), vec(RW_DIM), vec(RW_DIM), vec(RW_DIM), vec(RW_DIM),
                  pl.BlockSpec((RW_DIM, RW_HEADS), lambda b, c: (0, 0)),
                  pl.BlockSpec((RW_HEADS, RW_DIM), lambda b, c: (0, 0))],
        out_specs=[pl.BlockSpec((nbat, C, RW_DIM), lambda b, c: (b, c, 0)),
                   pl.BlockSpec((nbat, RW_HEADS, RW_HEAD, RW_HEAD), lambda b, c: (b, 0, 0, 0))],
        out_shape=[jax.ShapeDtypeStruct((B, T, RW_DIM), bf16),
                   jax.ShapeDtypeStruct((B, RW_HEADS, RW_HEAD, RW_HEAD), f32)],
        scratch_shapes=[pltpu.VMEM((nbat, 1, SHIFT_COLS), f32),
                        pltpu.VMEM((nbat * RW_HEADS, RW_HEAD, RW_HEAD), f32)],
        compiler_params=_cparams(("parallel", "arbitrary")),
        name="rwkv",
    )(p_shift, shift_prev.reshape(B, 1, SHIFT_COLS), s0, row(mu), row(w0), w_lora_w.astype(bf16), row(a0),
      w_lora_a.astype(bf16), row(k_k), row(k_a), row(r_k), row(ln_w), row(ln_b),
      jnp.asarray(seg_np), jnp.asarray(seg_np.T))
    return y, s_new


def _merge_body(x_ref, oa_ref, yr_ref, ga_ref, gb_ref, wa_ref, wb_ref, wo_ref, h_ref):
    ma = jnp.dot(oa_ref[...], wa_ref[...], preferred_element_type=f32)
    mb = jnp.dot(yr_ref[...], wb_ref[...], preferred_element_type=f32)
    m = jax.nn.sigmoid(ga_ref[...]) * ma + jax.nn.sigmoid(gb_ref[...]) * mb
    h_ref[...] = x_ref[...] + jnp.dot(m.astype(bf16), wo_ref[...], preferred_element_type=f32)


def _merge(x, o_att, y_rw, p_merge, w_a, w_b, w_o):
    M, D = x.shape
    tm = _pick_tile(M, 256, 8)
    row = lambda i: (i, 0)
    const = lambda i: (0, 0)
    return pl.pallas_call(
        _merge_body,
        grid=(M // tm,),
        in_specs=[pl.BlockSpec((tm, D), row),
                  pl.BlockSpec((tm, ATT_DIM), row),
                  pl.BlockSpec((tm, RW_DIM), row),
                  pl.BlockSpec((tm, D), lambda i: (i, 0)),
                  pl.BlockSpec((tm, D), lambda i: (i, 1)),
                  pl.BlockSpec((ATT_DIM, D), const),
                  pl.BlockSpec((RW_DIM, D), const),
                  pl.BlockSpec((D, D), const)],
        out_specs=pl.BlockSpec((tm, D), row),
        out_shape=jax.ShapeDtypeStruct((M, D), f32),
        compiler_params=_cparams(("parallel",)),
        name="merge",
    )(x, o_att, y_rw, p_merge, p_merge, w_a, w_b, w_o)


def _conv_ffn_body(h_ref, g_ref, wug_ref, wuv_ref, cwg_ref, cwv_ref, cbg_ref, cbv_ref, wd_ref,
                   pg_ref, pv_ref, y_ref, tg_ref, tv_ref, hn_ref, cg_ref, cv_ref, *, tm, seq_rows, tail, nsub):
    i = pl.program_id(1)
    j = pl.program_id(2)
    carried = tm <= seq_rows

    @pl.when(j == 0)
    def _():
        h = h_ref[0]
        ms = jnp.mean(h * h, axis=-1, keepdims=True)
        hn_ref[...] = (h * lax.rsqrt(ms + NORM_EPS) * g_ref[...]).astype(bf16)
        y_ref[0] = h

    def taps(cw_ref, cb_ref, u2, u1, u):
        return cb_ref[...] + cw_ref[0:1, :] * u2 + cw_ref[1:2, :] * u1 + cw_ref[2:3, :] * u

    def finish(rows, gate, val):
        act = (gate * jax.nn.sigmoid(gate) * val).astype(bf16)
        y_ref[0, rows, :] += jnp.dot(act, wd_ref[...], preferred_element_type=f32)

    if not carried:
        t_in = lax.broadcasted_iota(jnp.int32, (tm, 1), 0) % seq_rows
        hn = hn_ref[...]

        def conv(u, cw_ref, cb_ref, prev_ref):
            pr = prev_ref[0]
            u1 = jnp.where(t_in == 0, pltpu.roll(pr, tm - 1, axis=0), pltpu.roll(u, 1, axis=0))
            u2 = jnp.where(t_in == 0, pr, jnp.where(t_in == 1, pr, pltpu.roll(u, 2, axis=0)))
            return taps(cw_ref, cb_ref, u2, u1, u)

        ug = jnp.dot(hn, wug_ref[...], preferred_element_type=f32)
        uv = jnp.dot(hn, wuv_ref[...], preferred_element_type=f32)
        tg_ref[0, 0] = ug
        tv_ref[0, 0] = uv
        finish(slice(None), conv(ug, cwg_ref, cbg_ref, pg_ref), conv(uv, cwv_ref, cbv_ref, pv_ref))
        return

    @pl.when(i == 0)
    def _():
        cg_ref[j] = pg_ref[0]
        cv_ref[j] = pv_ref[0]

    ts = tm // nsub
    rowi = lax.broadcasted_iota(jnp.int32, (ts, 1), 0)
    prev_g = (cg_ref[j, 0:1, :], cg_ref[j, 1:2, :])
    prev_v = (cv_ref[j, 0:1, :], cv_ref[j, 1:2, :])

    def conv(u, cw_ref, cb_ref, prev):
        p2, p1 = prev
        u1 = jnp.where(rowi == 0, p1, pltpu.roll(u, 1, axis=0))
        u2 = jnp.where(rowi == 0, p2, jnp.where(rowi == 1, p1, pltpu.roll(u, 2, axis=0)))
        return taps(cw_ref, cb_ref, u2, u1, u), (u[ts - 2:ts - 1, :], u[ts - 1:ts, :])

    for sb in range(nsub):
        rows = slice(sb * ts, (sb + 1) * ts)
        hn = hn_ref[rows, :]
        ug = jnp.dot(hn, wug_ref[...], preferred_element_type=f32)
        uv = jnp.dot(hn, wuv_ref[...], preferred_element_type=f32)
        gate, prev_g = conv(ug, cwg_ref, cbg_ref, prev_g)
        val, prev_v = conv(uv, cwv_ref, cbv_ref, prev_v)
        finish(rows, gate, val)
    cg_ref[j] = jnp.concatenate(prev_g, axis=0)
    cv_ref[j] = jnp.concatenate(prev_v, axis=0)
    tg_ref[0, 0] = ug[ts - tail:ts, :]
    tv_ref[0, 0] = uv[ts - tail:ts, :]


def _conv_ffn(h, conv_prev, norm_g, w_up, conv_w, conv_b, w_down, *, fold):
    B, T, D = h.shape
    dff = w_down.shape[0]
    tf = _pick_tile(dff, 512, LANE)
    nf = dff // tf
    if not fold:
        nb_, tm = B, _pick_tile(T, 1024, 8)
        tail = 8
        hh = h
        prev = conv_prev
        prev_spec_g = pl.BlockSpec((1, CONV_W - 1, tf), lambda b, i, j: (b, 0, j))
        prev_spec_v = pl.BlockSpec((1, CONV_W - 1, tf), lambda b, i, j: (b, 0, nf + j))
    else:
        nb_, tm = 1, B * T
        tail = tm
        hh = h.reshape(1, B * T, D)
        assert T >= CONV_W - 1
        prev = jnp.concatenate([conv_prev, jnp.zeros((B, T - (CONV_W - 1), 2 * dff), f32)],
                               axis=1).reshape(1, B * T, 2 * dff)
        prev_spec_g = pl.BlockSpec((1, tm, tf), lambda b, i, j: (0, 0, j))
        prev_spec_v = pl.BlockSpec((1, tm, tf), lambda b, i, j: (0, 0, nf + j))
    nt = hh.shape[1] // tm
    nsub = FFN_ROW_SUBBLOCKS if (not fold and tm % (8 * FFN_ROW_SUBBLOCKS) == 0) else 1
    body = functools.partial(_conv_ffn_body, tm=tm, seq_rows=T, tail=tail, nsub=nsub)
    cw = conv_w
    cb = conv_b.reshape(1, 2 * dff)
    tail_spec = pl.BlockSpec((1, 1, tail, tf), lambda b, i, j: (b, i, 0, j))
    tail_shape = jax.ShapeDtypeStruct((nb_, nt, tail, dff), f32)
    y, ug, uv = pl.pallas_call(
        body,
        grid=(nb_, nt, nf),
        in_specs=[pl.BlockSpec((1, tm, D), lambda b, i, j: (b, i, 0)),
                  pl.BlockSpec((1, D), lambda b, i, j: (0, 0)),
                  pl.BlockSpec((D, tf), lambda b, i, j: (0, j)),
                  pl.BlockSpec((D, tf), lambda b, i, j: (0, nf + j)),
                  pl.BlockSpec((CONV_W, tf), lambda b, i, j: (0, j)),
                  pl.BlockSpec((CONV_W, tf), lambda b, i, j: (0, nf + j)),
                  pl.BlockSpec((1, tf), lambda b, i, j: (0, j)),
                  pl.BlockSpec((1, tf), lambda b, i, j: (0, nf + j)),
                  pl.BlockSpec((tf, D), lambda b, i, j: (j, 0)),
                  prev_spec_g, prev_spec_v],
        out_specs=[pl.BlockSpec((1, tm, D), lambda b, i, j: (b, i, 0)), tail_spec, tail_spec],
        out_shape=[jax.ShapeDtypeStruct(hh.shape, f32), tail_shape, tail_shape],
        scratch_shapes=[pltpu.VMEM((tm, D), bf16),
                        pltpu.VMEM((nf, CONV_W - 1, tf), f32),
                        pltpu.VMEM((nf, CONV_W - 1, tf), f32)],
        compiler_params=_cparams(("parallel", "arbitrary", "arbitrary")),
        name="conv_ffn",
    )(hh, norm_g.reshape(1, D), w_up, w_up, cw, cw, cb, cb, w_down, prev, prev)
    return y, ug[:, -1], uv[:, -1]


def _split_w_in(w_in):
    o = 0
    parts = []
    for n in (ATT_DIM, KV_COLS, KV_COLS, KV_COLS, 3 * N_HEADS, SHIFT_COLS, 2 * w_in.shape[0]):
        parts.append(w_in[:, o:o + n])
        o += n
    wq, wc, ws, ww, wg, wsh, wm = parts
    wg = jnp.pad(wg, ((0, 0), (0, LANE - 3 * N_HEADS)))
    w_att = jnp.concatenate([wq, wc, ws, ww, wg], axis=1).astype(bf16)
    return w_att, wsh.astype(bf16), wm.astype(bf16)


def _head_major(x, n):
    B, T, _ = x.shape
    return x.reshape(B, T, n, HEAD_DIM).transpose(0, 2, 1, 3)


def _mixer_inputs(x2d, pos_tab, norm_g, w_parts, consts, wc, with_summ):
    w_att, w_sh, w_mg = w_parts
    p_att = _norm_matmul(x2d, norm_g, w_att)
    p_shift = _norm_matmul(x2d, norm_g, w_sh)
    p_merge = _norm_matmul(x2d, norm_g, w_mg)
    post = _qk_post(p_att, pos_tab, consts, wc, with_summ)
    return post, p_shift, p_merge


def kernel(x_prompt, x_sample, cache_kv_cmp, cache_kv_sel, page_table, cache_kv_win, state_wkv, state_shift, state_conv, norm1_g, w_in, q_gain, k_gains, w_cmp, mu_shift, w0, w_lora_w, a0, w_lora_a, k_k, k_a, r_k, ln_x_w, ln_x_b, w_branch_a, w_branch_b, w_out, norm2_g, w_up, conv_w, conv_b, w_down):
    B, T, D = x_prompt.shape
    DB, TS, _ = x_sample.shape
    depth = w_in.shape[0]
    assert depth == 1, "single-layer trunk"
    l = 0
    page = cache_kv_cmp.shape[2]
    n_pages = page_table.shape[1]
    past_len = n_pages * page
    assert past_len % BLOCK == 0 and TS <= BLOCK and page % BLOCK == 0 and T % BLOCK == 0
    dff = w_down.shape[1]

    w_parts = _split_w_in(w_in[l])
    consts = _qk_consts(q_gain[l], k_gains[l])
    wc = _compress_weights(w_cmp[l])
    rw = (mu_shift[l], w0[l], w_lora_w[l], a0[l], w_lora_a[l], k_k[l], k_a[l], r_k[l], ln_x_w[l], ln_x_b[l])
    w_a = w_branch_a[l].astype(bf16)
    w_b = w_branch_b[l].astype(bf16)
    w_o = w_out[l].astype(bf16)
    w_u = w_up[l].astype(bf16)
    w_d = w_down[l].astype(bf16)

    xp = x_prompt.reshape(B * T, D)
    tabs_p = _rope_tables(jnp.arange(T, dtype=jnp.int32))
    ((q_pad, kvc, kvs, kvw, gates, summ, ks_aug, vs_pad, kw_pad, vw_pad),
     p_shift, p_merge) = _mixer_inputs(xp, tabs_p, norm1_g[l], w_parts, consts, wc, True)
    nb = T // BLOCK
    summ = summ.reshape(B, nb, KV_COLS)
    lane_pad = lambda x: jnp.pad(x, ((0, 0), (0, 0), (0, 0), (0, LANE - HEAD_DIM))).astype(bf16)
    kvw3 = kvw.reshape(B, T, KV_COLS)
    o_att = _nsa_prompt(q_pad,
                        lane_pad(_head_major(summ[:, :, :K_COLS], N_KV)),
                        lane_pad(_head_major(summ[:, :, K_COLS:], N_KV)),
                        ks_aug, vs_pad, kw_pad, vw_pad, gates.reshape(B, T, LANE))
    p_shift3 = p_shift.reshape(B, T, SHIFT_COLS)
    y_rw, wkv_p = _rwkv(p_shift3, jnp.zeros((B, SHIFT_COLS), f32),
                        jnp.zeros((B, RW_HEADS, RW_HEAD, RW_HEAD), f32), rw)
    h_p = _merge(xp, o_att.reshape(B * T, ATT_DIM), y_rw.reshape(B * T, RW_DIM), p_merge, w_a, w_b, w_o)
    y_p, ug, uv = _conv_ffn(h_p.reshape(B, T, D), jnp.zeros((B, CONV_W - 1, 2 * dff), f32), norm2_g[l],
                            w_u, conv_w[l], conv_b[l], w_d, fold=False)
    assert T >= CONV_W - 1
    conv_p = jnp.concatenate([ug[:, -(CONV_W - 1):], uv[:, -(CONV_W - 1):]], axis=-1)
    kv_shape_p = (1, B, T, 2, N_KV, HEAD_DIM)
    keep_p = min(WINDOW, T)
    outs_p = (y_p,
              kvc.reshape(kv_shape_p), kvs.reshape(kv_shape_p),
              kvw3[:, T - keep_p:].reshape(1, B, keep_p, 2, N_KV, HEAD_DIM),
              wkv_p[None], p_shift3[:, -1][None], conv_p[None])

    xs = x_sample.reshape(DB * TS, D)
    pos_s = past_len + jnp.arange(TS, dtype=jnp.int32)
    tabs_s = tuple(jnp.tile(t, (DB, 1)) for t in _rope_tables(pos_s))
    (q, kvc_s, kvs_s, kvw_s, gates), p_shift, p_merge = _mixer_inputs(xs, tabs_s, norm1_g[l], w_parts, consts, wc, False)
    pps = _pick_tile(n_pages, 8, 1)
    summ_s = _compress_pool(_pages_row_minor(cache_kv_cmp[l]), page_table, w_cmp[l], pps)
    summ_s = summ_s.reshape(DB, past_len // BLOCK, KV_COLS)
    R = HPG * TS
    q_g = _head_major(q.reshape(DB, TS, ATT_DIM), N_HEADS).reshape(DB, N_KV, R, HEAD_DIM)
    gates_g = (gates[:, :3 * N_HEADS].reshape(DB, TS, N_HEADS, 3).transpose(0, 2, 1, 3)
               .reshape(DB, N_KV, R, 3))
    keep = cache_kv_win.shape[2]
    kvw_s3 = kvw_s.reshape(DB, TS, KV_COLS)
    o_g = _nsa_sample(q_g, gates_g, summ_s, kvs_s.reshape(DB, TS, KV_COLS), _pages_row_minor(cache_kv_win[l]),
                      kvw_s3, _pages_row_minor(cache_kv_sel[l]), page_table, pps, past_len)
    o_att_s = (o_g.reshape(DB, N_HEADS, TS, HEAD_DIM).transpose(0, 2, 1, 3)
               .reshape(DB * TS, ATT_DIM).astype(bf16))
    p_shift3s = p_shift.reshape(DB, TS, SHIFT_COLS)
    y_rw_s, wkv_s = _rwkv(p_shift3s, state_shift[l], state_wkv[l], rw)
    h_s = _merge(xs, o_att_s, y_rw_s.reshape(DB * TS, RW_DIM), p_merge, w_a, w_b, w_o)
    y_s, ug, uv = _conv_ffn(h_s.reshape(DB, TS, D), state_conv[l], norm2_g[l],
                            w_u, conv_w[l], conv_b[l], w_d, fold=True)
    up_s = jnp.concatenate([state_conv[l],
                            jnp.concatenate([ug.reshape(DB, TS, dff), uv.reshape(DB, TS, dff)], axis=-1)], axis=1)
    conv_s = up_s[:, TS:]
    win_s = jnp.concatenate([cache_kv_win[l], kvw_s.reshape(DB, TS, 2, N_KV, HEAD_DIM)], axis=1)[:, TS:]
    kv_shape_s = (1, DB, TS, 2, N_KV, HEAD_DIM)

    return (outs_p[0], y_s.reshape(DB, TS, D),
            outs_p[1], kvc_s.reshape(kv_shape_s),
            outs_p[2], kvs_s.reshape(kv_shape_s),
            outs_p[3], win_s.reshape(1, DB, keep, 2, N_KV, HEAD_DIM),
            outs_p[4], wkv_s[None],
            outs_p[5], p_shift3s[:, -1][None],
            outs_p[6], conv_s[None])
```

```python
import functools

import numpy as np
import jax
import jax.numpy as jnp
from jax import lax
from jax.experimental import pallas as pl
from jax.experimental.pallas import tpu as pltpu

f32 = jnp.float32
bf16 = jnp.bfloat16

N_HEADS = 16
N_KV = 4
HPG = N_HEADS // N_KV
HEAD_DIM = 64
ROPE_DIM = HEAD_DIM // 4
ROPE_THETA = 500000.0
BLOCK = 64
N_SEL = 16
WINDOW = 512
RW_HEADS = 16
RW_HEAD = 64
RW_DIM = RW_HEADS * RW_HEAD
LORA_W = 64
LORA_A = 64
LN_X_EPS = 64e-5
CONV_W = 3
NORM_EPS = 1e-6
ATT_DIM = N_HEADS * HEAD_DIM
KV_COLS = 2 * N_KV * HEAD_DIM
K_COLS = N_KV * HEAD_DIM
SHIFT_COLS = 3 * RW_DIM + LORA_W + LORA_A
NEG = -1e30
LOG2E = 1.4426950408889634

LANE = 128
VMEM_LIMIT = 56 * 1024 * 1024
QKV_COLS = ATT_DIM + 3 * KV_COLS
ATT_PROJ_COLS = QKV_COLS + LANE
N_NORM_HEADS = QKV_COLS // HEAD_DIM
HIGHEST = lax.Precision.HIGHEST
RWKV_BATCH_PER_STEP = 2
FFN_ROW_SUBBLOCKS = 2
NSA_GROUPS_PER_STEP = 1
PAGES_PER_STEP = 16


def _cparams(sem):
    return pltpu.CompilerParams(dimension_semantics=sem, vmem_limit_bytes=VMEM_LIMIT)


def _pick_tile(n, cap, mult):
    best = None
    for t in range(mult, min(n, cap) + 1, mult):
        if n % t == 0:
            best = t
    assert best is not None, (n, cap, mult)
    return best


def _norm_matmul_body(x_ref, g_ref, w_ref, o_ref, xn_ref):
    @pl.when(pl.program_id(1) == 0)
    def _():
        x = x_ref[...]
        ms = jnp.mean(x * x, axis=-1, keepdims=True)
        xn_ref[...] = (x * lax.rsqrt(ms + NORM_EPS) * g_ref[...]).astype(bf16)

    o_ref[...] = jnp.dot(xn_ref[...], w_ref[...], preferred_element_type=f32)


def _norm_matmul(x, gain, w):
    M, D = x.shape
    N = w.shape[1]
    tm = _pick_tile(M, 1024, 8)
    tn = _pick_tile(N, 1024, LANE)
    return pl.pallas_call(
        _norm_matmul_body,
        grid=(M // tm, N // tn),
        in_specs=[pl.BlockSpec((tm, D), lambda i, j: (i, 0)),
                  pl.BlockSpec((1, D), lambda i, j: (0, 0)),
                  pl.BlockSpec((D, tn), lambda i, j: (0, j))],
        out_specs=pl.BlockSpec((tm, tn), lambda i, j: (i, j)),
        out_shape=jax.ShapeDtypeStruct((M, N), f32),
        scratch_shapes=[pltpu.VMEM((tm, D), bf16)],
        compiler_params=_cparams(("parallel", "arbitrary")),
        name="norm_matmul",
    )(x, gain.reshape(1, D), w)


def _qk_post_body(p_ref, cos_ref, sa_ref, sb_ref, gvec_ref, isk_ref, seg_ref, segt_ref, wc_ref,
                  q_ref, kvc_ref, kvs_ref, kvw_ref, gate_ref, *prompt_refs, nt):
    y = p_ref[:, :QKV_COLS]
    isk = isk_ref[...] > 0.5
    sq_hi, sq_lo = _split_bf16(y * y)
    ss = (jnp.dot(sq_hi, seg_ref[...], preferred_element_type=f32)
          + jnp.dot(sq_lo, seg_ref[...], preferred_element_type=f32)) * (1.0 / HEAD_DIM)
    rs_hi, rs_lo = _split_bf16(lax.rsqrt(ss + NORM_EPS))
    rb = (jnp.dot(rs_hi, segt_ref[...], preferred_element_type=f32)
          + jnp.dot(rs_lo, segt_ref[...], preferred_element_type=f32))
    yn = jnp.where(isk, y * rb * gvec_ref[...], y)
    reps = QKV_COLS // LANE
    cos = jnp.where(isk, jnp.concatenate([cos_ref[...]] * reps, axis=1), 1.0)
    sa = jnp.where(isk, jnp.concatenate([sa_ref[...]] * reps, axis=1), 0.0)
    sb = jnp.where(isk, jnp.concatenate([sb_ref[...]] * reps, axis=1), 0.0)
    half = ROPE_DIM // 2
    out = (yn * cos + pltpu.roll(yn, QKV_COLS - half, axis=1) * sa + pltpu.roll(yn, half, axis=1) * sb)
    qs = out[:, :ATT_DIM] * (HEAD_DIM ** -0.5 * (LOG2E if prompt_refs else 1.0))
    kvc = out[:, ATT_DIM:ATT_DIM + KV_COLS]
    kvs = out[:, ATT_DIM + KV_COLS:ATT_DIM + 2 * KV_COLS]
    kvw = out[:, ATT_DIM + 2 * KV_COLS:]
    kvc_ref[...] = kvc
    kvs_ref[...] = kvs
    kvw_ref[...] = kvw
    gate_ref[...] = jax.nn.sigmoid(p_ref[:, QKV_COLS:])
    if not prompt_refs:
        q_ref[...] = qs.astype(bf16)
        return
    summ_ref, ksa_ref, vsp_ref, kwp_ref, vwp_ref = prompt_refs
    tm = kvc.shape[0]
    blk = kvc.reshape(tm // BLOCK, BLOCK, KV_COLS) * wc_ref[...][None]
    summ_ref[0] = jnp.sum(blk, axis=1)
    zeros = jnp.zeros((tm, LANE - HEAD_DIM), f32)
    t0 = (pl.program_id(0) % nt) * tm
    blk_of_row = (t0 + lax.broadcasted_iota(jnp.int32, zeros.shape, 0)) // BLOCK
    onehot = jnp.where(lax.broadcasted_iota(jnp.int32, zeros.shape, 1) == blk_of_row, 1.0, 0.0)
    hd = lambda x, h: x[:, h * HEAD_DIM:(h + 1) * HEAD_DIM]
    pad = lambda x, tail: jnp.concatenate([x, tail], axis=1).astype(bf16)
    for h in range(N_HEADS):
        q_ref[0, h] = pad(hd(qs, h), zeros)
    for g in range(N_KV):
        ksa_ref[0, g] = pad(hd(kvs, g), onehot)
        vsp_ref[0, g] = pad(hd(kvs, N_KV + g), zeros)
        kwp_ref[0, g] = pad(hd(kvw, g), zeros)
        vwp_ref[0, g] = pad(hd(kvw, N_KV + g), zeros)


def _qk_post(p_att, tabs, consts, wc, prompt):
    M = p_att.shape[0]
    cos_t, sa_t, sb_t = tabs
    Tt = cos_t.shape[0]
    tm = _pick_tile(Tt, 256, BLOCK if prompt else 8)
    nt = Tt // tm
    gvec, isk, seg, segt = consts
    row = lambda i: (i, 0)
    tab = lambda i: (i % nt, 0)
    const = lambda i: (0, 0)
    kv_shape = jax.ShapeDtypeStruct((M, KV_COLS), f32)
    kv_spec = pl.BlockSpec((tm, KV_COLS), row)
    out_shape = [jax.ShapeDtypeStruct((M, ATT_DIM), bf16), kv_shape, kv_shape, kv_shape,
                 jax.ShapeDtypeStruct((M, LANE), f32)]
    out_specs = [pl.BlockSpec((tm, ATT_DIM), row), kv_spec, kv_spec, kv_spec, pl.BlockSpec((tm, LANE), row)]
    if prompt:
        assert Tt // BLOCK <= LANE - HEAD_DIM, "one-hot block lanes"
        B = M // Tt
        hm = lambda n: jax.ShapeDtypeStruct((B, n, Tt, LANE), bf16)
        hm_spec = lambda n: pl.BlockSpec((1, n, tm, LANE), lambda i: (i // nt, 0, i % nt, 0))
        out_shape[0], out_specs[0] = hm(N_HEADS), hm_spec(N_HEADS)
        out_shape += [jax.ShapeDtypeStruct((M // tm, tm // BLOCK, KV_COLS), f32)] + [hm(N_KV)] * 4
        out_specs += [pl.BlockSpec((1, tm // BLOCK, KV_COLS), lambda i: (i, 0, 0))] + [hm_spec(N_KV)] * 4
    return pl.pallas_call(
        functools.partial(_qk_post_body, nt=nt),
        grid=(M // tm,),
        in_specs=[pl.BlockSpec((tm, ATT_PROJ_COLS), row),
                  pl.BlockSpec((tm, LANE), tab), pl.BlockSpec((tm, LANE), tab), pl.BlockSpec((tm, LANE), tab),
                  pl.BlockSpec((1, QKV_COLS), const), pl.BlockSpec((1, QKV_COLS), const),
                  pl.BlockSpec((QKV_COLS, N_NORM_HEADS), const), pl.BlockSpec((N_NORM_HEADS, QKV_COLS), const),
                  pl.BlockSpec((BLOCK, KV_COLS), const)],
        out_specs=out_specs,
        out_shape=out_shape,
        compiler_params=_cparams(("parallel",)),
        name="qk_post",
    )(p_att, cos_t, sa_t, sb_t, gvec, isk, seg, segt, wc)


def _rope_tables(pos):
    half = ROPE_DIM // 2
    inv = ROPE_THETA ** (-jnp.arange(half, dtype=f32) * 2.0 / ROPE_DIM)
    ang = pos.astype(f32)[:, None] * inv[None, :]
    cos, sin = jnp.cos(ang), jnp.sin(ang)
    n = pos.shape[0]
    ones = jnp.ones((n, HEAD_DIM - ROPE_DIM), f32)
    zeros = jnp.zeros((n, HEAD_DIM - half), f32)
    cos_h = jnp.concatenate([cos, cos, ones], axis=1)
    sa_h = jnp.concatenate([-sin, zeros], axis=1)
    sb_h = jnp.concatenate([jnp.zeros((n, half), f32), sin, jnp.zeros((n, HEAD_DIM - ROPE_DIM), f32)], axis=1)
    rep = LANE // HEAD_DIM
    return tuple(jnp.concatenate([t] * rep, axis=1) for t in (cos_h, sa_h, sb_h))


def _qk_consts(q_gain, k_gains):
    ones_v = jnp.ones((K_COLS,), f32)
    gvec = jnp.concatenate([jnp.tile(q_gain, N_HEADS)]
                           + [t for s in range(3) for t in (jnp.tile(k_gains[s], N_KV), ones_v)])
    isk_np = np.concatenate([np.ones(ATT_DIM)] + [np.ones(K_COLS), np.zeros(K_COLS)] * 3).astype(np.float32)
    seg_np = (np.arange(QKV_COLS)[:, None] // HEAD_DIM == np.arange(N_NORM_HEADS)[None, :]).astype(np.float32)
    seg_np = seg_np * isk_np[:, None]
    return (gvec.reshape(1, QKV_COLS), jnp.asarray(isk_np).reshape(1, QKV_COLS),
            jnp.asarray(seg_np, bf16), jnp.asarray(seg_np.T, bf16))


def _compress_weights(w_cmp):
    return jnp.concatenate([jnp.tile(w_cmp[c], (1, N_KV)) for c in range(2)], axis=1)


def _select_blocks(imp, n_pick):
    nb = imp.shape[-1]
    lane = lax.broadcasted_iota(jnp.int32, imp.shape, imp.ndim - 1).astype(f32)
    sel = jnp.zeros(imp.shape, f32)
    for _ in range(min(n_pick, nb)):
        mx = jnp.max(imp, axis=-1, keepdims=True)
        idx = jnp.min(jnp.where(imp == mx, lane, float(nb)), axis=-1, keepdims=True)
        hit = (lane == idx) & (mx >= 0.0)
        sel = jnp.where(hit, 1.0, sel)
        imp = jnp.where(lane == idx, -2.0, imp)
    return sel


def _softmax_parts(s, mask):
    s = jnp.where(mask, s, NEG)
    m = jnp.max(s, axis=-1, keepdims=True)
    e = jnp.where(mask, jnp.exp(s - m), 0.0)
    return m, e


def _nt_dot(a, b):
    return lax.dot_general(a, b, (((1,), (1,)), ((), ())), preferred_element_type=f32)


def _nsa_prompt_body(q_ref, kc_ref, vc_ref, ks_ref, vs_ref, kw_ref, vw_ref, g_ref, gx_ref, place_ref,
                     sb_ref, wb_ref, o_ref, *, tq, T, kc_tile, slab):
    i = pl.program_id(2)
    q0 = i * tq
    R = HPG * tq
    nb = T // BLOCK
    gps = kc_ref.shape[1]
    tn_dims = (((0,), (0,)), ((), ()))
    tpos = q0 + lax.broadcasted_iota(jnp.int32, (1, R), 1) % tq
    blk = lax.broadcasted_iota(jnp.int32, (nb, R), 0)
    vis = (blk + 1) * BLOCK - 1 <= tpos
    blk_q = lax.broadcasted_iota(jnp.int32, (nb, tq), 0)
    cur_q = (q0 + lax.broadcasted_iota(jnp.int32, (1, tq), 1)) // BLOCK
    cand = blk_q < cur_q
    lane = lax.broadcasted_iota(jnp.int32, (1, LANE), 1)
    in_blk_lanes = jnp.where((lane >= HEAD_DIM) & (lane < HEAD_DIM + nb), 1.0, 0.0)

    def compressed_and_selection(gi):
        q = q_ref[0, gi * HPG:(gi + 1) * HPG].reshape(R, LANE)
        s_c = jnp.where(vis, _nt_dot(kc_ref[0, gi], q), NEG)
        e_c = jnp.where(vis, jnp.exp2(s_c - jnp.max(s_c, axis=0, keepdims=True)), 0.0)
        p_c = e_c / jnp.maximum(jnp.sum(e_c, axis=0, keepdims=True), 1e-30)
        o_c = lax.dot_general(p_c.astype(bf16), vc_ref[0, gi], tn_dims, preferred_element_type=f32)
        imp = p_c[:, 0:tq]
        for hh in range(1, HPG):
            imp = imp + p_c[:, hh * tq:(hh + 1) * tq]
        imp = jnp.where(cand, imp, -1.0)
        ahead = jnp.zeros((nb, tq), f32)
        for m in range(nb):
            row_m = imp[m:m + 1, :]
            tie = jnp.where(blk_q > m, 1.0, 0.0)
            ahead = ahead + jnp.where(row_m > imp, 1.0, jnp.where(row_m == imp, tie, 0.0))
        sel = jnp.where(cand, jnp.where(ahead < N_SEL - 1, 1.0, 0.0), jnp.where(blk_q == cur_q, 1.0, 0.0))
        sel_l = lax.dot_general(sel.astype(bf16), place_ref[...], tn_dims, preferred_element_type=f32)
        q_off = ((in_blk_lanes - sel_l) * NEG).astype(bf16)
        return q, q + jnp.concatenate([q_off] * HPG, axis=0), o_c

    def attend(carry, qq, k, v, bias):
        m, l, acc = carry
        s = _nt_dot(qq, k)
        if bias is not None:
            s = (s.reshape(HPG, tq, s.shape[-1]) + bias[None]).reshape(s.shape)
        m_new = jnp.maximum(m, jnp.max(s, axis=-1, keepdims=True))
        alpha = jnp.exp2(m - m_new)
        e = jnp.exp2(s - m_new)
        l = alpha * l + jnp.sum(e, axis=-1, keepdims=True)
        acc = alpha * acc + jnp.dot(e.astype(bf16), v, preferred_element_type=f32)
        return m_new, l, acc

    init = (jnp.full((R, 1), NEG, f32), jnp.zeros((R, 1), f32), jnp.zeros((R, LANE), f32))

    groups = [compressed_and_selection(gi) for gi in range(gps)]

    def sel_step(c, carries):
        k0 = pl.multiple_of(c * kc_tile, kc_tile)
        return tuple(attend(carries[gi], groups[gi][1], ks_ref[0, gi, pl.ds(k0, kc_tile), :],
                            vs_ref[0, gi, pl.ds(k0, kc_tile), :], None) for gi in range(gps))

    n_full = q0 // kc_tile
    carries = lax.fori_loop(0, n_full, sel_step, (init,) * gps)
    kd = pl.multiple_of(n_full * kc_tile, kc_tile)
    w0 = pl.multiple_of(jnp.clip(q0 + tq - slab, 0, T - slab), tq)
    g_hi, g_lo = _split_bf16(g_ref[0])
    outs = []
    for gi in range(gps):
        q, q_sel, o_c = groups[gi]
        _, l_s, acc_s = attend(carries[gi], q_sel, ks_ref[0, gi, pl.ds(kd, kc_tile), :],
                               vs_ref[0, gi, pl.ds(kd, kc_tile), :], sb_ref[0])
        o_s = acc_s / jnp.maximum(l_s, 1e-30)
        _, l_w, acc_w = attend(init, q, kw_ref[0, gi, pl.ds(w0, slab), :], vw_ref[0, gi, pl.ds(w0, slab), :],
                               wb_ref[0])
        o_w = acc_w / jnp.maximum(l_w, 1e-30)
        G = (jnp.dot(g_hi, gx_ref[gi], preferred_element_type=f32)
             + jnp.dot(g_lo, gx_ref[gi], preferred_element_type=f32))
        for hh in range(HPG):
            rs = slice(hh * tq, (hh + 1) * tq)
            gcol = lambda j: G[:, (hh * 3 + j) * LANE:(hh * 3 + j + 1) * LANE]
            o_h = gcol(0) * o_c[rs] + gcol(1) * o_s[rs] + gcol(2) * o_w[rs]
            outs.append(o_h[:, :HEAD_DIM])
    o_ref[0] = jnp.concatenate(outs, axis=1).astype(o_ref.dtype)


def _nsa_prompt(q_pad, kc_pad, vc_pad, ks_aug, vs_pad, kw_pad, vw_pad, gates):
    B, _, T, _ = q_pad.shape
    tq = _pick_tile(T, 256, 16)
    kc_tile = _pick_tile(T, 512, tq)
    slab = min(T, WINDOW + tq)
    nb = T // BLOCK
    nq = T // tq
    place = np.zeros((nb, LANE), np.float32)
    place[np.arange(nb), HEAD_DIM + np.arange(nb)] = 1.0
    gx = np.zeros((N_KV, LANE, 3 * HPG, LANE), np.float32)
    for g in range(N_KV):
        for c in range(3 * HPG):
            gx[g, g * 3 * HPG + c, c, :] = 1.0
    gx = gx.reshape(N_KV, LANE, 3 * HPG * LANE)
    r = np.arange(tq)[:, None]
    nrel = kc_tile // tq
    sel_bias = np.stack([np.where(np.arange(kc_tile)[None, :] <= rel * tq + r, 0.0, NEG) for rel in range(nrel)])
    n_wb = min(nq, WINDOW // tq + 1) if slab == WINDOW + tq else nq
    win_bias = []
    for i in range(n_wb):
        w0 = min(max(i * tq + tq - slab, 0), T - slab)
        dist = (i * tq + r) - (w0 + np.arange(slab)[None, :])
        win_bias.append(np.where((dist >= 0) & (dist <= WINDOW), 0.0, NEG))
    win_bias = np.stack(win_bias)
    gps = NSA_GROUPS_PER_STEP
    assert N_KV % gps == 0
    kv_spec = pl.BlockSpec((1, gps, T, LANE), lambda b, g, i: (b, g, 0, 0))
    c_spec = pl.BlockSpec((1, gps, nb, LANE), lambda b, g, i: (b, g, 0, 0))
    return pl.pallas_call(
        functools.partial(_nsa_prompt_body, tq=tq, T=T, kc_tile=kc_tile, slab=slab),
        grid=(B, N_KV // gps, nq),
        in_specs=[pl.BlockSpec((1, gps * HPG, tq, LANE), lambda b, g, i: (b, g, i, 0)),
                  c_spec, c_spec, kv_spec, kv_spec, kv_spec, kv_spec,
                  pl.BlockSpec((1, tq, LANE), lambda b, g, i: (b, i, 0)),
                  pl.BlockSpec((gps, LANE, 3 * HPG * LANE), lambda b, g, i: (g, 0, 0)),
                  pl.BlockSpec((nb, LANE), lambda b, g, i: (0, 0)),
                  pl.BlockSpec((1, tq, kc_tile), lambda b, g, i: (i % nrel, 0, 0)),
                  pl.BlockSpec((1, tq, slab), lambda b, g, i: (jnp.minimum(i, n_wb - 1), 0, 0))],
        out_specs=pl.BlockSpec((1, tq, gps * HPG * HEAD_DIM), lambda b, g, i: (b, i, g)),
        out_shape=jax.ShapeDtypeStruct((B, T, ATT_DIM), bf16),
        compiler_params=_cparams(("parallel", "parallel", "arbitrary")),
        name="nsa_prompt",
    )(q_pad, kc_pad, vc_pad, ks_aug, vs_pad, kw_pad, vw_pad, gates,
      jnp.asarray(gx, bf16), jnp.asarray(place, bf16), jnp.asarray(sel_bias, f32), jnp.asarray(win_bias, f32))


def _pages_row_minor(pool):
    return jnp.transpose(pool, (0, 2, 3, 4, 1))


def _split_bf16(x):
    hi = x.astype(bf16)
    return hi, (x - hi.astype(f32)).astype(bf16)


def _compress_pool_body(pt_ref, *refs, pps):
    wt = refs[pps][...]
    seg = refs[pps + 1][...]
    out_ref = refs[pps + 2]
    for p in range(pps):
        page = refs[p][0]
        x = (page * wt).reshape(KV_COLS, page.shape[-1])
        hi, lo = _split_bf16(x)
        out_ref[0, p] = _nt_dot(seg, hi) + _nt_dot(seg, lo)


def _compress_pool(pool_t, page_table, w_cmp, pps):
    DB, NP = page_table.shape
    page = pool_t.shape[-1]
    bpp = page // BLOCK
    wt = jnp.tile(jnp.transpose(w_cmp, (0, 2, 1)), (1, 1, bpp))[:, None]
    seg = jnp.asarray(np.arange(page)[None, :] // BLOCK == np.arange(bpp)[:, None], bf16)

    def page_spec(p):
        return pl.BlockSpec((1, 2, N_KV, HEAD_DIM, page), lambda b, s, pt: (pt[b, s * pps + p], 0, 0, 0, 0))

    return pl.pallas_call(
        functools.partial(_compress_pool_body, pps=pps),
        grid_spec=pltpu.PrefetchScalarGridSpec(
            num_scalar_prefetch=1,
            grid=(DB, NP // pps),
            in_specs=[page_spec(p) for p in range(pps)]
            + [pl.BlockSpec((2, 1, HEAD_DIM, page), lambda b, s, pt: (0, 0, 0, 0)),
               pl.BlockSpec((bpp, page), lambda b, s, pt: (0, 0))],
            out_specs=pl.BlockSpec((1, pps, bpp, KV_COLS), lambda b, s, pt: (b, s, 0, 0))),
        out_shape=jax.ShapeDtypeStruct((DB, NP, bpp, KV_COLS), f32),
        compiler_params=_cparams(("parallel", "arbitrary")),
        name="compress_pool",
    )(page_table, *([pool_t] * pps), wt, seg)


def _nsa_sample_body(pt_ref, *refs, pps, past_len, ts):
    pages = refs[:pps]
    (q_ref, g_ref, summ_ref, ns_ref, wb_ref, nw_ref, ex_ref, o_ref,
     selq_ref, m_ref, l_ref, acc_ref, oc_ref) = refs[pps:]
    s_id = pl.program_id(1)
    n_steps = pl.num_programs(1)
    R = HPG * ts
    nbp = summ_ref.shape[1]
    page = pages[0].shape[-1]
    kt = pps * page
    nbs = kt // BLOCK
    row = lax.broadcasted_iota(jnp.int32, (R, 1), 0)
    tpos = past_len + row % ts

    @pl.when(s_id == 0)
    def _():
        blk = lax.broadcasted_iota(jnp.int32, (R, nbp), 1)
        vis = (blk + 1) * BLOCK - 1 <= tpos
        cols = lambda g, c: summ_ref[0, :, c * K_COLS + g * HEAD_DIM:c * K_COLS + (g + 1) * HEAD_DIM].astype(bf16)
        s_c = jnp.concatenate([jnp.where(vis, _nt_dot(q_ref[0, g], cols(g, 0)), NEG) for g in range(N_KV)], axis=0)
        e_c = jnp.exp(s_c - jnp.max(s_c, axis=-1, keepdims=True))
        e_c = jnp.where(jnp.concatenate([vis] * N_KV, axis=0), e_c, 0.0)
        p_c = e_c / jnp.maximum(jnp.sum(e_c, axis=-1, keepdims=True), 1e-30)
        imps = []
        for g in range(N_KV):
            p_g = p_c[g * R:(g + 1) * R]
            oc_ref[g] = jnp.dot(p_g.astype(bf16), cols(g, 1), preferred_element_type=f32)
            imp = p_g[0:ts]
            for hh in range(1, HPG):
                imp = imp + p_g[hh * ts:(hh + 1) * ts]
            imps.append(imp)
        blk_q = lax.broadcasted_iota(jnp.int32, (N_KV * ts, nbp), 1)
        cur_q = (past_len + lax.broadcasted_iota(jnp.int32, (N_KV * ts, 1), 0) % ts) // BLOCK
        imp = jnp.where(blk_q < cur_q, jnp.concatenate(imps, axis=0), -1.0)
        off = ((1.0 - _select_blocks(imp, N_SEL - 1)) * NEG).astype(bf16)
        for g in range(N_KV):
            off_g = jnp.concatenate([off[g * ts:(g + 1) * ts]] * HPG, axis=0)
            for s in range(nbp // nbs):
                selq_ref[s, g] = off_g[:, s * nbs:(s + 1) * nbs]
        m_ref[...] = jnp.full(m_ref.shape, NEG, f32)
        l_ref[...] = jnp.zeros(l_ref.shape, f32)
        acc_ref[...] = jnp.zeros(acc_ref.shape, f32)

    def online_update(s_groups, pv_of_group):
        s = jnp.concatenate(s_groups, axis=0)
        m_old = m_ref[...]
        m_new = jnp.maximum(m_old, jnp.max(s, axis=-1, keepdims=True))
        alpha = jnp.exp(m_old - m_new)
        e = jnp.exp(s - m_new)
        l_ref[...] = alpha * l_ref[...] + jnp.sum(e, axis=-1, keepdims=True)
        e = e.astype(bf16)
        pv = jnp.concatenate([pv_of_group(g, e[g * R:(g + 1) * R]) for g in range(N_KV)], axis=0)
        acc_ref[...] = alpha * acc_ref[...] + pv
        m_ref[...] = m_new

    k0 = s_id * kt
    kpos = k0 + lax.broadcasted_iota(jnp.int32, (1, kt), 1)
    causal = jnp.where(kpos <= tpos, 0.0, NEG)
    page_rows = lambda c, g: jnp.concatenate([pages[p][0, c, g] for p in range(pps)], axis=1).astype(bf16)
    online_update(
        [jnp.dot(q_ref[0, g], page_rows(0, g), preferred_element_type=f32)
         + jnp.dot(selq_ref[s_id, g], ex_ref[...], preferred_element_type=f32) + causal for g in range(N_KV)],
        lambda g, e: _nt_dot(e, page_rows(1, g)))

    @pl.when(s_id == n_steps - 1)
    def _():
        npos = past_len + lax.broadcasted_iota(jnp.int32, (1, ts), 1)
        keep = wb_ref.shape[-1]
        wpos = past_len - keep + lax.broadcasted_iota(jnp.int32, (1, keep), 1)
        d_old = tpos - wpos
        d_new = tpos - npos
        mk_old = (d_old >= 0) & (d_old <= WINDOW) & (wpos >= 0)
        mk_new = (d_new >= 0) & (d_new <= WINDOW)
        ksl_of = lambda g: slice(g * HEAD_DIM, (g + 1) * HEAD_DIM)
        vsl_of = lambda g: slice(K_COLS + g * HEAD_DIM, K_COLS + (g + 1) * HEAD_DIM)
        online_update(
            [jnp.where(npos <= tpos, _nt_dot(q_ref[0, g], ns_ref[0, :, ksl_of(g)].astype(bf16)), NEG)
             for g in range(N_KV)],
            lambda g, e: jnp.dot(e, ns_ref[0, :, vsl_of(g)].astype(bf16), preferred_element_type=f32))
        o_s_all = acc_ref[...] / jnp.maximum(l_ref[...], 1e-30)
        for g in range(N_KV):
            q = q_ref[0, g]
            ksl, vsl = ksl_of(g), vsl_of(g)
            o_s = o_s_all[g * R:(g + 1) * R]
            s_old = jnp.where(mk_old, jnp.dot(q, wb_ref[0, 0, g].astype(bf16), preferred_element_type=f32), NEG)
            s_new = jnp.where(mk_new, _nt_dot(q, nw_ref[0, :, ksl].astype(bf16)), NEG)
            m = jnp.maximum(jnp.max(s_old, axis=-1, keepdims=True), jnp.max(s_new, axis=-1, keepdims=True))
            e_old = jnp.where(mk_old, jnp.exp(s_old - m), 0.0)
            e_new = jnp.where(mk_new, jnp.exp(s_new - m), 0.0)
            den = jnp.sum(e_old, axis=-1, keepdims=True) + jnp.sum(e_new, axis=-1, keepdims=True)
            o_w = (_nt_dot(e_old.astype(bf16), wb_ref[0, 1, g].astype(bf16))
                   + jnp.dot(e_new.astype(bf16), nw_ref[0, :, vsl].astype(bf16), preferred_element_type=f32)
                   ) / jnp.maximum(den, 1e-30)
            gt = g_ref[0, g]
            o_ref[0, g] = gt[:, 0:1] * oc_ref[g] + gt[:, 1:2] * o_s + gt[:, 2:3] * o_w


def _nsa_sample(q_g, gates_g, summ, new_sel, win_t, new_win, pool_t, page_table, pps, past_len):
    DB, NP = page_table.shape
    ts = new_sel.shape[1]
    R = HPG * ts
    nbp = summ.shape[1]
    page = pool_t.shape[-1]
    keep = win_t.shape[-1]
    kt = pps * page
    nbs = kt // BLOCK
    expand = jnp.asarray(np.arange(kt)[None, :] // BLOCK == np.arange(nbs)[:, None], bf16)

    def page_spec(p):
        return pl.BlockSpec((1, 2, N_KV, HEAD_DIM, page), lambda b, s, pt: (pt[b, s * pps + p], 0, 0, 0, 0))

    per_b4 = lambda b, s, pt: (b, 0, 0, 0)
    per_b3 = lambda b, s, pt: (b, 0, 0)
    return pl.pallas_call(
        functools.partial(_nsa_sample_body, pps=pps, past_len=past_len, ts=ts),
        grid_spec=pltpu.PrefetchScalarGridSpec(
            num_scalar_prefetch=1,
            grid=(DB, NP // pps),
            in_specs=[page_spec(p) for p in range(pps)]
            + [pl.BlockSpec((1, N_KV, R, HEAD_DIM), per_b4),
               pl.BlockSpec((1, N_KV, R, 3), per_b4),
               pl.BlockSpec((1, nbp, KV_COLS), per_b3),
               pl.BlockSpec((1, ts, KV_COLS), per_b3),
               pl.BlockSpec((1, 2, N_KV, HEAD_DIM, keep), lambda b, s, pt: (b, 0, 0, 0, 0)),
               pl.BlockSpec((1, ts, KV_COLS), per_b3),
               pl.BlockSpec((nbs, kt), lambda b, s, pt: (0, 0))],
            out_specs=pl.BlockSpec((1, N_KV, R, HEAD_DIM), per_b4),
            scratch_shapes=[pltpu.VMEM((NP // pps, N_KV, R, nbs), bf16),
                            pltpu.VMEM((N_KV * R, 1), f32),
                            pltpu.VMEM((N_KV * R, 1), f32),
                            pltpu.VMEM((N_KV * R, HEAD_DIM), f32),
                            pltpu.VMEM((N_KV, R, HEAD_DIM), f32)]),
        out_shape=jax.ShapeDtypeStruct((DB, N_KV, R, HEAD_DIM), f32),
        compiler_params=_cparams(("parallel", "arbitrary")),
        name="nsa_sample",
    )(page_table, *([pool_t] * pps), q_g, gates_g, summ, new_sel, win_t, new_win, expand)


def _bmm(spec, a, b):
    return jnp.einsum(spec, a.astype(bf16), b.astype(bf16), preferred_element_type=f32)


def _unit_lower_solve(L, rhs, C, bs):
    _mm = functools.partial(_bmm, "hij,hjk->hik")
    ri = lax.broadcasted_iota(jnp.int32, (1, C, C), 1)
    ci = lax.broadcasted_iota(jnp.int32, (1, C, C), 2)
    same = (ri // bs) == (ci // bs)
    eye = jnp.where(ri == ci, 1.0, 0.0)
    D = jnp.where(same, L, 0.0)
    T = eye - D
    P = D
    n = 2
    while n < bs:
        P = _mm(P, P)
        T = T + _mm(T, P)
        n *= 2
    x = _mm(T, rhs)
    nblk = C // bs
    if nblk == 1:
        return x
    Mb = _mm(T, jnp.where(same, 0.0, L))
    factors = []
    Pm = Mb
    n = 2
    while n < nblk:
        Pm = _mm(Pm, Pm)
        factors.append(Pm)
        n *= 2
    for Pm in reversed(factors):
        x = x + _mm(Pm, x)
    return x - _mm(Mb, x)


def _rwkv_body(ps_ref, sp_ref, s0_ref, mu_ref, w0_ref, ww_ref, a0_ref, wa_ref, kk_ref, ka_ref, rk_ref,
               lnw_ref, lnb_ref, seg_ref, segt_ref, y_ref, so_ref, carry_ref, state_ref, *, C, bs):
    c = pl.program_id(1)
    nbat = ps_ref.shape[0]
    H = RW_HEADS

    @pl.when(c == 0)
    def _():
        carry_ref[...] = sp_ref[...]
        state_ref[...] = s0_ref[...].reshape(state_ref.shape)

    def heads(x):
        return [x[:, h * RW_HEAD:(h + 1) * RW_HEAD] for h in range(H)]

    ri = lax.broadcasted_iota(jnp.int32, (C, C), 0)
    ci = lax.broadcasted_iota(jnp.int32, (C, C), 1)
    tril = jnp.where(ci <= ri, 1.0, 0.0)
    rowi = lax.broadcasted_iota(jnp.int32, (C, 1), 0)

    x1_l, x2_l, kb_l, v_l, rk_l, etot_l = [], [], [], [], [], []
    for n in range(nbat):
        ps = ps_ref[n]
        prev = jnp.where(rowi == 0, carry_ref[n], pltpu.roll(ps, 1, axis=0))
        carry_ref[n] = ps[C - 1:C, :]
        z = ps + (prev - ps) * mu_ref[...]
        r = z[:, 0:RW_DIM]
        k = z[:, RW_DIM:2 * RW_DIM]
        v = z[:, 2 * RW_DIM:3 * RW_DIM]
        xw = z[:, 3 * RW_DIM:3 * RW_DIM + LORA_W]
        xa = z[:, 3 * RW_DIM + LORA_W:]
        u = -(w0_ref[...] + jnp.dot(jnp.tanh(xw).astype(bf16), ww_ref[...], preferred_element_type=f32))
        softplus = jnp.maximum(u, 0.0) + jnp.log(1.0 + jnp.exp(-jnp.abs(u)))
        lw = -jnp.exp(-softplus - 0.5)
        a = jax.nn.sigmoid(a0_ref[...] + jnp.dot(xa.astype(bf16), wa_ref[...], preferred_element_type=f32))
        kk = k * kk_ref[...]
        ss = jnp.dot(kk * kk, seg_ref[...], precision=HIGHEST, preferred_element_type=f32)
        kk = kk * jnp.dot(lax.rsqrt(jnp.maximum(ss, 1e-24)), segt_ref[...], precision=HIGHEST,
                          preferred_element_type=f32)
        bb = kk * a
        k2 = k * (1.0 + (a - 1.0) * ka_ref[...])
        G = jnp.dot(tril, lw, precision=HIGHEST, preferred_element_type=f32)
        g_end = G[C - 1:C, :]
        e_neg = jnp.exp(-G)
        e_end = jnp.exp(g_end - G)
        x1_l.append(heads(jnp.concatenate([kk * jnp.exp(G - lw), r * jnp.exp(G)], axis=0).astype(bf16)))
        x2_l.append(heads(jnp.concatenate([k2 * e_neg, bb * e_neg], axis=0).astype(bf16)))
        kb_l.append(heads(jnp.concatenate([k2 * e_end, -(bb * e_end)], axis=0).astype(bf16)))
        v_l.append(heads(v))
        rk_l.append(heads(r * k2 * rk_ref[...]))
        etot_l.append(heads(jnp.exp(g_end)))

    stack = lambda lst: jnp.stack([t for per_b in lst for t in per_b], axis=0)
    X1, X2, KB = stack(x1_l), stack(x2_l), stack(kb_l)
    V, RK, ETOT = stack(v_l), stack(rk_l), stack(etot_l)

    strict = (ci < ri)[None]
    incl = (ci <= ri)[None]
    S = state_ref[...]
    A = _bmm("hck,hdk->hcd", X1, X2)
    P = _bmm("hck,hvk->hcv", X1, S)
    a_kk = jnp.where(strict, A[:, :C, :C], 0.0)
    a_kb = jnp.where(strict, A[:, :C, C:], 0.0)
    rhs = P[:, :C] + _bmm("hcd,hdv->hcv", a_kk, V)
    sa = _unit_lower_solve(a_kb, rhs, C, bs)
    a_r = jnp.concatenate([jnp.where(incl, A[:, C:, :C], 0.0), jnp.where(incl, -A[:, C:, C:], 0.0)], axis=2)
    vs = jnp.concatenate([V, sa], axis=1)
    y = P[:, C:] + _bmm("hcd,hdv->hcv", a_r, vs)
    state_ref[...] = S * ETOT + _bmm("hcv,hck->hvk", vs, KB)
    mean = jnp.mean(y, axis=-1, keepdims=True)
    var = jnp.mean(jnp.square(y - mean), axis=-1, keepdims=True)
    yn = (y - mean) * lax.rsqrt(var + LN_X_EPS)
    bonus = jnp.sum(RK, axis=-1, keepdims=True) * V
    for n in range(nbat):
        for h in range(H):
            sl = slice(h * RW_HEAD, (h + 1) * RW_HEAD)
            i = n * H + h
            y_ref[n, :, sl] = (yn[i] * lnw_ref[:, sl] + lnb_ref[:, sl] + bonus[i]).astype(y_ref.dtype)

    so_ref[...] = state_ref[...].reshape(so_ref.shape)


def _rwkv(p_shift, shift_prev, s0, rw):
    mu, w0, w_lora_w, a0, w_lora_a, k_k, k_a, r_k, ln_w, ln_b = rw
    B, T, _ = p_shift.shape
    C = _pick_tile(T, 64, 8)
    bs = min(16, C)
    nbat = _pick_tile(B, RWKV_BATCH_PER_STEP, 1)
    seg_np = (np.arange(RW_DIM)[:, None] // RW_HEAD == np.arange(RW_HEADS)[None, :]).astype(np.float32)
    vec = lambda n: pl.BlockSpec((1, n), lambda b, c: (0, 0))
    row = lambda t: t.reshape(1, -1)
    y, s_new = pl.pallas_call(
        functools.partial(_rwkv_body, C=C, bs=bs),
        grid=(B // nbat, T // C),
        in_specs=[pl.BlockSpec((nbat, C, SHIFT_COLS), lambda b, c: (b, c, 0)),
                  pl.BlockSpec((nbat, 1, SHIFT_COLS), lambda b, c: (b, 0, 0)),
                  pl.BlockSpec((nbat, RW_HEADS, RW_HEAD, RW_HEAD), lambda b, c: (b, 0, 0, 0)),
                  vec(SHIFT_COLS), vec(RW_DIM),
                  pl.BlockSpec((LORA_W, RW_DIM), lambda b, c: (0, 0)),
                  vec(RW_DIM),
                  pl.BlockSpec((LORA_A, RW_DIM), lambda b, c: (0, 0)),
                  vec(RW_DIM), vec(RW_DIM), vec(RW_DIM), vec(RW_DIM), vec(RW_DIM),
                  pl.BlockSpec((RW_DIM, RW_HEADS), lambda b, c: (0, 0)),
                  pl.BlockSpec((RW_HEADS, RW_DIM), lambda b, c: (0, 0))],
        out_specs=[pl.BlockSpec((nbat, C, RW_DIM), lambda b, c: (b, c, 0)),
                   pl.BlockSpec((nbat, RW_HEADS, RW_HEAD, RW_HEAD), lambda b, c: (b, 0, 0, 0))],
        out_shape=[jax.ShapeDtypeStruct((B, T, RW_DIM), bf16),
                   jax.ShapeDtypeStruct((B, RW_HEADS, RW_HEAD, RW_HEAD), f32)],
        scratch_shapes=[pltpu.VMEM((nbat, 1, SHIFT_COLS), f32),
                        pltpu.VMEM((nbat * RW_HEADS, RW_HEAD, RW_HEAD), f32)],
        compiler_params=_cparams(("parallel", "arbitrary")),
        name="rwkv",
    )(p_shift, shift_prev.reshape(B, 1, SHIFT_COLS), s0, row(mu), row(w0), w_lora_w.astype(bf16), row(a0),
      w_lora_a.astype(bf16), row(k_k), row(k_a), row(r_k), row(ln_w), row(ln_b),
      jnp.asarray(seg_np), jnp.asarray(seg_np.T))
    return y, s_new


def _merge_body(x_ref, oa_ref, yr_ref, ga_ref, gb_ref, wa_ref, wb_ref, wo_ref, h_ref):
    ma = jnp.dot(oa_ref[...], wa_ref[...], preferred_element_type=f32)
    mb = jnp.dot(yr_ref[...], wb_ref[...], preferred_element_type=f32)
    m = jax.nn.sigmoid(ga_ref[...]) * ma + jax.nn.sigmoid(gb_ref[...]) * mb
    h_ref[...] = x_ref[...] + jnp.dot(m.astype(bf16), wo_ref[...], preferred_element_type=f32)


def _merge(x, o_att, y_rw, p_merge, w_a, w_b, w_o):
    M, D = x.shape
    tm = _pick_tile(M, 256, 8)
    row = lambda i: (i, 0)
    const = lambda i: (0, 0)
    return pl.pallas_call(
        _merge_body,
        grid=(M // tm,),
        in_specs=[pl.BlockSpec((tm, D), row),
                  pl.BlockSpec((tm, ATT_DIM), row),
                  pl.BlockSpec((tm, RW_DIM), row),
                  pl.BlockSpec((tm, D), lambda i: (i, 0)),
                  pl.BlockSpec((tm, D), lambda i: (i, 1)),
                  pl.BlockSpec((ATT_DIM, D), const),
                  pl.BlockSpec((RW_DIM, D), const),
                  pl.BlockSpec((D, D), const)],
        out_specs=pl.BlockSpec((tm, D), row),
        out_shape=jax.ShapeDtypeStruct((M, D), f32),
        compiler_params=_cparams(("parallel",)),
        name="merge",
    )(x, o_att, y_rw, p_merge, p_merge, w_a, w_b, w_o)


def _conv_ffn_body(h_ref, g_ref, wug_ref, wuv_ref, cwg_ref, cwv_ref, cbg_ref, cbv_ref, wd_ref,
                   pg_ref, pv_ref, y_ref, tg_ref, tv_ref, hn_ref, cg_ref, cv_ref, *, tm, seq_rows, tail, nsub):
    i = pl.program_id(1)
    j = pl.program_id(2)
    carried = tm <= seq_rows

    @pl.when(j == 0)
    def _():
        h = h_ref[0]
        ms = jnp.mean(h * h, axis=-1, keepdims=True)
        hn_ref[...] = (h * lax.rsqrt(ms + NORM_EPS) * g_ref[...]).astype(bf16)
        y_ref[0] = h

    def taps(cw_ref, cb_ref, u2, u1, u):
        return cb_ref[...] + cw_ref[0:1, :] * u2 + cw_ref[1:2, :] * u1 + cw_ref[2:3, :] * u

    def finish(rows, gate, val):
        act = (gate * jax.nn.sigmoid(gate) * val).astype(bf16)
        y_ref[0, rows, :] += jnp.dot(act, wd_ref[...], preferred_element_type=f32)

    if not carried:
        t_in = lax.broadcasted_iota(jnp.int32, (tm, 1), 0) % seq_rows
        hn = hn_ref[...]

        def conv(u, cw_ref, cb_ref, prev_ref):
            pr = prev_ref[0]
            u1 = jnp.where(t_in == 0, pltpu.roll(pr, tm - 1, axis=0), pltpu.roll(u, 1, axis=0))
            u2 = jnp.where(t_in == 0, pr, jnp.where(t_in == 1, pr, pltpu.roll(u, 2, axis=0)))
            return taps(cw_ref, cb_ref, u2, u1, u)

        ug = jnp.dot(hn, wug_ref[...], preferred_element_type=f32)
        uv = jnp.dot(hn, wuv_ref[...], preferred_element_type=f32)
        tg_ref[0, 0] = ug
        tv_ref[0, 0] = uv
        finish(slice(None), conv(ug, cwg_ref, cbg_ref, pg_ref), conv(uv, cwv_ref, cbv_ref, pv_ref))
        return

    @pl.when(i == 0)
    def _():
        cg_ref[j] = pg_ref[0]
        cv_ref[j] = pv_ref[0]

    ts = tm // nsub
    rowi = lax.broadcasted_iota(jnp.int32, (ts, 1), 0)
    prev_g = (cg_ref[j, 0:1, :], cg_ref[j, 1:2, :])
    prev_v = (cv_ref[j, 0:1, :], cv_ref[j, 1:2, :])

    def conv(u, cw_ref, cb_ref, prev):
        p2, p1 = prev
        u1 = jnp.where(rowi == 0, p1, pltpu.roll(u, 1, axis=0))
        u2 = jnp.where(rowi == 0, p2, jnp.where(rowi == 1, p1, pltpu.roll(u, 2, axis=0)))
        return taps(cw_ref, cb_ref, u2, u1, u), (u[ts - 2:ts - 1, :], u[ts - 1:ts, :])

    for sb in range(nsub):
        rows = slice(sb * ts, (sb + 1) * ts)
        hn = hn_ref[rows, :]
        ug = jnp.dot(hn, wug_ref[...], preferred_element_type=f32)
        uv = jnp.dot(hn, wuv_ref[...], preferred_element_type=f32)
        gate, prev_g = conv(ug, cwg_ref, cbg_ref, prev_g)
        val, prev_v = conv(uv, cwv_ref, cbv_ref, prev_v)
        finish(rows, gate, val)
    cg_ref[j] = jnp.concatenate(prev_g, axis=0)
    cv_ref[j] = jnp.concatenate(prev_v, axis=0)
    tg_ref[0, 0] = ug[ts - tail:ts, :]
    tv_ref[0, 0] = uv[ts - tail:ts, :]


def _conv_ffn(h, conv_prev, norm_g, w_up, conv_w, conv_b, w_down, *, fold):
    B, T, D = h.shape
    dff = w_down.shape[0]
    tf = _pick_tile(dff, 512, LANE)
    nf = dff // tf
    if not fold:
        nb_, tm = B, _pick_tile(T, 1024, 8)
        tail = 8
        hh = h
        prev = conv_prev
        prev_spec_g = pl.BlockSpec((1, CONV_W - 1, tf), lambda b, i, j: (b, 0, j))
        prev_spec_v = pl.BlockSpec((1, CONV_W - 1, tf), lambda b, i, j: (b, 0, nf + j))
    else:
        nb_, tm = 1, B * T
        tail = tm
        hh = h.reshape(1, B * T, D)
        assert T >= CONV_W - 1
        prev = jnp.concatenate([conv_prev, jnp.zeros((B, T - (CONV_W - 1), 2 * dff), f32)],
                               axis=1).reshape(1, B * T, 2 * dff)
        prev_spec_g = pl.BlockSpec((1, tm, tf), lambda b, i, j: (0, 0, j))
        prev_spec_v = pl.BlockSpec((1, tm, tf), lambda b, i, j: (0, 0, nf + j))
    nt = hh.shape[1] // tm
    nsub = FFN_ROW_SUBBLOCKS if (not fold and tm % (8 * FFN_ROW_SUBBLOCKS) == 0) else 1
    body = functools.partial(_conv_ffn_body, tm=tm, seq_rows=T, tail=tail, nsub=nsub)
    cw = conv_w
    cb = conv_b.reshape(1, 2 * dff)
    tail_spec = pl.BlockSpec((1, 1, tail, tf), lambda b, i, j: (b, i, 0, j))
    tail_shape = jax.ShapeDtypeStruct((nb_, nt, tail, dff), f32)
    y, ug, uv = pl.pallas_call(
        body,
        grid=(nb_, nt, nf),
        in_specs=[pl.BlockSpec((1, tm, D), lambda b, i, j: (b, i, 0)),
                  pl.BlockSpec((1, D), lambda b, i, j: (0, 0)),
                  pl.BlockSpec((D, tf), lambda b, i, j: (0, j)),
                  pl.BlockSpec((D, tf), lambda b, i, j: (0, nf + j)),
                  pl.BlockSpec((CONV_W, tf), lambda b, i, j: (0, j)),
                  pl.BlockSpec((CONV_W, tf), lambda b, i, j: (0, nf + j)),
                  pl.BlockSpec((1, tf), lambda b, i, j: (0, j)),
                  pl.BlockSpec((1, tf), lambda b, i, j: (0, nf + j)),
                  pl.BlockSpec((tf, D), lambda b, i, j: (j, 0)),
                  prev_spec_g, prev_spec_v],
        out_specs=[pl.BlockSpec((1, tm, D), lambda b, i, j: (b, i, 0)), tail_spec, tail_spec],
        out_shape=[jax.ShapeDtypeStruct(hh.shape, f32), tail_shape, tail_shape],
        scratch_shapes=[pltpu.VMEM((tm, D), bf16),
                        pltpu.VMEM((nf, CONV_W - 1, tf), f32),
                        pltpu.VMEM((nf, CONV_W - 1, tf), f32)],
        compiler_params=_cparams(("parallel", "arbitrary", "arbitrary")),
        name="conv_ffn",
    )(hh, norm_g.reshape(1, D), w_up, w_up, cw, cw, cb, cb, w_down, prev, prev)
    return y, ug[:, -1], uv[:, -1]


def _split_w_in(w_in):
    o = 0
    parts = []
    for n in (ATT_DIM, KV_COLS, KV_COLS, KV_COLS, 3 * N_HEADS, SHIFT_COLS, 2 * w_in.shape[0]):
        parts.append(w_in[:, o:o + n])
        o += n
    wq, wc, ws, ww, wg, wsh, wm = parts
    wg = jnp.pad(wg, ((0, 0), (0, LANE - 3 * N_HEADS)))
    w_att = jnp.concatenate([wq, wc, ws, ww, wg], axis=1).astype(bf16)
    return w_att, wsh.astype(bf16), wm.astype(bf16)


def _head_major(x, n):
    B, T, _ = x.shape
    return x.reshape(B, T, n, HEAD_DIM).transpose(0, 2, 1, 3)


def _mixer_inputs(x2d, pos_tab, norm_g, w_parts, consts, wc, with_summ):
    w_att, w_sh, w_mg = w_parts
    p_att = _norm_matmul(x2d, norm_g, w_att)
    p_shift = _norm_matmul(x2d, norm_g, w_sh)
    p_merge = _norm_matmul(x2d, norm_g, w_mg)
    post = _qk_post(p_att, pos_tab, consts, wc, with_summ)
    return post, p_shift, p_merge


def kernel(x_prompt, x_sample, cache_kv_cmp, cache_kv_sel, page_table, cache_kv_win, state_wkv, state_shift, state_conv, norm1_g, w_in, q_gain, k_gains, w_cmp, mu_shift, w0, w_lora_w, a0, w_lora_a, k_k, k_a, r_k, ln_x_w, ln_x_b, w_branch_a, w_branch_b, w_out, norm2_g, w_up, conv_w, conv_b, w_down):
    B, T, D = x_prompt.shape
    DB, TS, _ = x_sample.shape
    depth = w_in.shape[0]
    assert depth == 1, "single-layer trunk"
    l = 0
    page = cache_kv_cmp.shape[2]
    n_pages = page_table.shape[1]
    past_len = n_pages * page
    assert past_len % BLOCK == 0 and TS <= BLOCK and page % BLOCK == 0 and T % BLOCK == 0
    dff = w_down.shape[1]

    w_parts = _split_w_in(w_in[l])
    consts = _qk_consts(q_gain[l], k_gains[l])
    wc = _compress_weights(w_cmp[l])
    rw = (mu_shift[l], w0[l], w_lora_w[l], a0[l], w_lora_a[l], k_k[l], k_a[l], r_k[l], ln_x_w[l], ln_x_b[l])
    w_a = w_branch_a[l].astype(bf16)
    w_b = w_branch_b[l].astype(bf16)
    w_o = w_out[l].astype(bf16)
    w_u = w_up[l].astype(bf16)
    w_d = w_down[l].astype(bf16)

    xp = x_prompt.reshape(B * T, D)
    tabs_p = _rope_tables(jnp.arange(T, dtype=jnp.int32))
    ((q_pad, kvc, kvs, kvw, gates, summ, ks_aug, vs_pad, kw_pad, vw_pad),
     p_shift, p_merge) = _mixer_inputs(xp, tabs_p, norm1_g[l], w_parts, consts, wc, True)
    nb = T // BLOCK
    summ = summ.reshape(B, nb, KV_COLS)
    lane_pad = lambda x: jnp.pad(x, ((0, 0), (0, 0), (0, 0), (0, LANE - HEAD_DIM))).astype(bf16)
    kvw3 = kvw.reshape(B, T, KV_COLS)
    o_att = _nsa_prompt(q_pad,
                        lane_pad(_head_major(summ[:, :, :K_COLS], N_KV)),
                        lane_pad(_head_major(summ[:, :, K_COLS:], N_KV)),
                        ks_aug, vs_pad, kw_pad, vw_pad, gates.reshape(B, T, LANE))
    p_shift3 = p_shift.reshape(B, T, SHIFT_COLS)
    y_rw, wkv_p = _rwkv(p_shift3, jnp.zeros((B, SHIFT_COLS), f32),
                        jnp.zeros((B, RW_HEADS, RW_HEAD, RW_HEAD), f32), rw)
    h_p = _merge(xp, o_att.reshape(B * T, ATT_DIM), y_rw.reshape(B * T, RW_DIM), p_merge, w_a, w_b, w_o)
    y_p, ug, uv = _conv_ffn(h_p.reshape(B, T, D), jnp.zeros((B, CONV_W - 1, 2 * dff), f32), norm2_g[l],
                            w_u, conv_w[l], conv_b[l], w_d, fold=False)
    assert T >= CONV_W - 1
    conv_p = jnp.concatenate([ug[:, -(CONV_W - 1):], uv[:, -(CONV_W - 1):]], axis=-1)
    kv_shape_p = (1, B, T, 2, N_KV, HEAD_DIM)
    keep_p = min(WINDOW, T)
    outs_p = (y_p,
              kvc.reshape(kv_shape_p), kvs.reshape(kv_shape_p),
              kvw3[:, T - keep_p:].reshape(1, B, keep_p, 2, N_KV, HEAD_DIM),
              wkv_p[None], p_shift3[:, -1][None], conv_p[None])

    xs = x_sample.reshape(DB * TS, D)
    pos_s = past_len + jnp.arange(TS, dtype=jnp.int32)
    tabs_s = tuple(jnp.tile(t, (DB, 1)) for t in _rope_tables(pos_s))
    (q, kvc_s, kvs_s, kvw_s, gates), p_shift, p_merge = _mixer_inputs(xs, tabs_s, norm1_g[l], w_parts, consts, wc, False)
    pps = _pick_tile(n_pages, PAGES_PER_STEP, 1)
    summ_s = _compress_pool(_pages_row_minor(cache_kv_cmp[l]), page_table, w_cmp[l], pps)
    summ_s = summ_s.reshape(DB, past_len // BLOCK, KV_COLS)
    R = HPG * TS
    q_g = _head_major(q.reshape(DB, TS, ATT_DIM), N_HEADS).reshape(DB, N_KV, R, HEAD_DIM)
    gates_g = (gates[:, :3 * N_HEADS].reshape(DB, TS, N_HEADS, 3).transpose(0, 2, 1, 3)
               .reshape(DB, N_KV, R, 3))
    keep = cache_kv_win.shape[2]
    kvw_s3 = kvw_s.reshape(DB, TS, KV_COLS)
    o_g = _nsa_sample(q_g, gates_g, summ_s, kvs_s.reshape(DB, TS, KV_COLS), _pages_row_minor(cache_kv_win[l]),
                      kvw_s3, _pages_row_minor(cache_kv_sel[l]), page_table, pps, past_len)
    o_att_s = (o_g.reshape(DB, N_HEADS, TS, HEAD_DIM).transpose(0, 2, 1, 3)
               .reshape(DB * TS, ATT_DIM).astype(bf16))
    p_shift3s = p_shift.reshape(DB, TS, SHIFT_COLS)
    y_rw_s, wkv_s = _rwkv(p_shift3s, state_shift[l], state_wkv[l], rw)
    h_s = _merge(xs, o_att_s, y_rw_s.reshape(DB * TS, RW_DIM), p_merge, w_a, w_b, w_o)
    y_s, ug, uv = _conv_ffn(h_s.reshape(DB, TS, D), state_conv[l], norm2_g[l],
                            w_u, conv_w[l], conv_b[l], w_d, fold=True)
    up_s = jnp.concatenate([state_conv[l],
                            jnp.concatenate([ug.reshape(DB, TS, dff), uv.reshape(DB, TS, dff)], axis=-1)], axis=1)
    conv_s = up_s[:, TS:]
    win_s = jnp.concatenate([cache_kv_win[l], kvw_s.reshape(DB, TS, 2, N_KV, HEAD_DIM)], axis=1)[:, TS:]
    kv_shape_s = (1, DB, TS, 2, N_KV, HEAD_DIM)

    return (outs_p[0], y_s.reshape(DB, TS, D),
            outs_p[1], kvc_s.reshape(kv_shape_s),
            outs_p[2], kvs_s.reshape(kv_shape_s),
            outs_p[3], win_s.reshape(1, DB, keep, 2, N_KV, HEAD_DIM),
            outs_p[4], wkv_s[None],
            outs_p[5], p_shift3s[:, -1][None],
            outs_p[6], conv_s[None])
```

```python
import functools

import numpy as np
import jax
import jax.numpy as jnp
from jax import lax
from jax.experimental import pallas as pl
from jax.experimental.pallas import tpu as pltpu

f32 = jnp.float32
bf16 = jnp.bfloat16

N_HEADS = 16
N_KV = 4
HPG = N_HEADS // N_KV
HEAD_DIM = 64
ROPE_DIM = HEAD_DIM // 4
ROPE_THETA = 500000.0
BLOCK = 64
N_SEL = 16
WINDOW = 512
RW_HEADS = 16
RW_HEAD = 64
RW_DIM = RW_HEADS * RW_HEAD
LORA_W = 64
LORA_A = 64
LN_X_EPS = 64e-5
CONV_W = 3
NORM_EPS = 1e-6
ATT_DIM = N_HEADS * HEAD_DIM
KV_COLS = 2 * N_KV * HEAD_DIM
K_COLS = N_KV * HEAD_DIM
SHIFT_COLS = 3 * RW_DIM + LORA_W + LORA_A
NEG = -1e30
LOG2E = 1.4426950408889634

LANE = 128
VMEM_LIMIT = 56 * 1024 * 1024
NORM_MATMUL_VMEM_BUDGET = 50 * 1024 * 1024
QKV_COLS = ATT_DIM + 3 * KV_COLS
ATT_PROJ_COLS = QKV_COLS + LANE
N_NORM_HEADS = QKV_COLS // HEAD_DIM
RWKV_BATCH_PER_STEP = 2
FFN_ROW_SUBBLOCKS = 2
NSA_GROUPS_PER_STEP = 1
PAGES_PER_STEP = 16


def _cparams(sem):
    return pltpu.CompilerParams(dimension_semantics=sem, vmem_limit_bytes=VMEM_LIMIT)


def _pick_tile(n, cap, mult):
    best = None
    for t in range(mult, min(n, cap) + 1, mult):
        if n % t == 0:
            best = t
    assert best is not None, (n, cap, mult)
    return best


def _norm_matmul_body(x_ref, g_ref, w_ref, o_ref, xn_ref):
    @pl.when(pl.program_id(1) == 0)
    def _():
        x = x_ref[...]
        ms = jnp.mean(x * x, axis=-1, keepdims=True)
        xn_ref[...] = (x * lax.rsqrt(ms + NORM_EPS) * g_ref[...]).astype(bf16)

    o_ref[...] = jnp.dot(xn_ref[...], w_ref[...], preferred_element_type=f32)


def _norm_matmul(x, gain, w):
    M, D = x.shape
    N = w.shape[1]
    tm = _pick_tile(M, 512, 8)
    fits = lambda tn: (2 * tm * D * 4 + tm * D * 2 + 2 * D * tn * 2 + 2 * tm * tn * 4) <= NORM_MATMUL_VMEM_BUDGET
    tn = max(t for t in range(LANE, N + 1, LANE) if N % t == 0 and fits(t))
    return pl.pallas_call(
        _norm_matmul_body,
        grid=(M // tm, N // tn),
        in_specs=[pl.BlockSpec((tm, D), lambda i, j: (i, 0)),
                  pl.BlockSpec((1, D), lambda i, j: (0, 0)),
                  pl.BlockSpec((D, tn), lambda i, j: (0, j))],
        out_specs=pl.BlockSpec((tm, tn), lambda i, j: (i, j)),
        out_shape=jax.ShapeDtypeStruct((M, N), f32),
        scratch_shapes=[pltpu.VMEM((tm, D), bf16)],
        compiler_params=_cparams(("parallel", "arbitrary")),
        name="norm_matmul",
    )(x, gain.reshape(1, D), w)


def _qk_post_body(p_ref, cos_ref, sa_ref, sb_ref, gvec_ref, isk_ref, seg_ref, segt_ref, wc_ref,
                  q_ref, kvc_ref, kvs_ref, kvw_ref, gate_ref, *prompt_refs, nt):
    y = p_ref[:, :QKV_COLS]
    isk = isk_ref[...] > 0.5
    sq_hi, sq_lo = _split_bf16(y * y)
    ss = (jnp.dot(sq_hi, seg_ref[...], preferred_element_type=f32)
          + jnp.dot(sq_lo, seg_ref[...], preferred_element_type=f32)) * (1.0 / HEAD_DIM)
    rs_hi, rs_lo = _split_bf16(lax.rsqrt(ss + NORM_EPS))
    rb = (jnp.dot(rs_hi, segt_ref[...], preferred_element_type=f32)
          + jnp.dot(rs_lo, segt_ref[...], preferred_element_type=f32))
    yn = jnp.where(isk, y * rb * gvec_ref[...], y)
    reps = QKV_COLS // LANE
    cos = jnp.where(isk, jnp.concatenate([cos_ref[...]] * reps, axis=1), 1.0)
    sa = jnp.where(isk, jnp.concatenate([sa_ref[...]] * reps, axis=1), 0.0)
    sb = jnp.where(isk, jnp.concatenate([sb_ref[...]] * reps, axis=1), 0.0)
    half = ROPE_DIM // 2
    out = (yn * cos + pltpu.roll(yn, QKV_COLS - half, axis=1) * sa + pltpu.roll(yn, half, axis=1) * sb)
    qs = out[:, :ATT_DIM] * (HEAD_DIM ** -0.5 * (LOG2E if prompt_refs else 1.0))
    kvc = out[:, ATT_DIM:ATT_DIM + KV_COLS]
    kvs = out[:, ATT_DIM + KV_COLS:ATT_DIM + 2 * KV_COLS]
    kvw = out[:, ATT_DIM + 2 * KV_COLS:]
    kvc_ref[...] = kvc
    kvs_ref[...] = kvs
    kvw_ref[...] = kvw
    gate_ref[...] = jax.nn.sigmoid(p_ref[:, QKV_COLS:])
    if not prompt_refs:
        q_ref[...] = qs.astype(bf16)
        return
    summ_ref, ksa_ref, vsp_ref, kwp_ref, vwp_ref = prompt_refs
    tm = kvc.shape[0]
    blk = kvc.reshape(tm // BLOCK, BLOCK, KV_COLS) * wc_ref[...][None]
    summ_ref[0] = jnp.sum(blk, axis=1)
    zeros = jnp.zeros((tm, LANE - HEAD_DIM), f32)
    t0 = (pl.program_id(0) % nt) * tm
    blk_of_row = (t0 + lax.broadcasted_iota(jnp.int32, zeros.shape, 0)) // BLOCK
    onehot = jnp.where(lax.broadcasted_iota(jnp.int32, zeros.shape, 1) == blk_of_row, 1.0, 0.0)
    hd = lambda x, h: x[:, h * HEAD_DIM:(h + 1) * HEAD_DIM]
    pad = lambda x, tail: jnp.concatenate([x, tail], axis=1).astype(bf16)
    for h in range(N_HEADS):
        q_ref[0, h] = pad(hd(qs, h), zeros)
    for g in range(N_KV):
        ksa_ref[0, g] = pad(hd(kvs, g), onehot)
        vsp_ref[0, g] = pad(hd(kvs, N_KV + g), zeros)
        kwp_ref[0, g] = pad(hd(kvw, g), zeros)
        vwp_ref[0, g] = pad(hd(kvw, N_KV + g), zeros)


def _qk_post(p_att, tabs, consts, wc, prompt):
    M = p_att.shape[0]
    cos_t, sa_t, sb_t = tabs
    Tt = cos_t.shape[0]
    tm = _pick_tile(Tt, 256, BLOCK if prompt else 8)
    nt = Tt // tm
    gvec, isk, seg, segt = consts
    row = lambda i: (i, 0)
    tab = lambda i: (i % nt, 0)
    const = lambda i: (0, 0)
    kv_shape = jax.ShapeDtypeStruct((M, KV_COLS), f32)
    kv_spec = pl.BlockSpec((tm, KV_COLS), row)
    out_shape = [jax.ShapeDtypeStruct((M, ATT_DIM), bf16), kv_shape, kv_shape, kv_shape,
                 jax.ShapeDtypeStruct((M, LANE), f32)]
    out_specs = [pl.BlockSpec((tm, ATT_DIM), row), kv_spec, kv_spec, kv_spec, pl.BlockSpec((tm, LANE), row)]
    if prompt:
        assert Tt // BLOCK <= LANE - HEAD_DIM, "one-hot block lanes"
        B = M // Tt
        hm = lambda n: jax.ShapeDtypeStruct((B, n, Tt, LANE), bf16)
        hm_spec = lambda n: pl.BlockSpec((1, n, tm, LANE), lambda i: (i // nt, 0, i % nt, 0))
        out_shape[0], out_specs[0] = hm(N_HEADS), hm_spec(N_HEADS)
        out_shape += [jax.ShapeDtypeStruct((M // tm, tm // BLOCK, KV_COLS), f32)] + [hm(N_KV)] * 4
        out_specs += [pl.BlockSpec((1, tm // BLOCK, KV_COLS), lambda i: (i, 0, 0))] + [hm_spec(N_KV)] * 4
    return pl.pallas_call(
        functools.partial(_qk_post_body, nt=nt),
        grid=(M // tm,),
        in_specs=[pl.BlockSpec((tm, ATT_PROJ_COLS), row),
                  pl.BlockSpec((tm, LANE), tab), pl.BlockSpec((tm, LANE), tab), pl.BlockSpec((tm, LANE), tab),
                  pl.BlockSpec((1, QKV_COLS), const), pl.BlockSpec((1, QKV_COLS), const),
                  pl.BlockSpec((QKV_COLS, N_NORM_HEADS), const), pl.BlockSpec((N_NORM_HEADS, QKV_COLS), const),
                  pl.BlockSpec((BLOCK, KV_COLS), const)],
        out_specs=out_specs,
        out_shape=out_shape,
        compiler_params=_cparams(("parallel",)),
        name="qk_post",
    )(p_att, cos_t, sa_t, sb_t, gvec, isk, seg, segt, wc)


def _rope_tables(pos):
    half = ROPE_DIM // 2
    inv = ROPE_THETA ** (-jnp.arange(half, dtype=f32) * 2.0 / ROPE_DIM)
    ang = pos.astype(f32)[:, None] * inv[None, :]
    cos, sin = jnp.cos(ang), jnp.sin(ang)
    n = pos.shape[0]
    ones = jnp.ones((n, HEAD_DIM - ROPE_DIM), f32)
    zeros = jnp.zeros((n, HEAD_DIM - half), f32)
    cos_h = jnp.concatenate([cos, cos, ones], axis=1)
    sa_h = jnp.concatenate([-sin, zeros], axis=1)
    sb_h = jnp.concatenate([jnp.zeros((n, half), f32), sin, jnp.zeros((n, HEAD_DIM - ROPE_DIM), f32)], axis=1)
    rep = LANE // HEAD_DIM
    return tuple(jnp.concatenate([t] * rep, axis=1) for t in (cos_h, sa_h, sb_h))


def _qk_consts(q_gain, k_gains):
    ones_v = jnp.ones((K_COLS,), f32)
    gvec = jnp.concatenate([jnp.tile(q_gain, N_HEADS)]
                           + [t for s in range(3) for t in (jnp.tile(k_gains[s], N_KV), ones_v)])
    isk_np = np.concatenate([np.ones(ATT_DIM)] + [np.ones(K_COLS), np.zeros(K_COLS)] * 3).astype(np.float32)
    seg_np = (np.arange(QKV_COLS)[:, None] // HEAD_DIM == np.arange(N_NORM_HEADS)[None, :]).astype(np.float32)
    seg_np = seg_np * isk_np[:, None]
    return (gvec.reshape(1, QKV_COLS), jnp.asarray(isk_np).reshape(1, QKV_COLS),
            jnp.asarray(seg_np, bf16), jnp.asarray(seg_np.T, bf16))


def _compress_weights(w_cmp):
    return jnp.concatenate([jnp.tile(w_cmp[c], (1, N_KV)) for c in range(2)], axis=1)


def _select_blocks(imp, n_pick):
    nb = imp.shape[-1]
    lane = lax.broadcasted_iota(jnp.int32, imp.shape, imp.ndim - 1).astype(f32)
    sel = jnp.zeros(imp.shape, f32)
    for _ in range(min(n_pick, nb)):
        mx = jnp.max(imp, axis=-1, keepdims=True)
        idx = jnp.min(jnp.where(imp == mx, lane, float(nb)), axis=-1, keepdims=True)
        hit = (lane == idx) & (mx >= 0.0)
        sel = jnp.where(hit, 1.0, sel)
        imp = jnp.where(lane == idx, -2.0, imp)
    return sel


def _softmax_parts(s, mask):
    s = jnp.where(mask, s, NEG)
    m = jnp.max(s, axis=-1, keepdims=True)
    e = jnp.where(mask, jnp.exp(s - m), 0.0)
    return m, e


def _nt_dot(a, b):
    return lax.dot_general(a, b, (((1,), (1,)), ((), ())), preferred_element_type=f32)


def _nsa_prompt_body(q_ref, kc_ref, vc_ref, ks_ref, vs_ref, kw_ref, vw_ref, g_ref, gx_ref, place_ref,
                     sb_ref, wb_ref, o_ref, *, tq, T, kc_tile, slab):
    i = pl.program_id(2)
    q0 = i * tq
    R = HPG * tq
    nb = T // BLOCK
    gps = kc_ref.shape[1]
    tn_dims = (((0,), (0,)), ((), ()))
    tpos = q0 + lax.broadcasted_iota(jnp.int32, (1, R), 1) % tq
    blk = lax.broadcasted_iota(jnp.int32, (nb, R), 0)
    vis = (blk + 1) * BLOCK - 1 <= tpos
    blk_q = lax.broadcasted_iota(jnp.int32, (nb, tq), 0)
    cur_q = (q0 + lax.broadcasted_iota(jnp.int32, (1, tq), 1)) // BLOCK
    cand = blk_q < cur_q
    lane = lax.broadcasted_iota(jnp.int32, (1, LANE), 1)
    in_blk_lanes = jnp.where((lane >= HEAD_DIM) & (lane < HEAD_DIM + nb), 1.0, 0.0)

    def compressed_and_selection(gi):
        q = q_ref[0, gi * HPG:(gi + 1) * HPG].reshape(R, LANE)
        s_c = jnp.where(vis, _nt_dot(kc_ref[0, gi], q), NEG)
        e_c = jnp.where(vis, jnp.exp2(s_c - jnp.max(s_c, axis=0, keepdims=True)), 0.0)
        p_c = e_c / jnp.maximum(jnp.sum(e_c, axis=0, keepdims=True), 1e-30)
        o_c = lax.dot_general(p_c.astype(bf16), vc_ref[0, gi], tn_dims, preferred_element_type=f32)
        imp = p_c[:, 0:tq]
        for hh in range(1, HPG):
            imp = imp + p_c[:, hh * tq:(hh + 1) * tq]
        imp = jnp.where(cand, imp, -1.0)
        ahead = jnp.zeros((nb, tq), f32)
        for m in range(nb):
            row_m = imp[m:m + 1, :]
            tie = jnp.where(blk_q > m, 1.0, 0.0)
            ahead = ahead + jnp.where(row_m > imp, 1.0, jnp.where(row_m == imp, tie, 0.0))
        sel = jnp.where(cand, jnp.where(ahead < N_SEL - 1, 1.0, 0.0), jnp.where(blk_q == cur_q, 1.0, 0.0))
        sel_l = lax.dot_general(sel.astype(bf16), place_ref[...], tn_dims, preferred_element_type=f32)
        q_off = ((in_blk_lanes - sel_l) * NEG).astype(bf16)
        return q, q + jnp.concatenate([q_off] * HPG, axis=0), o_c

    def attend(carry, qq, k, v, bias):
        m, l, acc = carry
        s = _nt_dot(qq, k)
        if bias is not None:
            s = (s.reshape(HPG, tq, s.shape[-1]) + bias[None]).reshape(s.shape)
        m_new = jnp.maximum(m, jnp.max(s, axis=-1, keepdims=True))
        alpha = jnp.exp2(m - m_new)
        e = jnp.exp2(s - m_new)
        l = alpha * l + jnp.sum(e, axis=-1, keepdims=True)
        acc = alpha * acc + jnp.dot(e.astype(bf16), v, preferred_element_type=f32)
        return m_new, l, acc

    init = (jnp.full((R, 1), NEG, f32), jnp.zeros((R, 1), f32), jnp.zeros((R, LANE), f32))

    groups = [compressed_and_selection(gi) for gi in range(gps)]

    def sel_step(c, carries):
        k0 = pl.multiple_of(c * kc_tile, kc_tile)
        return tuple(attend(carries[gi], groups[gi][1], ks_ref[0, gi, pl.ds(k0, kc_tile), :],
                            vs_ref[0, gi, pl.ds(k0, kc_tile), :], None) for gi in range(gps))

    n_full = q0 // kc_tile
    carries = lax.fori_loop(0, n_full, sel_step, (init,) * gps)
    kd = pl.multiple_of(n_full * kc_tile, kc_tile)
    w0 = pl.multiple_of(jnp.clip(q0 + tq - slab, 0, T - slab), tq)
    g_hi, g_lo = _split_bf16(g_ref[0])
    outs = []
    for gi in range(gps):
        q, q_sel, o_c = groups[gi]
        _, l_s, acc_s = attend(carries[gi], q_sel, ks_ref[0, gi, pl.ds(kd, kc_tile), :],
                               vs_ref[0, gi, pl.ds(kd, kc_tile), :], sb_ref[0])
        o_s = acc_s / jnp.maximum(l_s, 1e-30)
        _, l_w, acc_w = attend(init, q, kw_ref[0, gi, pl.ds(w0, slab), :], vw_ref[0, gi, pl.ds(w0, slab), :],
                               wb_ref[0])
        o_w = acc_w / jnp.maximum(l_w, 1e-30)
        G = (jnp.dot(g_hi, gx_ref[gi], preferred_element_type=f32)
             + jnp.dot(g_lo, gx_ref[gi], preferred_element_type=f32))
        for hh in range(HPG):
            rs = slice(hh * tq, (hh + 1) * tq)
            gcol = lambda j: G[:, (hh * 3 + j) * LANE:(hh * 3 + j + 1) * LANE]
            o_h = gcol(0) * o_c[rs] + gcol(1) * o_s[rs] + gcol(2) * o_w[rs]
            outs.append(o_h[:, :HEAD_DIM])
    o_ref[0] = jnp.concatenate(outs, axis=1).astype(o_ref.dtype)


def _nsa_prompt(q_pad, kc_pad, vc_pad, ks_aug, vs_pad, kw_pad, vw_pad, gates):
    B, _, T, _ = q_pad.shape
    tq = _pick_tile(T, 256, 16)
    kc_tile = _pick_tile(T, 512, tq)
    slab = min(T, WINDOW + tq)
    nb = T // BLOCK
    nq = T // tq
    place = np.zeros((nb, LANE), np.float32)
    place[np.arange(nb), HEAD_DIM + np.arange(nb)] = 1.0
    gx = np.zeros((N_KV, LANE, 3 * HPG, LANE), np.float32)
    for g in range(N_KV):
        for c in range(3 * HPG):
            gx[g, g * 3 * HPG + c, c, :] = 1.0
    gx = gx.reshape(N_KV, LANE, 3 * HPG * LANE)
    r = np.arange(tq)[:, None]
    nrel = kc_tile // tq
    sel_bias = np.stack([np.where(np.arange(kc_tile)[None, :] <= rel * tq + r, 0.0, NEG) for rel in range(nrel)])
    n_wb = min(nq, WINDOW // tq + 1) if slab == WINDOW + tq else nq
    win_bias = []
    for i in range(n_wb):
        w0 = min(max(i * tq + tq - slab, 0), T - slab)
        dist = (i * tq + r) - (w0 + np.arange(slab)[None, :])
        win_bias.append(np.where((dist >= 0) & (dist <= WINDOW), 0.0, NEG))
    win_bias = np.stack(win_bias)
    gps = NSA_GROUPS_PER_STEP
    assert N_KV % gps == 0
    kv_spec = pl.BlockSpec((1, gps, T, LANE), lambda b, g, i: (b, g, 0, 0))
    c_spec = pl.BlockSpec((1, gps, nb, LANE), lambda b, g, i: (b, g, 0, 0))
    return pl.pallas_call(
        functools.partial(_nsa_prompt_body, tq=tq, T=T, kc_tile=kc_tile, slab=slab),
        grid=(B, N_KV // gps, nq),
        in_specs=[pl.BlockSpec((1, gps * HPG, tq, LANE), lambda b, g, i: (b, g, i, 0)),
                  c_spec, c_spec, kv_spec, kv_spec, kv_spec, kv_spec,
                  pl.BlockSpec((1, tq, LANE), lambda b, g, i: (b, i, 0)),
                  pl.BlockSpec((gps, LANE, 3 * HPG * LANE), lambda b, g, i: (g, 0, 0)),
                  pl.BlockSpec((nb, LANE), lambda b, g, i: (0, 0)),
                  pl.BlockSpec((1, tq, kc_tile), lambda b, g, i: (i % nrel, 0, 0)),
                  pl.BlockSpec((1, tq, slab), lambda b, g, i: (jnp.minimum(i, n_wb - 1), 0, 0))],
        out_specs=pl.BlockSpec((1, tq, gps * HPG * HEAD_DIM), lambda b, g, i: (b, i, g)),
        out_shape=jax.ShapeDtypeStruct((B, T, ATT_DIM), bf16),
        compiler_params=_cparams(("parallel", "parallel", "arbitrary")),
        name="nsa_prompt",
    )(q_pad, kc_pad, vc_pad, ks_aug, vs_pad, kw_pad, vw_pad, gates,
      jnp.asarray(gx, bf16), jnp.asarray(place, bf16), jnp.asarray(sel_bias, f32), jnp.asarray(win_bias, f32))


def _pages_row_minor(pool):
    return jnp.transpose(pool, (0, 2, 3, 4, 1))


def _split_bf16(x):
    hi = x.astype(bf16)
    return hi, (x - hi.astype(f32)).astype(bf16)


def _dot_f32_rhs(w01, x, terms):
    acc = None
    for _ in range(terms):
        part = x.astype(bf16)
        x = x - part.astype(f32)
        d = jnp.dot(w01, part, preferred_element_type=f32)
        acc = d if acc is None else acc + d
    return acc


def _dot_f32_lhs(x, w01, terms):
    acc = None
    for _ in range(terms):
        part = x.astype(bf16)
        x = x - part.astype(f32)
        d = jnp.dot(part, w01, preferred_element_type=f32)
        acc = d if acc is None else acc + d
    return acc


def _compress_pool_body(pt_ref, *refs, pps):
    wt = refs[pps][...]
    seg = refs[pps + 1][...]
    out_ref = refs[pps + 2]
    for p in range(pps):
        page = refs[p][0]
        x = (page * wt).reshape(KV_COLS, page.shape[-1])
        out_ref[0, p] = _nt_dot(seg, x.astype(bf16))


def _compress_pool(pool_t, page_table, w_cmp, pps):
    DB, NP = page_table.shape
    page = pool_t.shape[-1]
    bpp = page // BLOCK
    wt = jnp.tile(jnp.transpose(w_cmp, (0, 2, 1)), (1, 1, bpp))[:, None]
    seg = jnp.asarray(np.arange(page)[None, :] // BLOCK == np.arange(bpp)[:, None], bf16)

    def page_spec(p):
        return pl.BlockSpec((1, 2, N_KV, HEAD_DIM, page), lambda b, s, pt: (pt[b, s * pps + p], 0, 0, 0, 0))

    return pl.pallas_call(
        functools.partial(_compress_pool_body, pps=pps),
        grid_spec=pltpu.PrefetchScalarGridSpec(
            num_scalar_prefetch=1,
            grid=(DB, NP // pps),
            in_specs=[page_spec(p) for p in range(pps)]
            + [pl.BlockSpec((2, 1, HEAD_DIM, page), lambda b, s, pt: (0, 0, 0, 0)),
               pl.BlockSpec((bpp, page), lambda b, s, pt: (0, 0))],
            out_specs=pl.BlockSpec((1, pps, bpp, KV_COLS), lambda b, s, pt: (b, s, 0, 0))),
        out_shape=jax.ShapeDtypeStruct((DB, NP, bpp, KV_COLS), f32),
        compiler_params=_cparams(("parallel", "arbitrary")),
        name="compress_pool",
    )(page_table, *([pool_t] * pps), wt, seg)


def _nsa_sample_body(pt_ref, *refs, pps, past_len, ts):
    pages = refs[:pps]
    (q_ref, g_ref, summ_ref, ns_ref, wb_ref, nw_ref, ex_ref, o_ref,
     selq_ref, m_ref, l_ref, acc_ref, oc_ref) = refs[pps:]
    s_id = pl.program_id(1)
    n_steps = pl.num_programs(1)
    R = HPG * ts
    nbp = summ_ref.shape[1]
    page = pages[0].shape[-1]
    kt = pps * page
    nbs = kt // BLOCK
    row = lax.broadcasted_iota(jnp.int32, (R, 1), 0)
    tpos = past_len + row % ts

    @pl.when(s_id == 0)
    def _():
        blk = lax.broadcasted_iota(jnp.int32, (R, nbp), 1)
        vis = (blk + 1) * BLOCK - 1 <= tpos
        cols = lambda g, c: summ_ref[0, :, c * K_COLS + g * HEAD_DIM:c * K_COLS + (g + 1) * HEAD_DIM].astype(bf16)
        s_c = jnp.concatenate([jnp.where(vis, _nt_dot(q_ref[0, g], cols(g, 0)), NEG) for g in range(N_KV)], axis=0)
        e_c = jnp.exp(s_c - jnp.max(s_c, axis=-1, keepdims=True))
        e_c = jnp.where(jnp.concatenate([vis] * N_KV, axis=0), e_c, 0.0)
        p_c = e_c / jnp.maximum(jnp.sum(e_c, axis=-1, keepdims=True), 1e-30)
        imps = []
        for g in range(N_KV):
            p_g = p_c[g * R:(g + 1) * R]
            oc_ref[g] = jnp.dot(p_g.astype(bf16), cols(g, 1), preferred_element_type=f32)
            imp = p_g[0:ts]
            for hh in range(1, HPG):
                imp = imp + p_g[hh * ts:(hh + 1) * ts]
            imps.append(imp)
        blk_q = lax.broadcasted_iota(jnp.int32, (N_KV * ts, nbp), 1)
        cur_q = (past_len + lax.broadcasted_iota(jnp.int32, (N_KV * ts, 1), 0) % ts) // BLOCK
        imp = jnp.where(blk_q < cur_q, jnp.concatenate(imps, axis=0), -1.0)
        off = ((1.0 - _select_blocks(imp, N_SEL - 1)) * NEG).astype(bf16)
        for g in range(N_KV):
            off_g = jnp.concatenate([off[g * ts:(g + 1) * ts]] * HPG, axis=0)
            for s in range(nbp // nbs):
                selq_ref[s, g] = off_g[:, s * nbs:(s + 1) * nbs]
        m_ref[...] = jnp.full(m_ref.shape, NEG, f32)
        l_ref[...] = jnp.zeros(l_ref.shape, f32)
        acc_ref[...] = jnp.zeros(acc_ref.shape, f32)

    def online_update(s_groups, pv_of_group):
        s = jnp.concatenate(s_groups, axis=0)
        m_old = m_ref[...]
        m_new = jnp.maximum(m_old, jnp.max(s, axis=-1, keepdims=True))
        alpha = jnp.exp(m_old - m_new)
        e = jnp.exp(s - m_new)
        l_ref[...] = alpha * l_ref[...] + jnp.sum(e, axis=-1, keepdims=True)
        e = e.astype(bf16)
        pv = jnp.concatenate([pv_of_group(g, e[g * R:(g + 1) * R]) for g in range(N_KV)], axis=0)
        acc_ref[...] = alpha * acc_ref[...] + pv
        m_ref[...] = m_new

    k0 = s_id * kt
    kpos = k0 + lax.broadcasted_iota(jnp.int32, (1, kt), 1)
    causal = jnp.where(kpos <= tpos, 0.0, NEG)
    page_rows = lambda c, g: jnp.concatenate([pages[p][0, c, g] for p in range(pps)], axis=1).astype(bf16)
    online_update(
        [jnp.dot(q_ref[0, g], page_rows(0, g), preferred_element_type=f32)
         + jnp.dot(selq_ref[s_id, g], ex_ref[...], preferred_element_type=f32) + causal for g in range(N_KV)],
        lambda g, e: _nt_dot(e, page_rows(1, g)))

    @pl.when(s_id == n_steps - 1)
    def _():
        npos = past_len + lax.broadcasted_iota(jnp.int32, (1, ts), 1)
        keep = wb_ref.shape[-1]
        wpos = past_len - keep + lax.broadcasted_iota(jnp.int32, (1, keep), 1)
        d_old = tpos - wpos
        d_new = tpos - npos
        mk_old = (d_old >= 0) & (d_old <= WINDOW) & (wpos >= 0)
        mk_new = (d_new >= 0) & (d_new <= WINDOW)
        ksl_of = lambda g: slice(g * HEAD_DIM, (g + 1) * HEAD_DIM)
        vsl_of = lambda g: slice(K_COLS + g * HEAD_DIM, K_COLS + (g + 1) * HEAD_DIM)
        online_update(
            [jnp.where(npos <= tpos, _nt_dot(q_ref[0, g], ns_ref[0, :, ksl_of(g)].astype(bf16)), NEG)
             for g in range(N_KV)],
            lambda g, e: jnp.dot(e, ns_ref[0, :, vsl_of(g)].astype(bf16), preferred_element_type=f32))
        o_s_all = acc_ref[...] / jnp.maximum(l_ref[...], 1e-30)
        for g in range(N_KV):
            q = q_ref[0, g]
            ksl, vsl = ksl_of(g), vsl_of(g)
            o_s = o_s_all[g * R:(g + 1) * R]
            s_old = jnp.where(mk_old, jnp.dot(q, wb_ref[0, 0, g].astype(bf16), preferred_element_type=f32), NEG)
            s_new = jnp.where(mk_new, _nt_dot(q, nw_ref[0, :, ksl].astype(bf16)), NEG)
            m = jnp.maximum(jnp.max(s_old, axis=-1, keepdims=True), jnp.max(s_new, axis=-1, keepdims=True))
            e_old = jnp.where(mk_old, jnp.exp(s_old - m), 0.0)
            e_new = jnp.where(mk_new, jnp.exp(s_new - m), 0.0)
            den = jnp.sum(e_old, axis=-1, keepdims=True) + jnp.sum(e_new, axis=-1, keepdims=True)
            o_w = (_nt_dot(e_old.astype(bf16), wb_ref[0, 1, g].astype(bf16))
                   + jnp.dot(e_new.astype(bf16), nw_ref[0, :, vsl].astype(bf16), preferred_element_type=f32)
                   ) / jnp.maximum(den, 1e-30)
            gt = g_ref[0, g]
            o_ref[0, g] = gt[:, 0:1] * oc_ref[g] + gt[:, 1:2] * o_s + gt[:, 2:3] * o_w


def _nsa_sample(q_g, gates_g, summ, new_sel, win_t, new_win, pool_t, page_table, pps, past_len):
    DB, NP = page_table.shape
    ts = new_sel.shape[1]
    R = HPG * ts
    nbp = summ.shape[1]
    page = pool_t.shape[-1]
    keep = win_t.shape[-1]
    kt = pps * page
    nbs = kt // BLOCK
    expand = jnp.asarray(np.arange(kt)[None, :] // BLOCK == np.arange(nbs)[:, None], bf16)

    def page_spec(p):
        return pl.BlockSpec((1, 2, N_KV, HEAD_DIM, page), lambda b, s, pt: (pt[b, s * pps + p], 0, 0, 0, 0))

    per_b4 = lambda b, s, pt: (b, 0, 0, 0)
    per_b3 = lambda b, s, pt: (b, 0, 0)
    return pl.pallas_call(
        functools.partial(_nsa_sample_body, pps=pps, past_len=past_len, ts=ts),
        grid_spec=pltpu.PrefetchScalarGridSpec(
            num_scalar_prefetch=1,
            grid=(DB, NP // pps),
            in_specs=[page_spec(p) for p in range(pps)]
            + [pl.BlockSpec((1, N_KV, R, HEAD_DIM), per_b4),
               pl.BlockSpec((1, N_KV, R, 3), per_b4),
               pl.BlockSpec((1, nbp, KV_COLS), per_b3),
               pl.BlockSpec((1, ts, KV_COLS), per_b3),
               pl.BlockSpec((1, 2, N_KV, HEAD_DIM, keep), lambda b, s, pt: (b, 0, 0, 0, 0)),
               pl.BlockSpec((1, ts, KV_COLS), per_b3),
               pl.BlockSpec((nbs, kt), lambda b, s, pt: (0, 0))],
            out_specs=pl.BlockSpec((1, N_KV, R, HEAD_DIM), per_b4),
            scratch_shapes=[pltpu.VMEM((NP // pps, N_KV, R, nbs), bf16),
                            pltpu.VMEM((N_KV * R, 1), f32),
                            pltpu.VMEM((N_KV * R, 1), f32),
                            pltpu.VMEM((N_KV * R, HEAD_DIM), f32),
                            pltpu.VMEM((N_KV, R, HEAD_DIM), f32)]),
        out_shape=jax.ShapeDtypeStruct((DB, N_KV, R, HEAD_DIM), f32),
        compiler_params=_cparams(("parallel", "arbitrary")),
        name="nsa_sample",
    )(page_table, *([pool_t] * pps), q_g, gates_g, summ, new_sel, win_t, new_win, expand)


def _bmm(spec, a, b):
    return jnp.einsum(spec, a.astype(bf16), b.astype(bf16), preferred_element_type=f32)


def _unit_lower_solve(L, rhs, C, bs):
    _mm = functools.partial(_bmm, "hij,hjk->hik")
    ri = lax.broadcasted_iota(jnp.int32, (1, C, C), 1)
    ci = lax.broadcasted_iota(jnp.int32, (1, C, C), 2)
    same = (ri // bs) == (ci // bs)
    eye = jnp.where(ri == ci, 1.0, 0.0)
    D = jnp.where(same, L, 0.0)
    T = eye - D
    P = D
    n = 2
    while n < bs:
        P = _mm(P, P)
        T = T + _mm(T, P)
        n *= 2
    x = _mm(T, rhs)
    nblk = C // bs
    if nblk == 1:
        return x
    Mb = _mm(T, jnp.where(same, 0.0, L))
    factors = []
    Pm = Mb
    n = 2
    while n < nblk:
        Pm = _mm(Pm, Pm)
        factors.append(Pm)
        n *= 2
    for Pm in reversed(factors):
        x = x + _mm(Pm, x)
    return x - _mm(Mb, x)


def _rwkv_body(ps_ref, sp_ref, s0_ref, mu_ref, w0_ref, ww_ref, a0_ref, wa_ref, kk_ref, ka_ref, rk_ref,
               lnw_ref, lnb_ref, seg_ref, segt_ref, y_ref, so_ref, carry_ref, state_ref, *, C, bs):
    c = pl.program_id(1)
    nbat = ps_ref.shape[0]
    H = RW_HEADS

    @pl.when(c == 0)
    def _():
        carry_ref[...] = sp_ref[...]
        state_ref[...] = s0_ref[...].reshape(state_ref.shape)

    def heads(x):
        return [x[:, h * RW_HEAD:(h + 1) * RW_HEAD] for h in range(H)]

    ri = lax.broadcasted_iota(jnp.int32, (C, C), 0)
    ci = lax.broadcasted_iota(jnp.int32, (C, C), 1)
    tril = jnp.where(ci <= ri, 1.0, 0.0).astype(bf16)
    rowi = lax.broadcasted_iota(jnp.int32, (C, 1), 0)

    x1_l, x2_l, kb_l, v_l, rk_l, etot_l = [], [], [], [], [], []
    for n in range(nbat):
        ps = ps_ref[n]
        prev = jnp.where(rowi == 0, carry_ref[n], pltpu.roll(ps, 1, axis=0))
        carry_ref[n] = ps[C - 1:C, :]
        z = ps + (prev - ps) * mu_ref[...]
        r = z[:, 0:RW_DIM]
        k = z[:, RW_DIM:2 * RW_DIM]
        v = z[:, 2 * RW_DIM:3 * RW_DIM]
        xw = z[:, 3 * RW_DIM:3 * RW_DIM + LORA_W]
        xa = z[:, 3 * RW_DIM + LORA_W:]
        u = -(w0_ref[...] + jnp.dot(jnp.tanh(xw).astype(bf16), ww_ref[...], preferred_element_type=f32))
        softplus = jnp.maximum(u, 0.0) + jnp.log(1.0 + jnp.exp(-jnp.abs(u)))
        lw = -jnp.exp(-softplus - 0.5)
        a = jax.nn.sigmoid(a0_ref[...] + jnp.dot(xa.astype(bf16), wa_ref[...], preferred_element_type=f32))
        kk = k * kk_ref[...]
        ss = _dot_f32_lhs(kk * kk, seg_ref[...], 2)
        kk = kk * _dot_f32_lhs(lax.rsqrt(jnp.maximum(ss, 1e-24)), segt_ref[...], 2)
        bb = kk * a
        k2 = k * (1.0 + (a - 1.0) * ka_ref[...])
        G = _dot_f32_rhs(tril, lw, 3)
        g_end = G[C - 1:C, :]
        e_neg = jnp.exp(-G)
        e_end = jnp.exp(g_end - G)
        x1_l.append(heads(jnp.concatenate([kk * jnp.exp(G - lw), r * jnp.exp(G)], axis=0).astype(bf16)))
        x2_l.append(heads(jnp.concatenate([k2 * e_neg, bb * e_neg], axis=0).astype(bf16)))
        kb_l.append(heads(jnp.concatenate([k2 * e_end, -(bb * e_end)], axis=0).astype(bf16)))
        v_l.append(heads(v))
        rk_l.append(heads(r * k2 * rk_ref[...]))
        etot_l.append(heads(jnp.exp(g_end)))

    stack = lambda lst: jnp.stack([t for per_b in lst for t in per_b], axis=0)
    X1, X2, KB = stack(x1_l), stack(x2_l), stack(kb_l)
    V, RK, ETOT = stack(v_l), stack(rk_l), stack(etot_l)

    strict = (ci < ri)[None]
    incl = (ci <= ri)[None]
    S = state_ref[...]
    A = _bmm("hck,hdk->hcd", X1, X2)
    P = _bmm("hck,hvk->hcv", X1, S)
    a_kk = jnp.where(strict, A[:, :C, :C], 0.0)
    a_kb = jnp.where(strict, A[:, :C, C:], 0.0)
    rhs = P[:, :C] + _bmm("hcd,hdv->hcv", a_kk, V)
    sa = _unit_lower_solve(a_kb, rhs, C, bs)
    a_r = jnp.concatenate([jnp.where(incl, A[:, C:, :C], 0.0), jnp.where(incl, -A[:, C:, C:], 0.0)], axis=2)
    vs = jnp.concatenate([V, sa], axis=1)
    y = P[:, C:] + _bmm("hcd,hdv->hcv", a_r, vs)
    state_ref[...] = S * ETOT + _bmm("hcv,hck->hvk", vs, KB)
    mean = jnp.mean(y, axis=-1, keepdims=True)
    var = jnp.mean(jnp.square(y - mean), axis=-1, keepdims=True)
    yn = (y - mean) * lax.rsqrt(var + LN_X_EPS)
    bonus = jnp.sum(RK, axis=-1, keepdims=True) * V
    for n in range(nbat):
        for h in range(H):
            sl = slice(h * RW_HEAD, (h + 1) * RW_HEAD)
            i = n * H + h
            y_ref[n, :, sl] = (yn[i] * lnw_ref[:, sl] + lnb_ref[:, sl] + bonus[i]).astype(y_ref.dtype)

    so_ref[...] = state_ref[...].reshape(so_ref.shape)


def _rwkv(p_shift, shift_prev, s0, rw):
    mu, w0, w_lora_w, a0, w_lora_a, k_k, k_a, r_k, ln_w, ln_b = rw
    B, T, _ = p_shift.shape
    C = _pick_tile(T, 64, 8)
    bs = min(16, C)
    nbat = _pick_tile(B, RWKV_BATCH_PER_STEP, 1)
    seg_np = (np.arange(RW_DIM)[:, None] // RW_HEAD == np.arange(RW_HEADS)[None, :]).astype(np.float32)
    vec = lambda n: pl.BlockSpec((1, n), lambda b, c: (0, 0))
    row = lambda t: t.reshape(1, -1)
    y, s_new = pl.pallas_call(
        functools.partial(_rwkv_body, C=C, bs=bs),
        grid=(B // nbat, T // C),
        in_specs=[pl.BlockSpec((nbat, C, SHIFT_COLS), lambda b, c: (b, c, 0)),
                  pl.BlockSpec((nbat, 1, SHIFT_COLS), lambda b, c: (b, 0, 0)),
                  pl.BlockSpec((nbat, RW_HEADS, RW_HEAD, RW_HEAD), lambda b, c: (b, 0, 0, 0)),
                  vec(SHIFT_COLS), vec(RW_DIM),
                  pl.BlockSpec((LORA_W, RW_DIM), lambda b, c: (0, 0)),
                  vec(RW_DIM),
                  pl.BlockSpec((LORA_A, RW_DIM), lambda b, c: (0, 0)),
                  vec(RW_DIM), vec(RW_DIM), vec(RW_DIM), vec(RW_DIM), vec(RW_DIM),
                  pl.BlockSpec((RW_DIM, RW_HEADS), lambda b, c: (0, 0)),
                  pl.BlockSpec((RW_HEADS, RW_DIM), lambda b, c: (0, 0))],
        out_specs=[pl.BlockSpec((nbat, C, RW_DIM), lambda b, c: (b, c, 0)),
                   pl.BlockSpec((nbat, RW_HEADS, RW_HEAD, RW_HEAD), lambda b, c: (b, 0, 0, 0))],
        out_shape=[jax.ShapeDtypeStruct((B, T, RW_DIM), bf16),
                   jax.ShapeDtypeStruct((B, RW_HEADS, RW_HEAD, RW_HEAD), f32)],
        scratch_shapes=[pltpu.VMEM((nbat, 1, SHIFT_COLS), f32),
                        pltpu.VMEM((nbat * RW_HEADS, RW_HEAD, RW_HEAD), f32)],
        compiler_params=_cparams(("parallel", "arbitrary")),
        name="rwkv",
    )(p_shift, shift_prev.reshape(B, 1, SHIFT_COLS), s0, row(mu), row(w0), w_lora_w.astype(bf16), row(a0),
      w_lora_a.astype(bf16), row(k_k), row(k_a), row(r_k), row(ln_w), row(ln_b),
      jnp.asarray(seg_np, bf16), jnp.asarray(seg_np.T, bf16))
    return y, s_new


def _merge_body(x_ref, oa_ref, yr_ref, ga_ref, gb_ref, wa_ref, wb_ref, wo_ref, h_ref):
    ma = jnp.dot(oa_ref[...], wa_ref[...], preferred_element_type=f32)
    mb = jnp.dot(yr_ref[...], wb_ref[...], preferred_element_type=f32)
    m = jax.nn.sigmoid(ga_ref[...]) * ma + jax.nn.sigmoid(gb_ref[...]) * mb
    h_ref[...] = x_ref[...] + jnp.dot(m.astype(bf16), wo_ref[...], preferred_element_type=f32)


def _merge(x, o_att, y_rw, p_merge, w_a, w_b, w_o):
    M, D = x.shape
    tm = _pick_tile(M, 256, 8)
    row = lambda i: (i, 0)
    const = lambda i: (0, 0)
    return pl.pallas_call(
        _merge_body,
        grid=(M // tm,),
        in_specs=[pl.BlockSpec((tm, D), row),
                  pl.BlockSpec((tm, ATT_DIM), row),
                  pl.BlockSpec((tm, RW_DIM), row),
                  pl.BlockSpec((tm, D), lambda i: (i, 0)),
                  pl.BlockSpec((tm, D), lambda i: (i, 1)),
                  pl.BlockSpec((ATT_DIM, D), const),
                  pl.BlockSpec((RW_DIM, D), const),
                  pl.BlockSpec((D, D), const)],
        out_specs=pl.BlockSpec((tm, D), row),
        out_shape=jax.ShapeDtypeStruct((M, D), f32),
        compiler_params=_cparams(("parallel",)),
        name="merge",
    )(x, o_att, y_rw, p_merge, p_merge, w_a, w_b, w_o)


def _conv_ffn_body(h_ref, g_ref, wug_ref, wuv_ref, cwg_ref, cwv_ref, cbg_ref, cbv_ref, wd_ref,
                   pg_ref, pv_ref, y_ref, tg_ref, tv_ref, hn_ref, cg_ref, cv_ref, *, tm, seq_rows, tail, nsub):
    i = pl.program_id(1)
    j = pl.program_id(2)
    carried = tm <= seq_rows

    @pl.when(j == 0)
    def _():
        h = h_ref[0]
        ms = jnp.mean(h * h, axis=-1, keepdims=True)
        hn_ref[...] = (h * lax.rsqrt(ms + NORM_EPS) * g_ref[...]).astype(bf16)
        y_ref[0] = h

    def taps(cw_ref, cb_ref, u2, u1, u):
        return cb_ref[...] + cw_ref[0:1, :] * u2 + cw_ref[1:2, :] * u1 + cw_ref[2:3, :] * u

    def finish(rows, gate, val):
        act = (gate * jax.nn.sigmoid(gate) * val).astype(bf16)
        y_ref[0, rows, :] += jnp.dot(act, wd_ref[...], preferred_element_type=f32)

    if not carried:
        t_in = lax.broadcasted_iota(jnp.int32, (tm, 1), 0) % seq_rows
        hn = hn_ref[...]

        def conv(u, cw_ref, cb_ref, prev_ref):
            pr = prev_ref[0]
            u1 = jnp.where(t_in == 0, pltpu.roll(pr, tm - 1, axis=0), pltpu.roll(u, 1, axis=0))
            u2 = jnp.where(t_in == 0, pr, jnp.where(t_in == 1, pr, pltpu.roll(u, 2, axis=0)))
            return taps(cw_ref, cb_ref, u2, u1, u)

        ug = jnp.dot(hn, wug_ref[...], preferred_element_type=f32)
        uv = jnp.dot(hn, wuv_ref[...], preferred_element_type=f32)
        tg_ref[0, 0] = ug
        tv_ref[0, 0] = uv
        finish(slice(None), conv(ug, cwg_ref, cbg_ref, pg_ref), conv(uv, cwv_ref, cbv_ref, pv_ref))
        return

    @pl.when(i == 0)
    def _():
        cg_ref[j] = pg_ref[0]
        cv_ref[j] = pv_ref[0]

    ts = tm // nsub
    rowi = lax.broadcasted_iota(jnp.int32, (ts, 1), 0)
    prev_g = (cg_ref[j, 0:1, :], cg_ref[j, 1:2, :])
    prev_v = (cv_ref[j, 0:1, :], cv_ref[j, 1:2, :])

    def conv(u, cw_ref, cb_ref, prev):
        p2, p1 = prev
        u1 = jnp.where(rowi == 0, p1, pltpu.roll(u, 1, axis=0))
        u2 = jnp.where(rowi == 0, p2, jnp.where(rowi == 1, p1, pltpu.roll(u, 2, axis=0)))
        return taps(cw_ref, cb_ref, u2, u1, u), (u[ts - 2:ts - 1, :], u[ts - 1:ts, :])

    for sb in range(nsub):
        rows = slice(sb * ts, (sb + 1) * ts)
        hn = hn_ref[rows, :]
        ug = jnp.dot(hn, wug_ref[...], preferred_element_type=f32)
        uv = jnp.dot(hn, wuv_ref[...], preferred_element_type=f32)
        gate, prev_g = conv(ug, cwg_ref, cbg_ref, prev_g)
        val, prev_v = conv(uv, cwv_ref, cbv_ref, prev_v)
        finish(rows, gate, val)
    cg_ref[j] = jnp.concatenate(prev_g, axis=0)
    cv_ref[j] = jnp.concatenate(prev_v, axis=0)
    tg_ref[0, 0] = ug[ts - tail:ts, :]
    tv_ref[0, 0] = uv[ts - tail:ts, :]


def _conv_ffn(h, conv_prev, norm_g, w_up, conv_w, conv_b, w_down, *, fold):
    B, T, D = h.shape
    dff = w_down.shape[0]
    tf = _pick_tile(dff, 512, LANE)
    nf = dff // tf
    if not fold:
        nb_, tm = B, _pick_tile(T, 1024, 8)
        tail = 8
        hh = h
        prev = conv_prev
        prev_spec_g = pl.BlockSpec((1, CONV_W - 1, tf), lambda b, i, j: (b, 0, j))
        prev_spec_v = pl.BlockSpec((1, CONV_W - 1, tf), lambda b, i, j: (b, 0, nf + j))
    else:
        nb_, tm = 1, B * T
        tail = tm
        hh = h.reshape(1, B * T, D)
        assert T >= CONV_W - 1
        prev = jnp.concatenate([conv_prev, jnp.zeros((B, T - (CONV_W - 1), 2 * dff), f32)],
                               axis=1).reshape(1, B * T, 2 * dff)
        prev_spec_g = pl.BlockSpec((1, tm, tf), lambda b, i, j: (0, 0, j))
        prev_spec_v = pl.BlockSpec((1, tm, tf), lambda b, i, j: (0, 0, nf + j))
    nt = hh.shape[1] // tm
    nsub = FFN_ROW_SUBBLOCKS if (not fold and tm % (8 * FFN_ROW_SUBBLOCKS) == 0) else 1
    body = functools.partial(_conv_ffn_body, tm=tm, seq_rows=T, tail=tail, nsub=nsub)
    cw = conv_w
    cb = conv_b.reshape(1, 2 * dff)
    tail_spec = pl.BlockSpec((1, 1, tail, tf), lambda b, i, j: (b, i, 0, j))
    tail_shape = jax.ShapeDtypeStruct((nb_, nt, tail, dff), f32)
    y, ug, uv = pl.pallas_call(
        body,
        grid=(nb_, nt, nf),
        in_specs=[pl.BlockSpec((1, tm, D), lambda b, i, j: (b, i, 0)),
                  pl.BlockSpec((1, D), lambda b, i, j: (0, 0)),
                  pl.BlockSpec((D, tf), lambda b, i, j: (0, j)),
                  pl.BlockSpec((D, tf), lambda b, i, j: (0, nf + j)),
                  pl.BlockSpec((CONV_W, tf), lambda b, i, j: (0, j)),
                  pl.BlockSpec((CONV_W, tf), lambda b, i, j: (0, nf + j)),
                  pl.BlockSpec((1, tf), lambda b, i, j: (0, j)),
                  pl.BlockSpec((1, tf), lambda b, i, j: (0, nf + j)),
                  pl.BlockSpec((tf, D), lambda b, i, j: (j, 0)),
                  prev_spec_g, prev_spec_v],
        out_specs=[pl.BlockSpec((1, tm, D), lambda b, i, j: (b, i, 0)), tail_spec, tail_spec],
        out_shape=[jax.ShapeDtypeStruct(hh.shape, f32), tail_shape, tail_shape],
        scratch_shapes=[pltpu.VMEM((tm, D), bf16),
                        pltpu.VMEM((nf, CONV_W - 1, tf), f32),
                        pltpu.VMEM((nf, CONV_W - 1, tf), f32)],
        compiler_params=_cparams(("parallel", "arbitrary", "arbitrary")),
        name="conv_ffn",
    )(hh, norm_g.reshape(1, D), w_up, w_up, cw, cw, cb, cb, w_down, prev, prev)
    return y, ug[:, -1], uv[:, -1]


def _split_w_in(w_in):
    o = 0
    parts = []
    for n in (ATT_DIM, KV_COLS, KV_COLS, KV_COLS, 3 * N_HEADS, SHIFT_COLS, 2 * w_in.shape[0]):
        parts.append(w_in[:, o:o + n])
        o += n
    wq, wc, ws, ww, wg, wsh, wm = parts
    wg = jnp.pad(wg, ((0, 0), (0, LANE - 3 * N_HEADS)))
    w_att = jnp.concatenate([wq, wc, ws, ww, wg], axis=1).astype(bf16)
    return w_att, wsh.astype(bf16), wm.astype(bf16)


def _head_major(x, n):
    B, T, _ = x.shape
    return x.reshape(B, T, n, HEAD_DIM).transpose(0, 2, 1, 3)


def _mixer_inputs(x2d, pos_tab, norm_g, w_parts, consts, wc, with_summ):
    w_att, w_sh, w_mg = w_parts
    p_att = _norm_matmul(x2d, norm_g, w_att)
    p_shift = _norm_matmul(x2d, norm_g, w_sh)
    p_merge = _norm_matmul(x2d, norm_g, w_mg)
    post = _qk_post(p_att, pos_tab, consts, wc, with_summ)
    return post, p_shift, p_merge


def kernel(x_prompt, x_sample, cache_kv_cmp, cache_kv_sel, page_table, cache_kv_win, state_wkv, state_shift, state_conv, norm1_g, w_in, q_gain, k_gains, w_cmp, mu_shift, w0, w_lora_w, a0, w_lora_a, k_k, k_a, r_k, ln_x_w, ln_x_b, w_branch_a, w_branch_b, w_out, norm2_g, w_up, conv_w, conv_b, w_down):
    B, T, D = x_prompt.shape
    DB, TS, _ = x_sample.shape
    depth = w_in.shape[0]
    assert depth == 1, "single-layer trunk"
    l = 0
    page = cache_kv_cmp.shape[2]
    n_pages = page_table.shape[1]
    past_len = n_pages * page
    assert past_len % BLOCK == 0 and TS <= BLOCK and page % BLOCK == 0 and T % BLOCK == 0
    dff = w_down.shape[1]

    w_parts = _split_w_in(w_in[l])
    consts = _qk_consts(q_gain[l], k_gains[l])
    wc = _compress_weights(w_cmp[l])
    rw = (mu_shift[l], w0[l], w_lora_w[l], a0[l], w_lora_a[l], k_k[l], k_a[l], r_k[l], ln_x_w[l], ln_x_b[l])
    w_a = w_branch_a[l].astype(bf16)
    w_b = w_branch_b[l].astype(bf16)
    w_o = w_out[l].astype(bf16)
    w_u = w_up[l].astype(bf16)
    w_d = w_down[l].astype(bf16)

    xp = x_prompt.reshape(B * T, D)
    tabs_p = _rope_tables(jnp.arange(T, dtype=jnp.int32))
    ((q_pad, kvc, kvs, kvw, gates, summ, ks_aug, vs_pad, kw_pad, vw_pad),
     p_shift, p_merge) = _mixer_inputs(xp, tabs_p, norm1_g[l], w_parts, consts, wc, True)
    nb = T // BLOCK
    summ = summ.reshape(B, nb, KV_COLS)
    lane_pad = lambda x: jnp.pad(x, ((0, 0), (0, 0), (0, 0), (0, LANE - HEAD_DIM))).astype(bf16)
    kvw3 = kvw.reshape(B, T, KV_COLS)
    o_att = _nsa_prompt(q_pad,
                        lane_pad(_head_major(summ[:, :, :K_COLS], N_KV)),
                        lane_pad(_head_major(summ[:, :, K_COLS:], N_KV)),
                        ks_aug, vs_pad, kw_pad, vw_pad, gates.reshape(B, T, LANE))
    p_shift3 = p_shift.reshape(B, T, SHIFT_COLS)
    y_rw, wkv_p = _rwkv(p_shift3, jnp.zeros((B, SHIFT_COLS), f32),
                        jnp.zeros((B, RW_HEADS, RW_HEAD, RW_HEAD), f32), rw)
    h_p = _merge(xp, o_att.reshape(B * T, ATT_DIM), y_rw.reshape(B * T, RW_DIM), p_merge, w_a, w_b, w_o)
    y_p, ug, uv = _conv_ffn(h_p.reshape(B, T, D), jnp.zeros((B, CONV_W - 1, 2 * dff), f32), norm2_g[l],
                            w_u, conv_w[l], conv_b[l], w_d, fold=False)
    assert T >= CONV_W - 1
    conv_p = jnp.concatenate([ug[:, -(CONV_W - 1):], uv[:, -(CONV_W - 1):]], axis=-1)
    kv_shape_p = (1, B, T, 2, N_KV, HEAD_DIM)
    keep_p = min(WINDOW, T)
    outs_p = (y_p,
              kvc.reshape(kv_shape_p), kvs.reshape(kv_shape_p),
              kvw3[:, T - keep_p:].reshape(1, B, keep_p, 2, N_KV, HEAD_DIM),
              wkv_p[None], p_shift3[:, -1][None], conv_p[None])

    xs = x_sample.reshape(DB * TS, D)
    pos_s = past_len + jnp.arange(TS, dtype=jnp.int32)
    tabs_s = tuple(jnp.tile(t, (DB, 1)) for t in _rope_tables(pos_s))
    (q, kvc_s, kvs_s, kvw_s, gates), p_shift, p_merge = _mixer_inputs(xs, tabs_s, norm1_g[l], w_parts, consts, wc, False)
    pps = _pick_tile(n_pages, PAGES_PER_STEP, 1)
    summ_s = _compress_pool(_pages_row_minor(cache_kv_cmp[l]), page_table, w_cmp[l], pps)
    summ_s = summ_s.reshape(DB, past_len // BLOCK, KV_COLS)
    R = HPG * TS
    q_g = _head_major(q.reshape(DB, TS, ATT_DIM), N_HEADS).reshape(DB, N_KV, R, HEAD_DIM)
    gates_g = (gates[:, :3 * N_HEADS].reshape(DB, TS, N_HEADS, 3).transpose(0, 2, 1, 3)
               .reshape(DB, N_KV, R, 3))
    keep = cache_kv_win.shape[2]
    kvw_s3 = kvw_s.reshape(DB, TS, KV_COLS)
    o_g = _nsa_sample(q_g, gates_g, summ_s, kvs_s.reshape(DB, TS, KV_COLS), _pages_row_minor(cache_kv_win[l]),
                      kvw_s3, _pages_row_minor(cache_kv_sel[l]), page_table, pps, past_len)
    o_att_s = (o_g.reshape(DB, N_HEADS, TS, HEAD_DIM).transpose(0, 2, 1, 3)
               .reshape(DB * TS, ATT_DIM).astype(bf16))
    p_shift3s = p_shift.reshape(DB, TS, SHIFT_COLS)
    y_rw_s, wkv_s = _rwkv(p_shift3s, state_shift[l], state_wkv[l], rw)
    h_s = _merge(xs, o_att_s, y_rw_s.reshape(DB * TS, RW_DIM), p_merge, w_a, w_b, w_o)
    y_s, ug, uv = _conv_ffn(h_s.reshape(DB, TS, D), state_conv[l], norm2_g[l],
                            w_u, conv_w[l], conv_b[l], w_d, fold=True)
    up_s = jnp.concatenate([state_conv[l],
                            jnp.concatenate([ug.reshape(DB, TS, dff), uv.reshape(DB, TS, dff)], axis=-1)], axis=1)
    conv_s = up_s[:, TS:]
    win_s = jnp.concatenate([cache_kv_win[l], kvw_s.reshape(DB, TS, 2, N_KV, HEAD_DIM)], axis=1)[:, TS:]
    kv_shape_s = (1, DB, TS, 2, N_KV, HEAD_DIM)

    return (outs_p[0], y_s.reshape(DB, TS, D),
            outs_p[1], kvc_s.reshape(kv_shape_s),
            outs_p[2], kvs_s.reshape(kv_shape_s),
            outs_p[3], win_s.reshape(1, DB, keep, 2, N_KV, HEAD_DIM),
            outs_p[4], wkv_s[None],
            outs_p[5], p_shift3s[:, -1][None],
            outs_p[6], conv_s[None])
```

```python
import functools

import numpy as np
import jax
import jax.numpy as jnp
from jax import lax
from jax.experimental import pallas as pl
from jax.experimental.pallas import tpu as pltpu

f32 = jnp.float32
bf16 = jnp.bfloat16

N_HEADS = 16
N_KV = 4
HPG = N_HEADS // N_KV
HEAD_DIM = 64
ROPE_DIM = HEAD_DIM // 4
ROPE_THETA = 500000.0
BLOCK = 64
N_SEL = 16
WINDOW = 512
RW_HEADS = 16
RW_HEAD = 64
RW_DIM = RW_HEADS * RW_HEAD
LORA_W = 64
LORA_A = 64
LN_X_EPS = 64e-5
CONV_W = 3
NORM_EPS = 1e-6
ATT_DIM = N_HEADS * HEAD_DIM
KV_COLS = 2 * N_KV * HEAD_DIM
K_COLS = N_KV * HEAD_DIM
SHIFT_COLS = 3 * RW_DIM + LORA_W + LORA_A
NEG = -1e30
LOG2E = 1.4426950408889634

LANE = 128
VMEM_LIMIT = 56 * 1024 * 1024
NORM_MATMUL_VMEM_BUDGET = 50 * 1024 * 1024
QKV_COLS = ATT_DIM + 3 * KV_COLS
ATT_PROJ_COLS = QKV_COLS + LANE
N_NORM_HEADS = QKV_COLS // HEAD_DIM
RWKV_BATCH_PER_STEP = 2
FFN_ROW_SUBBLOCKS = 2
NSA_GROUPS_PER_STEP = 1
PAGES_PER_STEP = 16


def _cparams(sem):
    return pltpu.CompilerParams(dimension_semantics=sem, vmem_limit_bytes=VMEM_LIMIT)


def _pick_tile(n, cap, mult):
    best = None
    for t in range(mult, min(n, cap) + 1, mult):
        if n % t == 0:
            best = t
    assert best is not None, (n, cap, mult)
    return best


def _norm_matmul_body(x_ref, g_ref, w_ref, o_ref, xn_ref):
    @pl.when(pl.program_id(1) == 0)
    def _():
        x = x_ref[...]
        ms = jnp.mean(x * x, axis=-1, keepdims=True)
        xn_ref[...] = (x * lax.rsqrt(ms + NORM_EPS) * g_ref[...]).astype(bf16)

    o_ref[...] = jnp.dot(xn_ref[...], w_ref[...], preferred_element_type=f32)


def _norm_matmul(x, gain, w):
    M, D = x.shape
    N = w.shape[1]
    tm = _pick_tile(M, 512, 8)
    tn = N
    if 2 * tm * D * 4 + tm * D * 2 + 2 * D * tn * 2 + 2 * tm * tn * 4 > NORM_MATMUL_VMEM_BUDGET:
        tm = _pick_tile(M, 1024, 8)
        tn = _pick_tile(N, 1024, 2 * LANE)
    return pl.pallas_call(
        _norm_matmul_body,
        grid=(M // tm, N // tn),
        in_specs=[pl.BlockSpec((tm, D), lambda i, j: (i, 0)),
                  pl.BlockSpec((1, D), lambda i, j: (0, 0)),
                  pl.BlockSpec((D, tn), lambda i, j: (0, j))],
        out_specs=pl.BlockSpec((tm, tn), lambda i, j: (i, j)),
        out_shape=jax.ShapeDtypeStruct((M, N), f32),
        scratch_shapes=[pltpu.VMEM((tm, D), bf16)],
        compiler_params=_cparams(("parallel", "arbitrary")),
        name="norm_matmul",
    )(x, gain.reshape(1, D), w)


def _qk_post_body(p_ref, cos_ref, sa_ref, sb_ref, gvec_ref, isk_ref, seg_ref, segt_ref, wc_ref,
                  q_ref, kvc_ref, kvs_ref, kvw_ref, gate_ref, *prompt_refs, nt):
    y = p_ref[:, :QKV_COLS]
    isk = isk_ref[...] > 0.5
    sq_hi, sq_lo = _split_bf16(y * y)
    ss = (jnp.dot(sq_hi, seg_ref[...], preferred_element_type=f32)
          + jnp.dot(sq_lo, seg_ref[...], preferred_element_type=f32)) * (1.0 / HEAD_DIM)
    rs_hi, rs_lo = _split_bf16(lax.rsqrt(ss + NORM_EPS))
    rb = (jnp.dot(rs_hi, segt_ref[...], preferred_element_type=f32)
          + jnp.dot(rs_lo, segt_ref[...], preferred_element_type=f32))
    yn = jnp.where(isk, y * rb * gvec_ref[...], y)
    reps = QKV_COLS // LANE
    cos = jnp.where(isk, jnp.concatenate([cos_ref[...]] * reps, axis=1), 1.0)
    sa = jnp.where(isk, jnp.concatenate([sa_ref[...]] * reps, axis=1), 0.0)
    sb = jnp.where(isk, jnp.concatenate([sb_ref[...]] * reps, axis=1), 0.0)
    half = ROPE_DIM // 2
    out = (yn * cos + pltpu.roll(yn, QKV_COLS - half, axis=1) * sa + pltpu.roll(yn, half, axis=1) * sb)
    qs = out[:, :ATT_DIM] * (HEAD_DIM ** -0.5 * (LOG2E if prompt_refs else 1.0))
    kvc = out[:, ATT_DIM:ATT_DIM + KV_COLS]
    kvs = out[:, ATT_DIM + KV_COLS:ATT_DIM + 2 * KV_COLS]
    kvw = out[:, ATT_DIM + 2 * KV_COLS:]
    kvc_ref[...] = kvc
    kvs_ref[...] = kvs
    kvw_ref[...] = kvw
    gates = jax.nn.sigmoid(p_ref[:, QKV_COLS:])
    if not prompt_refs:
        gate_ref[...] = gates
        q_ref[...] = qs.astype(bf16)
        return
    for g in range(N_KV):
        gate_ref[0, g] = gates if g == 0 else pltpu.roll(gates, LANE - g * 3 * HPG, axis=1)
    summ_ref, ksa_ref, vsp_ref, kwp_ref, vwp_ref = prompt_refs
    tm = kvc.shape[0]
    blk = kvc.reshape(tm // BLOCK, BLOCK, KV_COLS) * wc_ref[...][None]
    summ_ref[0] = jnp.sum(blk, axis=1)
    zeros = jnp.zeros((tm, LANE - HEAD_DIM), f32)
    t0 = (pl.program_id(0) % nt) * tm
    blk_of_row = (t0 + lax.broadcasted_iota(jnp.int32, zeros.shape, 0)) // BLOCK
    onehot = jnp.where(lax.broadcasted_iota(jnp.int32, zeros.shape, 1) == blk_of_row, 1.0, 0.0)
    hd = lambda x, h: x[:, h * HEAD_DIM:(h + 1) * HEAD_DIM]
    pad = lambda x, tail: jnp.concatenate([x, tail], axis=1).astype(bf16)
    for h in range(N_HEADS):
        q_ref[0, h] = pad(hd(qs, h), zeros)
    for g in range(N_KV):
        ksa_ref[0, g] = pad(hd(kvs, g), onehot)
        vsp_ref[0, g] = pad(hd(kvs, N_KV + g), zeros)
        kwp_ref[0, g] = pad(hd(kvw, g), zeros)
        vwp_ref[0, g] = pad(hd(kvw, N_KV + g), zeros)


def _qk_post(p_att, tabs, consts, wc, prompt):
    M = p_att.shape[0]
    cos_t, sa_t, sb_t = tabs
    Tt = cos_t.shape[0]
    tm = _pick_tile(Tt, 256, BLOCK if prompt else 8)
    nt = Tt // tm
    gvec, isk, seg, segt = consts
    row = lambda i: (i, 0)
    tab = lambda i: (i % nt, 0)
    const = lambda i: (0, 0)
    kv_shape = jax.ShapeDtypeStruct((M, KV_COLS), f32)
    kv_spec = pl.BlockSpec((tm, KV_COLS), row)
    out_shape = [jax.ShapeDtypeStruct((M, ATT_DIM), bf16), kv_shape, kv_shape, kv_shape,
                 jax.ShapeDtypeStruct((M, LANE), f32)]
    out_specs = [pl.BlockSpec((tm, ATT_DIM), row), kv_spec, kv_spec, kv_spec, pl.BlockSpec((tm, LANE), row)]
    if prompt:
        assert Tt // BLOCK <= LANE - HEAD_DIM, "one-hot block lanes"
        B = M // Tt
        hm = lambda n: jax.ShapeDtypeStruct((B, n, Tt, LANE), bf16)
        hm_spec = lambda n: pl.BlockSpec((1, n, tm, LANE), lambda i: (i // nt, 0, i % nt, 0))
        out_shape[0], out_specs[0] = hm(N_HEADS), hm_spec(N_HEADS)
        out_shape[4] = jax.ShapeDtypeStruct((B, N_KV, Tt, LANE), f32)
        out_specs[4] = hm_spec(N_KV)
        out_shape +=[jax.ShapeDtypeStruct((M // tm, tm // BLOCK, KV_COLS), f32)] + [hm(N_KV)] * 4
        out_specs += [pl.BlockSpec((1, tm // BLOCK, KV_COLS), lambda i: (i, 0, 0))] + [hm_spec(N_KV)] * 4
    return pl.pallas_call(
        functools.partial(_qk_post_body, nt=nt),
        grid=(M // tm,),
        in_specs=[pl.BlockSpec((tm, ATT_PROJ_COLS), row),
                  pl.BlockSpec((tm, LANE), tab), pl.BlockSpec((tm, LANE), tab), pl.BlockSpec((tm, LANE), tab),
                  pl.BlockSpec((1, QKV_COLS), const), pl.BlockSpec((1, QKV_COLS), const),
                  pl.BlockSpec((QKV_COLS, N_NORM_HEADS), const), pl.BlockSpec((N_NORM_HEADS, QKV_COLS), const),
                  pl.BlockSpec((BLOCK, KV_COLS), const)],
        out_specs=out_specs,
        out_shape=out_shape,
        compiler_params=_cparams(("parallel",)),
        name="qk_post",
    )(p_att, cos_t, sa_t, sb_t, gvec, isk, seg, segt, wc)


def _rope_tables(pos):
    half = ROPE_DIM // 2
    inv = ROPE_THETA ** (-jnp.arange(half, dtype=f32) * 2.0 / ROPE_DIM)
    ang = pos.astype(f32)[:, None] * inv[None, :]
    cos, sin = jnp.cos(ang), jnp.sin(ang)
    n = pos.shape[0]
    ones = jnp.ones((n, HEAD_DIM - ROPE_DIM), f32)
    zeros = jnp.zeros((n, HEAD_DIM - half), f32)
    cos_h = jnp.concatenate([cos, cos, ones], axis=1)
    sa_h = jnp.concatenate([-sin, zeros], axis=1)
    sb_h = jnp.concatenate([jnp.zeros((n, half), f32), sin, jnp.zeros((n, HEAD_DIM - ROPE_DIM), f32)], axis=1)
    rep = LANE // HEAD_DIM
    return tuple(jnp.concatenate([t] * rep, axis=1) for t in (cos_h, sa_h, sb_h))


def _qk_consts(q_gain, k_gains):
    ones_v = jnp.ones((K_COLS,), f32)
    gvec = jnp.concatenate([jnp.tile(q_gain, N_HEADS)]
                           + [t for s in range(3) for t in (jnp.tile(k_gains[s], N_KV), ones_v)])
    isk_np = np.concatenate([np.ones(ATT_DIM)] + [np.ones(K_COLS), np.zeros(K_COLS)] * 3).astype(np.float32)
    seg_np = (np.arange(QKV_COLS)[:, None] // HEAD_DIM == np.arange(N_NORM_HEADS)[None, :]).astype(np.float32)
    seg_np = seg_np * isk_np[:, None]
    return (gvec.reshape(1, QKV_COLS), jnp.asarray(isk_np).reshape(1, QKV_COLS),
            jnp.asarray(seg_np, bf16), jnp.asarray(seg_np.T, bf16))


def _compress_weights(w_cmp):
    return jnp.concatenate([jnp.tile(w_cmp[c], (1, N_KV)) for c in range(2)], axis=1)


def _select_blocks(imp, n_pick):
    nb = imp.shape[-1]
    lane = lax.broadcasted_iota(jnp.int32, imp.shape, imp.ndim - 1).astype(f32)
    sel = jnp.zeros(imp.shape, f32)
    for _ in range(min(n_pick, nb)):
        mx = jnp.max(imp, axis=-1, keepdims=True)
        idx = jnp.min(jnp.where(imp == mx, lane, float(nb)), axis=-1, keepdims=True)
        hit = (lane == idx) & (mx >= 0.0)
        sel = jnp.where(hit, 1.0, sel)
        imp = jnp.where(lane == idx, -2.0, imp)
    return sel


def _softmax_parts(s, mask):
    s = jnp.where(mask, s, NEG)
    m = jnp.max(s, axis=-1, keepdims=True)
    e = jnp.where(mask, jnp.exp(s - m), 0.0)
    return m, e


def _nt_dot(a, b):
    return lax.dot_general(a, b, (((1,), (1,)), ((), ())), preferred_element_type=f32)


def _nsa_prompt_body(q_ref, kc_ref, vc_ref, ks_ref, vs_ref, kw_ref, vw_ref, g_ref, place_ref,
                     sb_ref, wb_ref, o_ref, *, tq, T, kc_tile, slab):
    i = pl.program_id(2)
    q0 = i * tq
    R = HPG * tq
    nb = T // BLOCK
    gps = kc_ref.shape[1]
    tn_dims = (((0,), (0,)), ((), ()))
    tpos = q0 + lax.broadcasted_iota(jnp.int32, (1, R), 1) % tq
    blk = lax.broadcasted_iota(jnp.int32, (nb, R), 0)
    vis = (blk + 1) * BLOCK - 1 <= tpos
    blk_q = lax.broadcasted_iota(jnp.int32, (nb, tq), 0)
    cur_q = (q0 + lax.broadcasted_iota(jnp.int32, (1, tq), 1)) // BLOCK
    cand = blk_q < cur_q
    lane = lax.broadcasted_iota(jnp.int32, (1, LANE), 1)
    in_blk_lanes = jnp.where((lane >= HEAD_DIM) & (lane < HEAD_DIM + nb), 1.0, 0.0)

    def compressed_and_selection(gi):
        q = q_ref[0, gi * HPG:(gi + 1) * HPG].reshape(R, LANE)
        s_c = jnp.where(vis, _nt_dot(kc_ref[0, gi], q), NEG)
        e_c = jnp.where(vis, jnp.exp2(s_c - jnp.max(s_c, axis=0, keepdims=True)), 0.0)
        p_c = e_c / jnp.maximum(jnp.sum(e_c, axis=0, keepdims=True), 1e-30)
        o_c = lax.dot_general(p_c.astype(bf16), vc_ref[0, gi], tn_dims, preferred_element_type=f32)
        imp = p_c[:, 0:tq]
        for hh in range(1, HPG):
            imp = imp + p_c[:, hh * tq:(hh + 1) * tq]
        imp = jnp.where(cand, imp, -1.0)
        ahead = jnp.zeros((nb, tq), f32)
        for m in range(nb):
            row_m = imp[m:m + 1, :]
            tie = jnp.where(blk_q > m, 1.0, 0.0)
            ahead = ahead + jnp.where(row_m > imp, 1.0, jnp.where(row_m == imp, tie, 0.0))
        sel = jnp.where(cand, jnp.where(ahead < N_SEL - 1, 1.0, 0.0), jnp.where(blk_q == cur_q, 1.0, 0.0))
        sel_l = lax.dot_general(sel.astype(bf16), place_ref[...], tn_dims, preferred_element_type=f32)
        q_off = ((in_blk_lanes - sel_l) * NEG).astype(bf16)
        return q, q + jnp.concatenate([q_off] * HPG, axis=0), o_c

    def attend(carry, qq, k, v, bias):
        m, l, acc = carry
        s = _nt_dot(qq, k)
        if bias is not None:
            s = (s.reshape(HPG, tq, s.shape[-1]) + bias[None]).reshape(s.shape)
        m_new = jnp.maximum(m, jnp.max(s, axis=-1, keepdims=True))
        alpha = jnp.exp2(m - m_new)
        e = jnp.exp2(s - m_new)
        l = alpha * l + jnp.sum(e, axis=-1, keepdims=True)
        acc = alpha * acc + jnp.dot(e.astype(bf16), v, preferred_element_type=f32)
        return m_new, l, acc

    init = (jnp.full((R, 1), NEG, f32), jnp.zeros((R, 1), f32), jnp.zeros((R, LANE), f32))

    groups = [compressed_and_selection(gi) for gi in range(gps)]

    def sel_step(c, carries):
        k0 = pl.multiple_of(c * kc_tile, kc_tile)
        return tuple(attend(carries[gi], groups[gi][1], ks_ref[0, gi, pl.ds(k0, kc_tile), :],
                            vs_ref[0, gi, pl.ds(k0, kc_tile), :], None) for gi in range(gps))

    n_full = q0 // kc_tile
    carries = lax.fori_loop(0, n_full, sel_step, (init,) * gps)
    kd = pl.multiple_of(n_full * kc_tile, kc_tile)
    w0 = pl.multiple_of(jnp.clip(q0 + tq - slab, 0, T - slab), tq)
    outs = []
    for gi in range(gps):
        q, q_sel, o_c = groups[gi]
        _, l_s, acc_s = attend(carries[gi], q_sel, ks_ref[0, gi, pl.ds(kd, kc_tile), :],
                               vs_ref[0, gi, pl.ds(kd, kc_tile), :], sb_ref[0])
        o_s = acc_s / jnp.maximum(l_s, 1e-30)
        _, l_w, acc_w = attend(init, q, kw_ref[0, gi, pl.ds(w0, slab), :], vw_ref[0, gi, pl.ds(w0, slab), :],
                               wb_ref[0])
        o_w = acc_w / jnp.maximum(l_w, 1e-30)
        gt = g_ref[0, gi]
        for hh in range(HPG):
            rs = slice(hh * tq, (hh + 1) * tq)
            gcol = lambda j: gt[:, hh * 3 + j:hh * 3 + j + 1]
            o_h = gcol(0) * o_c[rs] + gcol(1) * o_s[rs] + gcol(2) * o_w[rs]
            outs.append(o_h[:, :HEAD_DIM])
    o_ref[0] = jnp.concatenate(outs, axis=1).astype(o_ref.dtype)


def _nsa_prompt(q_pad, kc_pad, vc_pad, ks_aug, vs_pad, kw_pad, vw_pad, gates):
    B, _, T, _ = q_pad.shape
    tq = _pick_tile(T, 256, 16)
    kc_tile = _pick_tile(T, 512, tq)
    slab = min(T, WINDOW + tq)
    nb = T // BLOCK
    nq = T // tq
    place = np.zeros((nb, LANE), np.float32)
    place[np.arange(nb), HEAD_DIM + np.arange(nb)] = 1.0
    r = np.arange(tq)[:, None]
    nrel = kc_tile // tq
    sel_bias = np.stack([np.where(np.arange(kc_tile)[None, :] <= rel * tq + r, 0.0, NEG) for rel in range(nrel)])
    n_wb = min(nq, WINDOW // tq + 1) if slab == WINDOW + tq else nq
    win_bias = []
    for i in range(n_wb):
        w0 = min(max(i * tq + tq - slab, 0), T - slab)
        dist = (i * tq + r) - (w0 + np.arange(slab)[None, :])
        win_bias.append(np.where((dist >= 0) & (dist <= WINDOW), 0.0, NEG))
    win_bias = np.stack(win_bias)
    gps = NSA_GROUPS_PER_STEP
    assert N_KV % gps == 0
    kv_spec = pl.BlockSpec((1, gps, T, LANE), lambda b, g, i: (b, g, 0, 0))
    c_spec = pl.BlockSpec((1, gps, nb, LANE), lambda b, g, i: (b, g, 0, 0))
    return pl.pallas_call(
        functools.partial(_nsa_prompt_body, tq=tq, T=T, kc_tile=kc_tile, slab=slab),
        grid=(B, N_KV // gps, nq),
        in_specs=[pl.BlockSpec((1, gps * HPG, tq, LANE), lambda b, g, i: (b, g, i, 0)),
                  c_spec, c_spec, kv_spec, kv_spec, kv_spec, kv_spec,
                  pl.BlockSpec((1, gps, tq, LANE), lambda b, g, i: (b, g, i, 0)),
                  pl.BlockSpec((nb, LANE), lambda b, g, i: (0, 0)),
                  pl.BlockSpec((1, tq, kc_tile), lambda b, g, i: (i % nrel, 0, 0)),
                  pl.BlockSpec((1, tq, slab), lambda b, g, i: (jnp.minimum(i, n_wb - 1), 0, 0))],
        out_specs=pl.BlockSpec((1, tq, gps * HPG * HEAD_DIM), lambda b, g, i: (b, i, g)),
        out_shape=jax.ShapeDtypeStruct((B, T, ATT_DIM), bf16),
        compiler_params=_cparams(("parallel", "parallel", "arbitrary")),
        name="nsa_prompt",
    )(q_pad, kc_pad, vc_pad, ks_aug, vs_pad, kw_pad, vw_pad, gates,
      jnp.asarray(place, bf16), jnp.asarray(sel_bias, f32), jnp.asarray(win_bias, f32))


def _pages_row_minor(pool):
    return jnp.transpose(pool, (0, 2, 3, 4, 1))


def _split_bf16(x):
    hi = x.astype(bf16)
    return hi, (x - hi.astype(f32)).astype(bf16)


def _dot_f32_rhs(w01, x, terms):
    acc = None
    for _ in range(terms):
        part = x.astype(bf16)
        x = x - part.astype(f32)
        d = jnp.dot(w01, part, preferred_element_type=f32)
        acc = d if acc is None else acc + d
    return acc


def _dot_f32_lhs(x, w01, terms):
    acc = None
    for _ in range(terms):
        part = x.astype(bf16)
        x = x - part.astype(f32)
        d = jnp.dot(part, w01, preferred_element_type=f32)
        acc = d if acc is None else acc + d
    return acc


def _compress_pool_body(pt_ref, *refs, pps):
    wt = refs[pps][...]
    seg = refs[pps + 1][...]
    out_ref = refs[pps + 2]
    for p in range(pps):
        page = refs[p][0]
        x = (page * wt).reshape(KV_COLS, page.shape[-1])
        out_ref[0, p] = _nt_dot(seg, x.astype(bf16))


def _compress_pool(pool_t, page_table, w_cmp, pps):
    DB, NP = page_table.shape
    page = pool_t.shape[-1]
    bpp = page // BLOCK
    wt = jnp.tile(jnp.transpose(w_cmp, (0, 2, 1)), (1, 1, bpp))[:, None]
    seg = jnp.asarray(np.arange(page)[None, :] // BLOCK == np.arange(bpp)[:, None], bf16)

    def page_spec(p):
        return pl.BlockSpec((1, 2, N_KV, HEAD_DIM, page), lambda b, s, pt: (pt[b, s * pps + p], 0, 0, 0, 0))

    return pl.pallas_call(
        functools.partial(_compress_pool_body, pps=pps),
        grid_spec=pltpu.PrefetchScalarGridSpec(
            num_scalar_prefetch=1,
            grid=(DB, NP // pps),
            in_specs=[page_spec(p) for p in range(pps)]
            + [pl.BlockSpec((2, 1, HEAD_DIM, page), lambda b, s, pt: (0, 0, 0, 0)),
               pl.BlockSpec((bpp, page), lambda b, s, pt: (0, 0))],
            out_specs=pl.BlockSpec((1, pps, bpp, KV_COLS), lambda b, s, pt: (b, s, 0, 0))),
        out_shape=jax.ShapeDtypeStruct((DB, NP, bpp, KV_COLS), f32),
        compiler_params=_cparams(("parallel", "arbitrary")),
        name="compress_pool",
    )(page_table, *([pool_t] * pps), wt, seg)


def _nsa_sample_body(pt_ref, *refs, pps, past_len, ts):
    pages = refs[:pps]
    (q_ref, g_ref, summ_ref, ns_ref, wb_ref, nw_ref, ex_ref, o_ref,
     selq_ref, m_ref, l_ref, acc_ref, oc_ref) = refs[pps:]
    s_id = pl.program_id(1)
    n_steps = pl.num_programs(1)
    R = HPG * ts
    nbp = summ_ref.shape[1]
    page = pages[0].shape[-1]
    kt = pps * page
    nbs = kt // BLOCK
    row = lax.broadcasted_iota(jnp.int32, (R, 1), 0)
    tpos = past_len + row % ts

    @pl.when(s_id == 0)
    def _():
        blk = lax.broadcasted_iota(jnp.int32, (R, nbp), 1)
        vis = (blk + 1) * BLOCK - 1 <= tpos
        cols = lambda g, c: summ_ref[0, :, c * K_COLS + g * HEAD_DIM:c * K_COLS + (g + 1) * HEAD_DIM].astype(bf16)
        s_c = jnp.concatenate([jnp.where(vis, _nt_dot(q_ref[0, g], cols(g, 0)), NEG) for g in range(N_KV)], axis=0)
        e_c = jnp.exp(s_c - jnp.max(s_c, axis=-1, keepdims=True))
        e_c = jnp.where(jnp.concatenate([vis] * N_KV, axis=0), e_c, 0.0)
        p_c = e_c / jnp.maximum(jnp.sum(e_c, axis=-1, keepdims=True), 1e-30)
        imps = []
        for g in range(N_KV):
            p_g = p_c[g * R:(g + 1) * R]
            oc_ref[g] = jnp.dot(p_g.astype(bf16), cols(g, 1), preferred_element_type=f32)
            imp = p_g[0:ts]
            for hh in range(1, HPG):
                imp = imp + p_g[hh * ts:(hh + 1) * ts]
            imps.append(imp)
        blk_q = lax.broadcasted_iota(jnp.int32, (N_KV * ts, nbp), 1)
        cur_q = (past_len + lax.broadcasted_iota(jnp.int32, (N_KV * ts, 1), 0) % ts) // BLOCK
        imp = jnp.where(blk_q < cur_q, jnp.concatenate(imps, axis=0), -1.0)
        off = ((1.0 - _select_blocks(imp, N_SEL - 1)) * NEG).astype(bf16)
        for g in range(N_KV):
            off_g = jnp.concatenate([off[g * ts:(g + 1) * ts]] * HPG, axis=0)
            for s in range(nbp // nbs):
                selq_ref[s, g] = off_g[:, s * nbs:(s + 1) * nbs]
        m_ref[...] = jnp.full(m_ref.shape, NEG, f32)
        l_ref[...] = jnp.zeros(l_ref.shape, f32)
        acc_ref[...] = jnp.zeros(acc_ref.shape, f32)

    def online_update(s_groups, pv_of_group):
        s = jnp.concatenate(s_groups, axis=0)
        m_old = m_ref[...]
        m_new = jnp.maximum(m_old, jnp.max(s, axis=-1, keepdims=True))
        alpha = jnp.exp(m_old - m_new)
        e = jnp.exp(s - m_new)
        l_ref[...] = alpha * l_ref[...] + jnp.sum(e, axis=-1, keepdims=True)
        e = e.astype(bf16)
        pv = jnp.concatenate([pv_of_group(g, e[g * R:(g + 1) * R]) for g in range(N_KV)], axis=0)
        acc_ref[...] = alpha * acc_ref[...] + pv
        m_ref[...] = m_new

    k0 = s_id * kt
    kpos = k0 + lax.broadcasted_iota(jnp.int32, (1, kt), 1)
    causal = jnp.where(kpos <= tpos, 0.0, NEG)
    page_rows = lambda c, g: jnp.concatenate([pages[p][0, c, g] for p in range(pps)], axis=1).astype(bf16)
    online_update(
        [jnp.dot(q_ref[0, g], page_rows(0, g), preferred_element_type=f32)
         + jnp.dot(selq_ref[s_id, g], ex_ref[...], preferred_element_type=f32) + causal for g in range(N_KV)],
        lambda g, e: _nt_dot(e, page_rows(1, g)))

    @pl.when(s_id == n_steps - 1)
    def _():
        npos = past_len + lax.broadcasted_iota(jnp.int32, (1, ts), 1)
        keep = wb_ref.shape[-1]
        wpos = past_len - keep + lax.broadcasted_iota(jnp.int32, (1, keep), 1)
        d_old = tpos - wpos
        d_new = tpos - npos
        mk_old = (d_old >= 0) & (d_old <= WINDOW) & (wpos >= 0)
        mk_new = (d_new >= 0) & (d_new <= WINDOW)
        ksl_of = lambda g: slice(g * HEAD_DIM, (g + 1) * HEAD_DIM)
        vsl_of = lambda g: slice(K_COLS + g * HEAD_DIM, K_COLS + (g + 1) * HEAD_DIM)
        online_update(
            [jnp.where(npos <= tpos, _nt_dot(q_ref[0, g], ns_ref[0, :, ksl_of(g)].astype(bf16)), NEG)
             for g in range(N_KV)],
            lambda g, e: jnp.dot(e, ns_ref[0, :, vsl_of(g)].astype(bf16), preferred_element_type=f32))
        o_s_all = acc_ref[...] / jnp.maximum(l_ref[...], 1e-30)
        for g in range(N_KV):
            q = q_ref[0, g]
            ksl, vsl = ksl_of(g), vsl_of(g)
            o_s = o_s_all[g * R:(g + 1) * R]
            s_old = jnp.where(mk_old, jnp.dot(q, wb_ref[0, 0, g].astype(bf16), preferred_element_type=f32), NEG)
            s_new = jnp.where(mk_new, _nt_dot(q, nw_ref[0, :, ksl].astype(bf16)), NEG)
            m = jnp.maximum(jnp.max(s_old, axis=-1, keepdims=True), jnp.max(s_new, axis=-1, keepdims=True))
            e_old = jnp.where(mk_old, jnp.exp(s_old - m), 0.0)
            e_new = jnp.where(mk_new, jnp.exp(s_new - m), 0.0)
            den = jnp.sum(e_old, axis=-1, keepdims=True) + jnp.sum(e_new, axis=-1, keepdims=True)
            o_w = (_nt_dot(e_old.astype(bf16), wb_ref[0, 1, g].astype(bf16))
                   + jnp.dot(e_new.astype(bf16), nw_ref[0, :, vsl].astype(bf16), preferred_element_type=f32)
                   ) / jnp.maximum(den, 1e-30)
            gt = g_ref[0, g]
            o_ref[0, g] = gt[:, 0:1] * oc_ref[g] + gt[:, 1:2] * o_s + gt[:, 2:3] * o_w


def _nsa_sample(q_g, gates_g, summ, new_sel, win_t, new_win, pool_t, page_table, pps, past_len):
    DB, NP = page_table.shape
    ts = new_sel.shape[1]
    R = HPG * ts
    nbp = summ.shape[1]
    page = pool_t.shape[-1]
    keep = win_t.shape[-1]
    kt = pps * page
    nbs = kt // BLOCK
    expand = jnp.asarray(np.arange(kt)[None, :] // BLOCK == np.arange(nbs)[:, None], bf16)

    def page_spec(p):
        return pl.BlockSpec((1, 2, N_KV, HEAD_DIM, page), lambda b, s, pt: (pt[b, s * pps + p], 0, 0, 0, 0))

    per_b4 = lambda b, s, pt: (b, 0, 0, 0)
    per_b3 = lambda b, s, pt: (b, 0, 0)
    return pl.pallas_call(
        functools.partial(_nsa_sample_body, pps=pps, past_len=past_len, ts=ts),
        grid_spec=pltpu.PrefetchScalarGridSpec(
            num_scalar_prefetch=1,
            grid=(DB, NP // pps),
            in_specs=[page_spec(p) for p in range(pps)]
            + [pl.BlockSpec((1, N_KV, R, HEAD_DIM), per_b4),
               pl.BlockSpec((1, N_KV, R, 3), per_b4),
               pl.BlockSpec((1, nbp, KV_COLS), per_b3),
               pl.BlockSpec((1, ts, KV_COLS), per_b3),
               pl.BlockSpec((1, 2, N_KV, HEAD_DIM, keep), lambda b, s, pt: (b, 0, 0, 0, 0)),
               pl.BlockSpec((1, ts, KV_COLS), per_b3),
               pl.BlockSpec((nbs, kt), lambda b, s, pt: (0, 0))],
            out_specs=pl.BlockSpec((1, N_KV, R, HEAD_DIM), per_b4),
            scratch_shapes=[pltpu.VMEM((NP // pps, N_KV, R, nbs), bf16),
                            pltpu.VMEM((N_KV * R, 1), f32),
                            pltpu.VMEM((N_KV * R, 1), f32),
                            pltpu.VMEM((N_KV * R, HEAD_DIM), f32),
                            pltpu.VMEM((N_KV, R, HEAD_DIM), f32)]),
        out_shape=jax.ShapeDtypeStruct((DB, N_KV, R, HEAD_DIM), f32),
        compiler_params=_cparams(("parallel", "arbitrary")),
        name="nsa_sample",
    )(page_table, *([pool_t] * pps), q_g, gates_g, summ, new_sel, win_t, new_win, expand)


def _bmm(spec, a, b):
    return jnp.einsum(spec, a.astype(bf16), b.astype(bf16), preferred_element_type=f32)


def _unit_lower_solve(L, rhs, C, bs):
    _mm = functools.partial(_bmm, "hij,hjk->hik")
    ri = lax.broadcasted_iota(jnp.int32, (1, C, C), 1)
    ci = lax.broadcasted_iota(jnp.int32, (1, C, C), 2)
    same = (ri // bs) == (ci // bs)
    eye = jnp.where(ri == ci, 1.0, 0.0)
    D = jnp.where(same, L, 0.0)
    T = eye - D
    P = D
    n = 2
    while n < bs:
        P = _mm(P, P)
        T = T + _mm(T, P)
        n *= 2
    x = _mm(T, rhs)
    nblk = C // bs
    if nblk == 1:
        return x
    Mb = _mm(T, jnp.where(same, 0.0, L))
    factors = []
    Pm = Mb
    n = 2
    while n < nblk:
        Pm = _mm(Pm, Pm)
        factors.append(Pm)
        n *= 2
    for Pm in reversed(factors):
        x = x + _mm(Pm, x)
    return x - _mm(Mb, x)


def _rwkv_body(ps_ref, sp_ref, s0_ref, mu_ref, w0_ref, ww_ref, a0_ref, wa_ref, kk_ref, ka_ref, rk_ref,
               lnw_ref, lnb_ref, seg_ref, segt_ref, y_ref, so_ref, carry_ref, state_ref, *, C, bs):
    c = pl.program_id(1)
    nbat = ps_ref.shape[0]
    H = RW_HEADS

    @pl.when(c == 0)
    def _():
        carry_ref[...] = sp_ref[...]
        state_ref[...] = s0_ref[...].reshape(state_ref.shape)

    def heads(x):
        return [x[:, h * RW_HEAD:(h + 1) * RW_HEAD] for h in range(H)]

    ri = lax.broadcasted_iota(jnp.int32, (C, C), 0)
    ci = lax.broadcasted_iota(jnp.int32, (C, C), 1)
    tril = jnp.where(ci <= ri, 1.0, 0.0).astype(bf16)
    rowi = lax.broadcasted_iota(jnp.int32, (C, 1), 0)

    x1_l, x2_l, kb_l, v_l, rk_l, etot_l = [], [], [], [], [], []
    for n in range(nbat):
        ps = ps_ref[n]
        prev = jnp.where(rowi == 0, carry_ref[n], pltpu.roll(ps, 1, axis=0))
        carry_ref[n] = ps[C - 1:C, :]
        z = ps + (prev - ps) * mu_ref[...]
        r = z[:, 0:RW_DIM]
        k = z[:, RW_DIM:2 * RW_DIM]
        v = z[:, 2 * RW_DIM:3 * RW_DIM]
        xw = z[:, 3 * RW_DIM:3 * RW_DIM + LORA_W]
        xa = z[:, 3 * RW_DIM + LORA_W:]
        u = -(w0_ref[...] + jnp.dot(jnp.tanh(xw).astype(bf16), ww_ref[...], preferred_element_type=f32))
        softplus = jnp.maximum(u, 0.0) + jnp.log(1.0 + jnp.exp(-jnp.abs(u)))
        lw = -jnp.exp(-softplus - 0.5)
        a = jax.nn.sigmoid(a0_ref[...] + jnp.dot(xa.astype(bf16), wa_ref[...], preferred_element_type=f32))
        kk = k * kk_ref[...]
        ss = _dot_f32_lhs(kk * kk, seg_ref[...], 2)
        kk = kk * _dot_f32_lhs(lax.rsqrt(jnp.maximum(ss, 1e-24)), segt_ref[...], 2)
        bb = kk * a
        k2 = k * (1.0 + (a - 1.0) * ka_ref[...])
        G = _dot_f32_rhs(tril, lw, 3)
        g_end = G[C - 1:C, :]
        e_neg = jnp.exp(-G)
        e_end = jnp.exp(g_end - G)
        x1_l.append(heads(jnp.concatenate([kk * jnp.exp(G - lw), r * jnp.exp(G)], axis=0).astype(bf16)))
        x2_l.append(heads(jnp.concatenate([k2 * e_neg, bb * e_neg], axis=0).astype(bf16)))
        kb_l.append(heads(jnp.concatenate([k2 * e_end, -(bb * e_end)], axis=0).astype(bf16)))
        v_l.append(heads(v))
        rk_l.append(heads(r * k2 * rk_ref[...]))
        etot_l.append(heads(jnp.exp(g_end)))

    stack = lambda lst: jnp.stack([t for per_b in lst for t in per_b], axis=0)
    X1, X2, KB = stack(x1_l), stack(x2_l), stack(kb_l)
    V, RK, ETOT = stack(v_l), stack(rk_l), stack(etot_l)

    strict = (ci < ri)[None]
    incl = (ci <= ri)[None]
    S = state_ref[...]
    A = _bmm("hck,hdk->hcd", X1, X2)
    P = _bmm("hck,hvk->hcv", X1, S)
    a_kk = jnp.where(strict, A[:, :C, :C], 0.0)
    a_kb = jnp.where(strict, A[:, :C, C:], 0.0)
    rhs = P[:, :C] + _bmm("hcd,hdv->hcv", a_kk, V)
    sa = _unit_lower_solve(a_kb, rhs, C, bs)
    a_r = jnp.concatenate([jnp.where(incl, A[:, C:, :C], 0.0), jnp.where(incl, -A[:, C:, C:], 0.0)], axis=2)
    vs = jnp.concatenate([V, sa], axis=1)
    y = P[:, C:] + _bmm("hcd,hdv->hcv", a_r, vs)
    state_ref[...] = S * ETOT + _bmm("hcv,hck->hvk", vs, KB)
    mean = jnp.mean(y, axis=-1, keepdims=True)
    var = jnp.mean(jnp.square(y - mean), axis=-1, keepdims=True)
    yn = (y - mean) * lax.rsqrt(var + LN_X_EPS)
    bonus = jnp.sum(RK, axis=-1, keepdims=True) * V
    for n in range(nbat):
        for h in range(H):
            sl = slice(h * RW_HEAD, (h + 1) * RW_HEAD)
            i = n * H + h
            y_ref[n, :, sl] = (yn[i] * lnw_ref[:, sl] + lnb_ref[:, sl] + bonus[i]).astype(y_ref.dtype)

    so_ref[...] = state_ref[...].reshape(so_ref.shape)


def _rwkv(p_shift, shift_prev, s0, rw):
    mu, w0, w_lora_w, a0, w_lora_a, k_k, k_a, r_k, ln_w, ln_b = rw
    B, T, _ = p_shift.shape
    C = _pick_tile(T, 64, 8)
    bs = min(16, C)
    nbat = _pick_tile(B, RWKV_BATCH_PER_STEP, 1)
    seg_np = (np.arange(RW_DIM)[:, None] // RW_HEAD == np.arange(RW_HEADS)[None, :]).astype(np.float32)
    vec = lambda n: pl.BlockSpec((1, n), lambda b, c: (0, 0))
    row = lambda t: t.reshape(1, -1)
    y, s_new = pl.pallas_call(
        functools.partial(_rwkv_body, C=C, bs=bs),
        grid=(B // nbat, T // C),
        in_specs=[pl.BlockSpec((nbat, C, SHIFT_COLS), lambda b, c: (b, c, 0)),
                  pl.BlockSpec((nbat, 1, SHIFT_COLS), lambda b, c: (b, 0, 0)),
                  pl.BlockSpec((nbat, RW_HEADS, RW_HEAD, RW_HEAD), lambda b, c: (b, 0, 0, 0)),
                  vec(SHIFT_COLS), vec(RW_DIM),
                  pl.BlockSpec((LORA_W, RW_DIM), lambda b, c: (0, 0)),
                  vec(RW_DIM),
                  pl.BlockSpec((LORA_A, RW_DIM), lambda b, c: (0, 0)),
                  vec(RW_DIM), vec(RW_DIM), vec(RW_DIM), vec(RW_DIM), vec(RW_DIM),
                  pl.BlockSpec((RW_DIM, RW_HEADS), lambda b, c: (0, 0)),
                  pl.BlockSpec((RW_HEADS, RW_DIM), lambda b, c: (0, 0))],
        out_specs=[pl.BlockSpec((nbat, C, RW_DIM), lambda b, c: (b, c, 0)),
                   pl.BlockSpec((nbat, RW_HEADS, RW_HEAD, RW_HEAD), lambda b, c: (b, 0, 0, 0))],
        out_shape=[jax.ShapeDtypeStruct((B, T, RW_DIM), bf16),
                   jax.ShapeDtypeStruct((B, RW_HEADS, RW_HEAD, RW_HEAD), f32)],
        scratch_shapes=[pltpu.VMEM((nbat, 1, SHIFT_COLS), f32),
                        pltpu.VMEM((nbat * RW_HEADS, RW_HEAD, RW_HEAD), f32)],
        compiler_params=_cparams(("parallel", "arbitrary")),
        name="rwkv",
    )(p_shift, shift_prev.reshape(B, 1, SHIFT_COLS), s0, row(mu), row(w0), w_lora_w.astype(bf16), row(a0),
      w_lora_a.astype(bf16), row(k_k), row(k_a), row(r_k), row(ln_w), row(ln_b),
      jnp.asarray(seg_np, bf16), jnp.asarray(seg_np.T, bf16))
    return y, s_new


def _merge_body(x_ref, oa_ref, yr_ref, ga_ref, gb_ref, wa_ref, wb_ref, wo_ref, h_ref):
    ma = jnp.dot(oa_ref[...], wa_ref[...], preferred_element_type=f32)
    mb = jnp.dot(yr_ref[...], wb_ref[...], preferred_element_type=f32)
    m = jax.nn.sigmoid(ga_ref[...]) * ma + jax.nn.sigmoid(gb_ref[...]) * mb
    h_ref[...] = x_ref[...] + jnp.dot(m.astype(bf16), wo_ref[...], preferred_element_type=f32)


def _merge(x, o_att, y_rw, p_merge, w_a, w_b, w_o):
    M, D = x.shape
    tm = _pick_tile(M, 256, 8)
    row = lambda i: (i, 0)
    const = lambda i: (0, 0)
    return pl.pallas_call(
        _merge_body,
        grid=(M // tm,),
        in_specs=[pl.BlockSpec((tm, D), row),
                  pl.BlockSpec((tm, ATT_DIM), row),
                  pl.BlockSpec((tm, RW_DIM), row),
                  pl.BlockSpec((tm, D), lambda i: (i, 0)),
                  pl.BlockSpec((tm, D), lambda i: (i, 1)),
                  pl.BlockSpec((ATT_DIM, D), const),
                  pl.BlockSpec((RW_DIM, D), const),
                  pl.BlockSpec((D, D), const)],
        out_specs=pl.BlockSpec((tm, D), row),
        out_shape=jax.ShapeDtypeStruct((M, D), f32),
        compiler_params=_cparams(("parallel",)),
        name="merge",
    )(x, o_att, y_rw, p_merge, p_merge, w_a, w_b, w_o)


def _conv_ffn_body(h_ref, g_ref, wug_ref, wuv_ref, cwg_ref, cwv_ref, cbg_ref, cbv_ref, wd_ref,
                   pg_ref, pv_ref, y_ref, tg_ref, tv_ref, hn_ref, cg_ref, cv_ref, *, tm, seq_rows, tail, nsub):
    i = pl.program_id(1)
    j = pl.program_id(2)
    carried = tm <= seq_rows

    @pl.when(j == 0)
    def _():
        h = h_ref[0]
        ms = jnp.mean(h * h, axis=-1, keepdims=True)
        hn_ref[...] = (h * lax.rsqrt(ms + NORM_EPS) * g_ref[...]).astype(bf16)
        y_ref[0] = h

    def taps(cw_ref, cb_ref, u2, u1, u):
        return cb_ref[...] + cw_ref[0:1, :] * u2 + cw_ref[1:2, :] * u1 + cw_ref[2:3, :] * u

    def finish(rows, gate, val):
        act = (gate * jax.nn.sigmoid(gate) * val).astype(bf16)
        y_ref[0, rows, :] += jnp.dot(act, wd_ref[...], preferred_element_type=f32)

    if not carried:
        t_in = lax.broadcasted_iota(jnp.int32, (tm, 1), 0) % seq_rows
        hn = hn_ref[...]

        def conv(u, cw_ref, cb_ref, prev_ref):
            pr = prev_ref[0]
            u1 = jnp.where(t_in == 0, pltpu.roll(pr, tm - 1, axis=0), pltpu.roll(u, 1, axis=0))
            u2 = jnp.where(t_in == 0, pr, jnp.where(t_in == 1, pr, pltpu.roll(u, 2, axis=0)))
            return taps(cw_ref, cb_ref, u2, u1, u)

        ug = jnp.dot(hn, wug_ref[...], preferred_element_type=f32)
        uv = jnp.dot(hn, wuv_ref[...], preferred_element_type=f32)
        tg_ref[0, 0] = ug
        tv_ref[0, 0] = uv
        finish(slice(None), conv(ug, cwg_ref, cbg_ref, pg_ref), conv(uv, cwv_ref, cbv_ref, pv_ref))
        return

    @pl.when(i == 0)
    def _():
        cg_ref[j] = pg_ref[0]
        cv_ref[j] = pv_ref[0]

    ts = tm // nsub
    rowi = lax.broadcasted_iota(jnp.int32, (ts, 1), 0)
    prev_g = (cg_ref[j, 0:1, :], cg_ref[j, 1:2, :])
    prev_v = (cv_ref[j, 0:1, :], cv_ref[j, 1:2, :])

    def conv(u, cw_ref, cb_ref, prev):
        p2, p1 = prev
        u1 = jnp.where(rowi == 0, p1, pltpu.roll(u, 1, axis=0))
        u2 = jnp.where(rowi == 0, p2, jnp.where(rowi == 1, p1, pltpu.roll(u, 2, axis=0)))
        return taps(cw_ref, cb_ref, u2, u1, u), (u[ts - 2:ts - 1, :], u[ts - 1:ts, :])

    for sb in range(nsub):
        rows = slice(sb * ts, (sb + 1) * ts)
        hn = hn_ref[rows, :]
        ug = jnp.dot(hn, wug_ref[...], preferred_element_type=f32)
        uv = jnp.dot(hn, wuv_ref[...], preferred_element_type=f32)
        gate, prev_g = conv(ug, cwg_ref, cbg_ref, prev_g)
        val, prev_v = conv(uv, cwv_ref, cbv_ref, prev_v)
        finish(rows, gate, val)
    cg_ref[j] = jnp.concatenate(prev_g, axis=0)
    cv_ref[j] = jnp.concatenate(prev_v, axis=0)
    tg_ref[0, 0] = ug[ts - tail:ts, :]
    tv_ref[0, 0] = uv[ts - tail:ts, :]


def _conv_ffn(h, conv_prev, norm_g, w_up, conv_w, conv_b, w_down, *, fold):
    B, T, D = h.shape
    dff = w_down.shape[0]
    tf = _pick_tile(dff, 512, LANE)
    nf = dff // tf
    if not fold:
        nb_, tm = B, _pick_tile(T, 1024, 8)
        tail = 8
        hh = h
        prev = conv_prev
        prev_spec_g = pl.BlockSpec((1, CONV_W - 1, tf), lambda b, i, j: (b, 0, j))
        prev_spec_v = pl.BlockSpec((1, CONV_W - 1, tf), lambda b, i, j: (b, 0, nf + j))
    else:
        nb_, tm = 1, B * T
        tail = tm
        hh = h.reshape(1, B * T, D)
        assert T >= CONV_W - 1
        prev = jnp.concatenate([conv_prev, jnp.zeros((B, T - (CONV_W - 1), 2 * dff), f32)],
                               axis=1).reshape(1, B * T, 2 * dff)
        prev_spec_g = pl.BlockSpec((1, tm, tf), lambda b, i, j: (0, 0, j))
        prev_spec_v = pl.BlockSpec((1, tm, tf), lambda b, i, j: (0, 0, nf + j))
    nt = hh.shape[1] // tm
    nsub = FFN_ROW_SUBBLOCKS if (not fold and tm % (8 * FFN_ROW_SUBBLOCKS) == 0) else 1
    body = functools.partial(_conv_ffn_body, tm=tm, seq_rows=T, tail=tail, nsub=nsub)
    cw = conv_w
    cb = conv_b.reshape(1, 2 * dff)
    tail_spec = pl.BlockSpec((1, 1, tail, tf), lambda b, i, j: (b, i, 0, j))
    tail_shape = jax.ShapeDtypeStruct((nb_, nt, tail, dff), f32)
    y, ug, uv = pl.pallas_call(
        body,
        grid=(nb_, nt, nf),
        in_specs=[pl.BlockSpec((1, tm, D), lambda b, i, j: (b, i, 0)),
                  pl.BlockSpec((1, D), lambda b, i, j: (0, 0)),
                  pl.BlockSpec((D, tf), lambda b, i, j: (0, j)),
                  pl.BlockSpec((D, tf), lambda b, i, j: (0, nf + j)),
                  pl.BlockSpec((CONV_W, tf), lambda b, i, j: (0, j)),
                  pl.BlockSpec((CONV_W, tf), lambda b, i, j: (0, nf + j)),
                  pl.BlockSpec((1, tf), lambda b, i, j: (0, j)),
                  pl.BlockSpec((1, tf), lambda b, i, j: (0, nf + j)),
                  pl.BlockSpec((tf, D), lambda b, i, j: (j, 0)),
                  prev_spec_g, prev_spec_v],
        out_specs=[pl.BlockSpec((1, tm, D), lambda b, i, j: (b, i, 0)), tail_spec, tail_spec],
        out_shape=[jax.ShapeDtypeStruct(hh.shape, f32), tail_shape, tail_shape],
        scratch_shapes=[pltpu.VMEM((tm, D), bf16),
                        pltpu.VMEM((nf, CONV_W - 1, tf), f32),
                        pltpu.VMEM((nf, CONV_W - 1, tf), f32)],
        compiler_params=_cparams(("parallel", "arbitrary", "arbitrary")),
        name="conv_ffn",
    )(hh, norm_g.reshape(1, D), w_up, w_up, cw, cw, cb, cb, w_down, prev, prev)
    return y, ug[:, -1], uv[:, -1]


def _split_w_in(w_in):
    o = 0
    parts = []
    for n in (ATT_DIM, KV_COLS, KV_COLS, KV_COLS, 3 * N_HEADS, SHIFT_COLS, 2 * w_in.shape[0]):
        parts.append(w_in[:, o:o + n])
        o += n
    wq, wc, ws, ww, wg, wsh, wm = parts
    wg = jnp.pad(wg, ((0, 0), (0, LANE - 3 * N_HEADS)))
    w_att = jnp.concatenate([wq, wc, ws, ww, wg], axis=1).astype(bf16)
    return w_att, wsh.astype(bf16), wm.astype(bf16)


def _head_major(x, n):
    B, T, _ = x.shape
    return x.reshape(B, T, n, HEAD_DIM).transpose(0, 2, 1, 3)


def _mixer_inputs(x2d, pos_tab, norm_g, w_parts, consts, wc, with_summ):
    w_att, w_sh, w_mg = w_parts
    p_att = _norm_matmul(x2d, norm_g, w_att)
    p_shift = _norm_matmul(x2d, norm_g, w_sh)
    p_merge = _norm_matmul(x2d, norm_g, w_mg)
    post = _qk_post(p_att, pos_tab, consts, wc, with_summ)
    return post, p_shift, p_merge


def kernel(x_prompt, x_sample, cache_kv_cmp, cache_kv_sel, page_table, cache_kv_win, state_wkv, state_shift, state_conv, norm1_g, w_in, q_gain, k_gains, w_cmp, mu_shift, w0, w_lora_w, a0, w_lora_a, k_k, k_a, r_k, ln_x_w, ln_x_b, w_branch_a, w_branch_b, w_out, norm2_g, w_up, conv_w, conv_b, w_down):
    B, T, D = x_prompt.shape
    DB, TS, _ = x_sample.shape
    depth = w_in.shape[0]
    assert depth == 1, "single-layer trunk"
    l = 0
    page = cache_kv_cmp.shape[2]
    n_pages = page_table.shape[1]
    past_len = n_pages * page
    assert past_len % BLOCK == 0 and TS <= BLOCK and page % BLOCK == 0 and T % BLOCK == 0
    dff = w_down.shape[1]

    w_parts = _split_w_in(w_in[l])
    consts = _qk_consts(q_gain[l], k_gains[l])
    wc = _compress_weights(w_cmp[l])
    rw = (mu_shift[l], w0[l], w_lora_w[l], a0[l], w_lora_a[l], k_k[l], k_a[l], r_k[l], ln_x_w[l], ln_x_b[l])
    w_a = w_branch_a[l].astype(bf16)
    w_b = w_branch_b[l].astype(bf16)
    w_o = w_out[l].astype(bf16)
    w_u = w_up[l].astype(bf16)
    w_d = w_down[l].astype(bf16)

    xp = x_prompt.reshape(B * T, D)
    tabs_p = _rope_tables(jnp.arange(T, dtype=jnp.int32))
    ((q_pad, kvc, kvs, kvw, gates, summ, ks_aug, vs_pad, kw_pad, vw_pad),
     p_shift, p_merge) = _mixer_inputs(xp, tabs_p, norm1_g[l], w_parts, consts, wc, True)
    nb = T // BLOCK
    summ = summ.reshape(B, nb, KV_COLS)
    lane_pad = lambda x: jnp.pad(x, ((0, 0), (0, 0), (0, 0), (0, LANE - HEAD_DIM))).astype(bf16)
    kvw3 = kvw.reshape(B, T, KV_COLS)
    o_att = _nsa_prompt(q_pad,
                        lane_pad(_head_major(summ[:, :, :K_COLS], N_KV)),
                        lane_pad(_head_major(summ[:, :, K_COLS:], N_KV)),
                        ks_aug, vs_pad, kw_pad, vw_pad, gates)
    p_shift3 = p_shift.reshape(B, T, SHIFT_COLS)
    y_rw, wkv_p = _rwkv(p_shift3, jnp.zeros((B, SHIFT_COLS), f32),
                        jnp.zeros((B, RW_HEADS, RW_HEAD, RW_HEAD), f32), rw)
    h_p = _merge(xp, o_att.reshape(B * T, ATT_DIM), y_rw.reshape(B * T, RW_DIM), p_merge, w_a, w_b, w_o)
    y_p, ug, uv = _conv_ffn(h_p.reshape(B, T, D), jnp.zeros((B, CONV_W - 1, 2 * dff), f32), norm2_g[l],
                            w_u, conv_w[l], conv_b[l], w_d, fold=False)
    assert T >= CONV_W - 1
    conv_p = jnp.concatenate([ug[:, -(CONV_W - 1):], uv[:, -(CONV_W - 1):]], axis=-1)
    kv_shape_p = (1, B, T, 2, N_KV, HEAD_DIM)
    keep_p = min(WINDOW, T)
    outs_p = (y_p,
              kvc.reshape(kv_shape_p), kvs.reshape(kv_shape_p),
              kvw3[:, T - keep_p:].reshape(1, B, keep_p, 2, N_KV, HEAD_DIM),
              wkv_p[None], p_shift3[:, -1][None], conv_p[None])

    xs = x_sample.reshape(DB * TS, D)
    pos_s = past_len + jnp.arange(TS, dtype=jnp.int32)
    tabs_s = tuple(jnp.tile(t, (DB, 1)) for t in _rope_tables(pos_s))
    (q, kvc_s, kvs_s, kvw_s, gates), p_shift, p_merge = _mixer_inputs(xs, tabs_s, norm1_g[l], w_parts, consts, wc, False)
    pps = _pick_tile(n_pages, PAGES_PER_STEP, 1)
    summ_s = _compress_pool(_pages_row_minor(cache_kv_cmp[l]), page_table, w_cmp[l], pps)
    summ_s = summ_s.reshape(DB, past_len // BLOCK, KV_COLS)
    R = HPG * TS
    q_g = _head_major(q.reshape(DB, TS, ATT_DIM), N_HEADS).reshape(DB, N_KV, R, HEAD_DIM)
    gates_g = (gates[:, :3 * N_HEADS].reshape(DB, TS, N_HEADS, 3).transpose(0, 2, 1, 3)
               .reshape(DB, N_KV, R, 3))
    keep = cache_kv_win.shape[2]
    kvw_s3 = kvw_s.reshape(DB, TS, KV_COLS)
    o_g = _nsa_sample(q_g, gates_g, summ_s, kvs_s.reshape(DB, TS, KV_COLS), _pages_row_minor(cache_kv_win[l]),
                      kvw_s3, _pages_row_minor(cache_kv_sel[l]), page_table, pps, past_len)
    o_att_s = (o_g.reshape(DB, N_HEADS, TS, HEAD_DIM).transpose(0, 2, 1, 3)
               .reshape(DB * TS, ATT_DIM).astype(bf16))
    p_shift3s = p_shift.reshape(DB, TS, SHIFT_COLS)
    y_rw_s, wkv_s = _rwkv(p_shift3s, state_shift[l], state_wkv[l], rw)
    h_s = _merge(xs, o_att_s, y_rw_s.reshape(DB * TS, RW_DIM), p_merge, w_a, w_b, w_o)
    y_s, ug, uv = _conv_ffn(h_s.reshape(DB, TS, D), state_conv[l], norm2_g[l],
                            w_u, conv_w[l], conv_b[l], w_d, fold=True)
    up_s = jnp.concatenate([state_conv[l],
                            jnp.concatenate([ug.reshape(DB, TS, dff), uv.reshape(DB, TS, dff)], axis=-1)], axis=1)
    conv_s = up_s[:, TS:]
    win_s = jnp.concatenate([cache_kv_win[l], kvw_s.reshape(DB, TS, 2, N_KV, HEAD_DIM)], axis=1)[:, TS:]
    kv_shape_s = (1, DB, TS, 2, N_KV, HEAD_DIM)

    return (outs_p[0], y_s.reshape(DB, TS, D),
            outs_p[1], kvc_s.reshape(kv_shape_s),
            outs_p[2], kvs_s.reshape(kv_shape_s),
            outs_p[3], win_s.reshape(1, DB, keep, 2, N_KV, HEAD_DIM),
            outs_p[4], wkv_s[None],
            outs_p[5], p_shift3s[:, -1][None],
            outs_p[6], conv_s[None])
```

```python
import functools

import numpy as np
import jax
import jax.numpy as jnp
from jax import lax
from jax.experimental import pallas as pl
from jax.experimental.pallas import tpu as pltpu

f32 = jnp.float32
bf16 = jnp.bfloat16

N_HEADS = 16
N_KV = 4
HPG = N_HEADS // N_KV
HEAD_DIM = 64
ROPE_DIM = HEAD_DIM // 4
ROPE_THETA = 500000.0
BLOCK = 64
N_SEL = 16
WINDOW = 512
RW_HEADS = 16
RW_HEAD = 64
RW_DIM = RW_HEADS * RW_HEAD
LORA_W = 64
LORA_A = 64
LN_X_EPS = 64e-5
CONV_W = 3
NORM_EPS = 1e-6
ATT_DIM = N_HEADS * HEAD_DIM
KV_COLS = 2 * N_KV * HEAD_DIM
K_COLS = N_KV * HEAD_DIM
SHIFT_COLS = 3 * RW_DIM + LORA_W + LORA_A
NEG = -1e30
LOG2E = 1.4426950408889634

LANE = 128
VMEM_LIMIT = 56 * 1024 * 1024
NORM_MATMUL_VMEM_BUDGET = 50 * 1024 * 1024
QKV_COLS = ATT_DIM + 3 * KV_COLS
ATT_PROJ_COLS = QKV_COLS + LANE
N_NORM_HEADS = QKV_COLS // HEAD_DIM
RWKV_BATCH_PER_STEP = 2
FFN_ROW_SUBBLOCKS = 2
NSA_GROUPS_PER_STEP = 1
PAGES_PER_STEP = 32


def _cparams(sem):
    return pltpu.CompilerParams(dimension_semantics=sem, vmem_limit_bytes=VMEM_LIMIT)


def _pick_tile(n, cap, mult):
    best = None
    for t in range(mult, min(n, cap) + 1, mult):
        if n % t == 0:
            best = t
    assert best is not None, (n, cap, mult)
    return best


def _norm_matmul_body(x_ref, g_ref, w_ref, o_ref, xn_ref):
    @pl.when(pl.program_id(1) == 0)
    def _():
        x = x_ref[...]
        ms = jnp.mean(x * x, axis=-1, keepdims=True)
        xn_ref[...] = (x * lax.rsqrt(ms + NORM_EPS) * g_ref[...]).astype(bf16)

    o_ref[...] = jnp.dot(xn_ref[...], w_ref[...], preferred_element_type=f32)


def _norm_matmul(x, gain, w):
    M, D = x.shape
    N = w.shape[1]
    tm = _pick_tile(M, 512, 8)
    tn = N
    if 2 * tm * D * 4 + tm * D * 2 + 2 * D * tn * 2 + 2 * tm * tn * 4 > NORM_MATMUL_VMEM_BUDGET:
        tm = _pick_tile(M, 1024, 8)
        tn = _pick_tile(N, 1024, 2 * LANE)
    return pl.pallas_call(
        _norm_matmul_body,
        grid=(M // tm, N // tn),
        in_specs=[pl.BlockSpec((tm, D), lambda i, j: (i, 0)),
                  pl.BlockSpec((1, D), lambda i, j: (0, 0)),
                  pl.BlockSpec((D, tn), lambda i, j: (0, j))],
        out_specs=pl.BlockSpec((tm, tn), lambda i, j: (i, j)),
        out_shape=jax.ShapeDtypeStruct((M, N), f32),
        scratch_shapes=[pltpu.VMEM((tm, D), bf16)],
        compiler_params=_cparams(("parallel", "arbitrary")),
        name="norm_matmul",
    )(x, gain.reshape(1, D), w)


def _qk_post_body(p_ref, cos_ref, sa_ref, sb_ref, gvec_ref, isk_ref, seg_ref, segt_ref, wc_ref,
                  q_ref, kvc_ref, kvs_ref, kvw_ref, gate_ref, *prompt_refs, nt):
    y = p_ref[:, :QKV_COLS]
    isk = isk_ref[...] > 0.5
    sq_hi, sq_lo = _split_bf16(y * y)
    ss = (jnp.dot(sq_hi, seg_ref[...], preferred_element_type=f32)
          + jnp.dot(sq_lo, seg_ref[...], preferred_element_type=f32)) * (1.0 / HEAD_DIM)
    rs_hi, rs_lo = _split_bf16(lax.rsqrt(ss + NORM_EPS))
    rb = (jnp.dot(rs_hi, segt_ref[...], preferred_element_type=f32)
          + jnp.dot(rs_lo, segt_ref[...], preferred_element_type=f32))
    yn = jnp.where(isk, y * rb * gvec_ref[...], y)
    reps = QKV_COLS // LANE
    cos = jnp.where(isk, jnp.concatenate([cos_ref[...]] * reps, axis=1), 1.0)
    sa = jnp.where(isk, jnp.concatenate([sa_ref[...]] * reps, axis=1), 0.0)
    sb = jnp.where(isk, jnp.concatenate([sb_ref[...]] * reps, axis=1), 0.0)
    half = ROPE_DIM // 2
    out = (yn * cos + pltpu.roll(yn, QKV_COLS - half, axis=1) * sa + pltpu.roll(yn, half, axis=1) * sb)
    qs = out[:, :ATT_DIM] * (HEAD_DIM ** -0.5 * (LOG2E if prompt_refs else 1.0))
    kvc = out[:, ATT_DIM:ATT_DIM + KV_COLS]
    kvs = out[:, ATT_DIM + KV_COLS:ATT_DIM + 2 * KV_COLS]
    kvw = out[:, ATT_DIM + 2 * KV_COLS:]
    kvc_ref[...] = kvc
    kvs_ref[...] = kvs
    kvw_ref[...] = kvw
    gates = jax.nn.sigmoid(p_ref[:, QKV_COLS:])
    if not prompt_refs:
        gate_ref[...] = gates
        q_ref[...] = qs.astype(bf16)
        return
    for g in range(N_KV):
        gate_ref[0, g] = gates if g == 0 else pltpu.roll(gates, LANE - g * 3 * HPG, axis=1)
    summ_ref, ksa_ref, vsp_ref, kwp_ref, vwp_ref = prompt_refs
    tm = kvc.shape[0]
    blk = kvc.reshape(tm // BLOCK, BLOCK, KV_COLS) * wc_ref[...][None]
    summ_ref[0] = jnp.sum(blk, axis=1)
    zeros = jnp.zeros((tm, LANE - HEAD_DIM), f32)
    t0 = (pl.program_id(0) % nt) * tm
    blk_of_row = (t0 + lax.broadcasted_iota(jnp.int32, zeros.shape, 0)) // BLOCK
    onehot = jnp.where(lax.broadcasted_iota(jnp.int32, zeros.shape, 1) == blk_of_row, 1.0, 0.0)
    hd = lambda x, h: x[:, h * HEAD_DIM:(h + 1) * HEAD_DIM]
    pad = lambda x, tail: jnp.concatenate([x, tail], axis=1).astype(bf16)
    for h in range(N_HEADS):
        q_ref[0, h] = pad(hd(qs, h), zeros)
    for g in range(N_KV):
        ksa_ref[0, g] = pad(hd(kvs, g), onehot)
        vsp_ref[0, g] = pad(hd(kvs, N_KV + g), zeros)
        kwp_ref[0, g] = pad(hd(kvw, g), zeros)
        vwp_ref[0, g] = pad(hd(kvw, N_KV + g), zeros)


def _qk_post(p_att, tabs, consts, wc, prompt):
    M = p_att.shape[0]
    cos_t, sa_t, sb_t = tabs
    Tt = cos_t.shape[0]
    tm = _pick_tile(Tt, 256, BLOCK if prompt else 8)
    nt = Tt // tm
    gvec, isk, seg, segt = consts
    row = lambda i: (i, 0)
    tab = lambda i: (i % nt, 0)
    const = lambda i: (0, 0)
    kv_shape = jax.ShapeDtypeStruct((M, KV_COLS), f32)
    kv_spec = pl.BlockSpec((tm, KV_COLS), row)
    out_shape = [jax.ShapeDtypeStruct((M, ATT_DIM), bf16), kv_shape, kv_shape, kv_shape,
                 jax.ShapeDtypeStruct((M, LANE), f32)]
    out_specs = [pl.BlockSpec((tm, ATT_DIM), row), kv_spec, kv_spec, kv_spec, pl.BlockSpec((tm, LANE), row)]
    if prompt:
        assert Tt // BLOCK <= LANE - HEAD_DIM, "one-hot block lanes"
        B = M // Tt
        hm = lambda n: jax.ShapeDtypeStruct((B, n, Tt, LANE), bf16)
        hm_spec = lambda n: pl.BlockSpec((1, n, tm, LANE), lambda i: (i // nt, 0, i % nt, 0))
        out_shape[0], out_specs[0] = hm(N_HEADS), hm_spec(N_HEADS)
        out_shape[4] = jax.ShapeDtypeStruct((B, N_KV, Tt, LANE), f32)
        out_specs[4] = hm_spec(N_KV)
        out_shape +=[jax.ShapeDtypeStruct((M // tm, tm // BLOCK, KV_COLS), f32)] + [hm(N_KV)] * 4
        out_specs += [pl.BlockSpec((1, tm // BLOCK, KV_COLS), lambda i: (i, 0, 0))] + [hm_spec(N_KV)] * 4
    return pl.pallas_call(
        functools.partial(_qk_post_body, nt=nt),
        grid=(M // tm,),
        in_specs=[pl.BlockSpec((tm, ATT_PROJ_COLS), row),
                  pl.BlockSpec((tm, LANE), tab), pl.BlockSpec((tm, LANE), tab), pl.BlockSpec((tm, LANE), tab),
                  pl.BlockSpec((1, QKV_COLS), const), pl.BlockSpec((1, QKV_COLS), const),
                  pl.BlockSpec((QKV_COLS, N_NORM_HEADS), const), pl.BlockSpec((N_NORM_HEADS, QKV_COLS), const),
                  pl.BlockSpec((BLOCK, KV_COLS), const)],
        out_specs=out_specs,
        out_shape=out_shape,
        compiler_params=_cparams(("parallel",)),
        name="qk_post",
    )(p_att, cos_t, sa_t, sb_t, gvec, isk, seg, segt, wc)


def _rope_tables(pos):
    half = ROPE_DIM // 2
    inv = ROPE_THETA ** (-jnp.arange(half, dtype=f32) * 2.0 / ROPE_DIM)
    ang = pos.astype(f32)[:, None] * inv[None, :]
    cos, sin = jnp.cos(ang), jnp.sin(ang)
    n = pos.shape[0]
    ones = jnp.ones((n, HEAD_DIM - ROPE_DIM), f32)
    zeros = jnp.zeros((n, HEAD_DIM - half), f32)
    cos_h = jnp.concatenate([cos, cos, ones], axis=1)
    sa_h = jnp.concatenate([-sin, zeros], axis=1)
    sb_h = jnp.concatenate([jnp.zeros((n, half), f32), sin, jnp.zeros((n, HEAD_DIM - ROPE_DIM), f32)], axis=1)
    rep = LANE // HEAD_DIM
    return tuple(jnp.concatenate([t] * rep, axis=1) for t in (cos_h, sa_h, sb_h))


def _qk_consts(q_gain, k_gains):
    ones_v = jnp.ones((K_COLS,), f32)
    gvec = jnp.concatenate([jnp.tile(q_gain, N_HEADS)]
                           + [t for s in range(3) for t in (jnp.tile(k_gains[s], N_KV), ones_v)])
    isk_np = np.concatenate([np.ones(ATT_DIM)] + [np.ones(K_COLS), np.zeros(K_COLS)] * 3).astype(np.float32)
    seg_np = (np.arange(QKV_COLS)[:, None] // HEAD_DIM == np.arange(N_NORM_HEADS)[None, :]).astype(np.float32)
    seg_np = seg_np * isk_np[:, None]
    return (gvec.reshape(1, QKV_COLS), jnp.asarray(isk_np).reshape(1, QKV_COLS),
            jnp.asarray(seg_np, bf16), jnp.asarray(seg_np.T, bf16))


def _compress_weights(w_cmp):
    return jnp.concatenate([jnp.tile(w_cmp[c], (1, N_KV)) for c in range(2)], axis=1)


def _select_blocks(imp, n_pick):
    nb = imp.shape[-1]
    lane = lax.broadcasted_iota(jnp.int32, imp.shape, imp.ndim - 1).astype(f32)
    sel = jnp.zeros(imp.shape, f32)
    for _ in range(min(n_pick, nb)):
        mx = jnp.max(imp, axis=-1, keepdims=True)
        idx = jnp.min(jnp.where(imp == mx, lane, float(nb)), axis=-1, keepdims=True)
        hit = (lane == idx) & (mx >= 0.0)
        sel = jnp.where(hit, 1.0, sel)
        imp = jnp.where(lane == idx, -2.0, imp)
    return sel


def _softmax_parts(s, mask):
    s = jnp.where(mask, s, NEG)
    m = jnp.max(s, axis=-1, keepdims=True)
    e = jnp.where(mask, jnp.exp(s - m), 0.0)
    return m, e


def _nt_dot(a, b):
    return lax.dot_general(a, b, (((1,), (1,)), ((), ())), preferred_element_type=f32)


def _nsa_prompt_body(q_ref, kc_ref, vc_ref, ks_ref, vs_ref, kw_ref, vw_ref, g_ref, place_ref,
                     sb_ref, wb_ref, o_ref, *, tq, T, kc_tile, slab):
    i = pl.program_id(2)
    q0 = i * tq
    R = HPG * tq
    nb = T // BLOCK
    gps = kc_ref.shape[1]
    tn_dims = (((0,), (0,)), ((), ()))
    tpos = q0 + lax.broadcasted_iota(jnp.int32, (1, R), 1) % tq
    blk = lax.broadcasted_iota(jnp.int32, (nb, R), 0)
    vis = (blk + 1) * BLOCK - 1 <= tpos
    blk_q = lax.broadcasted_iota(jnp.int32, (nb, tq), 0)
    cur_q = (q0 + lax.broadcasted_iota(jnp.int32, (1, tq), 1)) // BLOCK
    cand = blk_q < cur_q
    lane = lax.broadcasted_iota(jnp.int32, (1, LANE), 1)
    in_blk_lanes = jnp.where((lane >= HEAD_DIM) & (lane < HEAD_DIM + nb), 1.0, 0.0)

    def compressed_and_selection(gi):
        q = q_ref[0, gi * HPG:(gi + 1) * HPG].reshape(R, LANE)
        s_c = jnp.where(vis, _nt_dot(kc_ref[0, gi], q), NEG)
        e_c = jnp.where(vis, jnp.exp2(s_c - jnp.max(s_c, axis=0, keepdims=True)), 0.0)
        p_c = e_c / jnp.maximum(jnp.sum(e_c, axis=0, keepdims=True), 1e-30)
        o_c = lax.dot_general(p_c.astype(bf16), vc_ref[0, gi], tn_dims, preferred_element_type=f32)
        imp = p_c[:, 0:tq]
        for hh in range(1, HPG):
            imp = imp + p_c[:, hh * tq:(hh + 1) * tq]
        imp = jnp.where(cand, imp, -1.0)
        ahead = jnp.zeros((nb, tq), f32)
        for m in range(nb):
            row_m = imp[m:m + 1, :]
            tie = jnp.where(blk_q > m, 1.0, 0.0)
            ahead = ahead + jnp.where(row_m > imp, 1.0, jnp.where(row_m == imp, tie, 0.0))
        sel = jnp.where(cand, jnp.where(ahead < N_SEL - 1, 1.0, 0.0), jnp.where(blk_q == cur_q, 1.0, 0.0))
        sel_l = lax.dot_general(sel.astype(bf16), place_ref[...], tn_dims, preferred_element_type=f32)
        q_off = ((in_blk_lanes - sel_l) * NEG).astype(bf16)
        return q, q + jnp.concatenate([q_off] * HPG, axis=0), o_c

    def attend(carry, qq, k, v, bias):
        m, l, acc = carry
        s = _nt_dot(qq, k)
        if bias is not None:
            s = (s.reshape(HPG, tq, s.shape[-1]) + bias[None]).reshape(s.shape)
        m_new = jnp.maximum(m, jnp.max(s, axis=-1, keepdims=True))
        alpha = jnp.exp2(m - m_new)
        e = jnp.exp2(s - m_new)
        l = alpha * l + jnp.sum(e, axis=-1, keepdims=True)
        acc = alpha * acc + jnp.dot(e.astype(bf16), v, preferred_element_type=f32)
        return m_new, l, acc

    init = (jnp.full((R, 1), NEG, f32), jnp.zeros((R, 1), f32), jnp.zeros((R, LANE), f32))

    groups = [compressed_and_selection(gi) for gi in range(gps)]

    def sel_step(c, carries):
        k0 = pl.multiple_of(c * kc_tile, kc_tile)
        return tuple(attend(carries[gi], groups[gi][1], ks_ref[0, gi, pl.ds(k0, kc_tile), :],
                            vs_ref[0, gi, pl.ds(k0, kc_tile), :], None) for gi in range(gps))

    n_full = q0 // kc_tile
    carries = lax.fori_loop(0, n_full, sel_step, (init,) * gps)
    kd = pl.multiple_of(n_full * kc_tile, kc_tile)
    w0 = pl.multiple_of(jnp.clip(q0 + tq - slab, 0, T - slab), tq)
    outs = []
    for gi in range(gps):
        q, q_sel, o_c = groups[gi]
        _, l_s, acc_s = attend(carries[gi], q_sel, ks_ref[0, gi, pl.ds(kd, kc_tile), :],
                               vs_ref[0, gi, pl.ds(kd, kc_tile), :], sb_ref[0])
        o_s = acc_s / jnp.maximum(l_s, 1e-30)
        _, l_w, acc_w = attend(init, q, kw_ref[0, gi, pl.ds(w0, slab), :], vw_ref[0, gi, pl.ds(w0, slab), :],
                               wb_ref[0])
        o_w = acc_w / jnp.maximum(l_w, 1e-30)
        gt = g_ref[0, gi]
        for hh in range(HPG):
            rs = slice(hh * tq, (hh + 1) * tq)
            gcol = lambda j: gt[:, hh * 3 + j:hh * 3 + j + 1]
            o_h = gcol(0) * o_c[rs] + gcol(1) * o_s[rs] + gcol(2) * o_w[rs]
            outs.append(o_h[:, :HEAD_DIM])
    o_ref[0] = jnp.concatenate(outs, axis=1).astype(o_ref.dtype)


def _nsa_prompt(q_pad, kc_pad, vc_pad, ks_aug, vs_pad, kw_pad, vw_pad, gates):
    B, _, T, _ = q_pad.shape
    tq = _pick_tile(T, 256, 16)
    kc_tile = _pick_tile(T, 512, tq)
    slab = min(T, WINDOW + tq)
    nb = T // BLOCK
    nq = T // tq
    place = np.zeros((nb, LANE), np.float32)
    place[np.arange(nb), HEAD_DIM + np.arange(nb)] = 1.0
    r = np.arange(tq)[:, None]
    nrel = kc_tile // tq
    sel_bias = np.stack([np.where(np.arange(kc_tile)[None, :] <= rel * tq + r, 0.0, NEG) for rel in range(nrel)])
    n_wb = min(nq, WINDOW // tq + 1) if slab == WINDOW + tq else nq
    win_bias = []
    for i in range(n_wb):
        w0 = min(max(i * tq + tq - slab, 0), T - slab)
        dist = (i * tq + r) - (w0 + np.arange(slab)[None, :])
        win_bias.append(np.where((dist >= 0) & (dist <= WINDOW), 0.0, NEG))
    win_bias = np.stack(win_bias)
    gps = NSA_GROUPS_PER_STEP
    assert N_KV % gps == 0
    kv_spec = pl.BlockSpec((1, gps, T, LANE), lambda b, g, i: (b, g, 0, 0))
    c_spec = pl.BlockSpec((1, gps, nb, LANE), lambda b, g, i: (b, g, 0, 0))
    return pl.pallas_call(
        functools.partial(_nsa_prompt_body, tq=tq, T=T, kc_tile=kc_tile, slab=slab),
        grid=(B, N_KV // gps, nq),
        in_specs=[pl.BlockSpec((1, gps * HPG, tq, LANE), lambda b, g, i: (b, g, i, 0)),
                  c_spec, c_spec, kv_spec, kv_spec, kv_spec, kv_spec,
                  pl.BlockSpec((1, gps, tq, LANE), lambda b, g, i: (b, g, i, 0)),
                  pl.BlockSpec((nb, LANE), lambda b, g, i: (0, 0)),
                  pl.BlockSpec((1, tq, kc_tile), lambda b, g, i: (i % nrel, 0, 0)),
                  pl.BlockSpec((1, tq, slab), lambda b, g, i: (jnp.minimum(i, n_wb - 1), 0, 0))],
        out_specs=pl.BlockSpec((1, tq, gps * HPG * HEAD_DIM), lambda b, g, i: (b, i, g)),
        out_shape=jax.ShapeDtypeStruct((B, T, ATT_DIM), bf16),
        compiler_params=_cparams(("parallel", "parallel", "arbitrary")),
        name="nsa_prompt",
    )(q_pad, kc_pad, vc_pad, ks_aug, vs_pad, kw_pad, vw_pad, gates,
      jnp.asarray(place, bf16), jnp.asarray(sel_bias, f32), jnp.asarray(win_bias, f32))


def _pages_row_minor(pool):
    return jnp.transpose(pool, (0, 2, 3, 4, 1))


def _split_bf16(x):
    hi = x.astype(bf16)
    return hi, (x - hi.astype(f32)).astype(bf16)


def _dot_f32_rhs(w01, x, terms):
    acc = None
    for _ in range(terms):
        part = x.astype(bf16)
        x = x - part.astype(f32)
        d = jnp.dot(w01, part, preferred_element_type=f32)
        acc = d if acc is None else acc + d
    return acc


def _dot_f32_lhs(x, w01, terms):
    acc = None
    for _ in range(terms):
        part = x.astype(bf16)
        x = x - part.astype(f32)
        d = jnp.dot(part, w01, preferred_element_type=f32)
        acc = d if acc is None else acc + d
    return acc


def _compress_pool_body(pt_ref, *refs, pps):
    wt = refs[pps][...]
    seg = refs[pps + 1][...]
    out_ref = refs[pps + 2]
    for p in range(pps):
        page = refs[p][0]
        x = (page * wt).reshape(KV_COLS, page.shape[-1])
        out_ref[0, p] = _nt_dot(seg, x.astype(bf16))


def _compress_pool(pool_t, page_table, w_cmp, pps):
    DB, NP = page_table.shape
    page = pool_t.shape[-1]
    bpp = page // BLOCK
    wt = jnp.tile(jnp.transpose(w_cmp, (0, 2, 1)), (1, 1, bpp))[:, None]
    seg = jnp.asarray(np.arange(page)[None, :] // BLOCK == np.arange(bpp)[:, None], bf16)

    def page_spec(p):
        return pl.BlockSpec((1, 2, N_KV, HEAD_DIM, page), lambda b, s, pt: (pt[b, s * pps + p], 0, 0, 0, 0))

    return pl.pallas_call(
        functools.partial(_compress_pool_body, pps=pps),
        grid_spec=pltpu.PrefetchScalarGridSpec(
            num_scalar_prefetch=1,
            grid=(DB, NP // pps),
            in_specs=[page_spec(p) for p in range(pps)]
            + [pl.BlockSpec((2, 1, HEAD_DIM, page), lambda b, s, pt: (0, 0, 0, 0)),
               pl.BlockSpec((bpp, page), lambda b, s, pt: (0, 0))],
            out_specs=pl.BlockSpec((1, pps, bpp, KV_COLS), lambda b, s, pt: (b, s, 0, 0))),
        out_shape=jax.ShapeDtypeStruct((DB, NP, bpp, KV_COLS), f32),
        compiler_params=_cparams(("parallel", "arbitrary")),
        name="compress_pool",
    )(page_table, *([pool_t] * pps), wt, seg)


def _nsa_sample_body(pt_ref, *refs, pps, past_len, ts):
    pages = refs[:pps]
    (q_ref, g_ref, summ_ref, ns_ref, wb_ref, nw_ref, ex_ref, o_ref,
     selq_ref, m_ref, l_ref, acc_ref, oc_ref) = refs[pps:]
    s_id = pl.program_id(1)
    n_steps = pl.num_programs(1)
    R = HPG * ts
    nbp = summ_ref.shape[1]
    page = pages[0].shape[-1]
    kt = pps * page
    nbs = kt // BLOCK
    row = lax.broadcasted_iota(jnp.int32, (R, 1), 0)
    tpos = past_len + row % ts

    @pl.when(s_id == 0)
    def _():
        blk = lax.broadcasted_iota(jnp.int32, (R, nbp), 1)
        vis = (blk + 1) * BLOCK - 1 <= tpos
        cols = lambda g, c: summ_ref[0, :, c * K_COLS + g * HEAD_DIM:c * K_COLS + (g + 1) * HEAD_DIM].astype(bf16)
        s_c = jnp.concatenate([jnp.where(vis, _nt_dot(q_ref[0, g], cols(g, 0)), NEG) for g in range(N_KV)], axis=0)
        e_c = jnp.exp(s_c - jnp.max(s_c, axis=-1, keepdims=True))
        e_c = jnp.where(jnp.concatenate([vis] * N_KV, axis=0), e_c, 0.0)
        p_c = e_c / jnp.maximum(jnp.sum(e_c, axis=-1, keepdims=True), 1e-30)
        imps = []
        for g in range(N_KV):
            p_g = p_c[g * R:(g + 1) * R]
            oc_ref[g] = jnp.dot(p_g.astype(bf16), cols(g, 1), preferred_element_type=f32)
            imp = p_g[0:ts]
            for hh in range(1, HPG):
                imp = imp + p_g[hh * ts:(hh + 1) * ts]
            imps.append(imp)
        blk_q = lax.broadcasted_iota(jnp.int32, (N_KV * ts, nbp), 1)
        cur_q = (past_len + lax.broadcasted_iota(jnp.int32, (N_KV * ts, 1), 0) % ts) // BLOCK
        imp = jnp.where(blk_q < cur_q, jnp.concatenate(imps, axis=0), -1.0)
        off = ((1.0 - _select_blocks(imp, N_SEL - 1)) * NEG).astype(bf16)
        for g in range(N_KV):
            off_g = jnp.concatenate([off[g * ts:(g + 1) * ts]] * HPG, axis=0)
            for s in range(nbp // nbs):
                selq_ref[s, g] = off_g[:, s * nbs:(s + 1) * nbs]
        m_ref[...] = jnp.full(m_ref.shape, NEG, f32)
        l_ref[...] = jnp.zeros(l_ref.shape, f32)
        acc_ref[...] = jnp.zeros(acc_ref.shape, f32)

    def online_update(s_groups, pv_of_group):
        s = jnp.concatenate(s_groups, axis=0)
        m_old = m_ref[...]
        m_new = jnp.maximum(m_old, jnp.max(s, axis=-1, keepdims=True))
        alpha = jnp.exp(m_old - m_new)
        e = jnp.exp(s - m_new)
        l_ref[...] = alpha * l_ref[...] + jnp.sum(e, axis=-1, keepdims=True)
        e = e.astype(bf16)
        pv = jnp.concatenate([pv_of_group(g, e[g * R:(g + 1) * R]) for g in range(N_KV)], axis=0)
        acc_ref[...] = alpha * acc_ref[...] + pv
        m_ref[...] = m_new

    k0 = s_id * kt
    kpos = k0 + lax.broadcasted_iota(jnp.int32, (1, kt), 1)
    causal = jnp.where(kpos <= tpos, 0.0, NEG)
    page_rows = lambda c, g: jnp.concatenate([pages[p][0, c, g] for p in range(pps)], axis=1).astype(bf16)
    online_update(
        [jnp.dot(q_ref[0, g], page_rows(0, g), preferred_element_type=f32)
         + jnp.dot(selq_ref[s_id, g], ex_ref[...], preferred_element_type=f32) + causal for g in range(N_KV)],
        lambda g, e: _nt_dot(e, page_rows(1, g)))

    @pl.when(s_id == n_steps - 1)
    def _():
        npos = past_len + lax.broadcasted_iota(jnp.int32, (1, ts), 1)
        keep = wb_ref.shape[-1]
        wpos = past_len - keep + lax.broadcasted_iota(jnp.int32, (1, keep), 1)
        d_old = tpos - wpos
        d_new = tpos - npos
        mk_old = (d_old >= 0) & (d_old <= WINDOW) & (wpos >= 0)
        mk_new = (d_new >= 0) & (d_new <= WINDOW)
        ksl_of = lambda g: slice(g * HEAD_DIM, (g + 1) * HEAD_DIM)
        vsl_of = lambda g: slice(K_COLS + g * HEAD_DIM, K_COLS + (g + 1) * HEAD_DIM)
        online_update(
            [jnp.where(npos <= tpos, _nt_dot(q_ref[0, g], ns_ref[0, :, ksl_of(g)].astype(bf16)), NEG)
             for g in range(N_KV)],
            lambda g, e: jnp.dot(e, ns_ref[0, :, vsl_of(g)].astype(bf16), preferred_element_type=f32))
        o_s_all = acc_ref[...] / jnp.maximum(l_ref[...], 1e-30)
        for g in range(N_KV):
            q = q_ref[0, g]
            ksl, vsl = ksl_of(g), vsl_of(g)
            o_s = o_s_all[g * R:(g + 1) * R]
            s_old = jnp.where(mk_old, jnp.dot(q, wb_ref[0, 0, g].astype(bf16), preferred_element_type=f32), NEG)
            s_new = jnp.where(mk_new, _nt_dot(q, nw_ref[0, :, ksl].astype(bf16)), NEG)
            m = jnp.maximum(jnp.max(s_old, axis=-1, keepdims=True), jnp.max(s_new, axis=-1, keepdims=True))
            e_old = jnp.where(mk_old, jnp.exp(s_old - m), 0.0)
            e_new = jnp.where(mk_new, jnp.exp(s_new - m), 0.0)
            den = jnp.sum(e_old, axis=-1, keepdims=True) + jnp.sum(e_new, axis=-1, keepdims=True)
            o_w = (_nt_dot(e_old.astype(bf16), wb_ref[0, 1, g].astype(bf16))
                   + jnp.dot(e_new.astype(bf16), nw_ref[0, :, vsl].astype(bf16), preferred_element_type=f32)
                   ) / jnp.maximum(den, 1e-30)
            gt = g_ref[0, g]
            o_ref[0, g] = gt[:, 0:1] * oc_ref[g] + gt[:, 1:2] * o_s + gt[:, 2:3] * o_w


def _nsa_sample(q_g, gates_g, summ, new_sel, win_t, new_win, pool_t, page_table, pps, past_len):
    DB, NP = page_table.shape
    ts = new_sel.shape[1]
    R = HPG * ts
    nbp = summ.shape[1]
    page = pool_t.shape[-1]
    keep = win_t.shape[-1]
    kt = pps * page
    nbs = kt // BLOCK
    expand = jnp.asarray(np.arange(kt)[None, :] // BLOCK == np.arange(nbs)[:, None], bf16)

    def page_spec(p):
        return pl.BlockSpec((1, 2, N_KV, HEAD_DIM, page), lambda b, s, pt: (pt[b, s * pps + p], 0, 0, 0, 0))

    per_b4 = lambda b, s, pt: (b, 0, 0, 0)
    per_b3 = lambda b, s, pt: (b, 0, 0)
    return pl.pallas_call(
        functools.partial(_nsa_sample_body, pps=pps, past_len=past_len, ts=ts),
        grid_spec=pltpu.PrefetchScalarGridSpec(
            num_scalar_prefetch=1,
            grid=(DB, NP // pps),
            in_specs=[page_spec(p) for p in range(pps)]
            + [pl.BlockSpec((1, N_KV, R, HEAD_DIM), per_b4),
               pl.BlockSpec((1, N_KV, R, 3), per_b4),
               pl.BlockSpec((1, nbp, KV_COLS), per_b3),
               pl.BlockSpec((1, ts, KV_COLS), per_b3),
               pl.BlockSpec((1, 2, N_KV, HEAD_DIM, keep), lambda b, s, pt: (b, 0, 0, 0, 0)),
               pl.BlockSpec((1, ts, KV_COLS), per_b3),
               pl.BlockSpec((nbs, kt), lambda b, s, pt: (0, 0))],
            out_specs=pl.BlockSpec((1, N_KV, R, HEAD_DIM), per_b4),
            scratch_shapes=[pltpu.VMEM((NP // pps, N_KV, R, nbs), bf16),
                            pltpu.VMEM((N_KV * R, 1), f32),
                            pltpu.VMEM((N_KV * R, 1), f32),
                            pltpu.VMEM((N_KV * R, HEAD_DIM), f32),
                            pltpu.VMEM((N_KV, R, HEAD_DIM), f32)]),
        out_shape=jax.ShapeDtypeStruct((DB, N_KV, R, HEAD_DIM), f32),
        compiler_params=_cparams(("parallel", "arbitrary")),
        name="nsa_sample",
    )(page_table, *([pool_t] * pps), q_g, gates_g, summ, new_sel, win_t, new_win, expand)


def _bmm(spec, a, b):
    return jnp.einsum(spec, a.astype(bf16), b.astype(bf16), preferred_element_type=f32)


def _unit_lower_solve(L, rhs, C, bs):
    _mm = functools.partial(_bmm, "hij,hjk->hik")
    ri = lax.broadcasted_iota(jnp.int32, (1, C, C), 1)
    ci = lax.broadcasted_iota(jnp.int32, (1, C, C), 2)
    same = (ri // bs) == (ci // bs)
    eye = jnp.where(ri == ci, 1.0, 0.0)
    D = jnp.where(same, L, 0.0)
    T = eye - D
    P = D
    n = 2
    while n < bs:
        P = _mm(P, P)
        T = T + _mm(T, P)
        n *= 2
    x = _mm(T, rhs)
    nblk = C // bs
    if nblk == 1:
        return x
    Mb = _mm(T, jnp.where(same, 0.0, L))
    factors = []
    Pm = Mb
    n = 2
    while n < nblk:
        Pm = _mm(Pm, Pm)
        factors.append(Pm)
        n *= 2
    for Pm in reversed(factors):
        x = x + _mm(Pm, x)
    return x - _mm(Mb, x)


def _rwkv_body(ps_ref, sp_ref, s0_ref, mu_ref, w0_ref, ww_ref, a0_ref, wa_ref, kk_ref, ka_ref, rk_ref,
               lnw_ref, lnb_ref, seg_ref, segt_ref, y_ref, so_ref, carry_ref, state_ref, *, C, bs):
    c = pl.program_id(1)
    nbat = ps_ref.shape[0]
    H = RW_HEADS

    @pl.when(c == 0)
    def _():
        carry_ref[...] = sp_ref[...]
        state_ref[...] = s0_ref[...].reshape(state_ref.shape)

    def heads(x):
        return [x[:, h * RW_HEAD:(h + 1) * RW_HEAD] for h in range(H)]

    ri = lax.broadcasted_iota(jnp.int32, (C, C), 0)
    ci = lax.broadcasted_iota(jnp.int32, (C, C), 1)
    tril = jnp.where(ci <= ri, 1.0, 0.0).astype(bf16)
    rowi = lax.broadcasted_iota(jnp.int32, (C, 1), 0)

    x1_l, x2_l, kb_l, v_l, rk_l, etot_l = [], [], [], [], [], []
    for n in range(nbat):
        ps = ps_ref[n]
        prev = jnp.where(rowi == 0, carry_ref[n], pltpu.roll(ps, 1, axis=0))
        carry_ref[n] = ps[C - 1:C, :]
        z = ps + (prev - ps) * mu_ref[...]
        r = z[:, 0:RW_DIM]
        k = z[:, RW_DIM:2 * RW_DIM]
        v = z[:, 2 * RW_DIM:3 * RW_DIM]
        xw = z[:, 3 * RW_DIM:3 * RW_DIM + LORA_W]
        xa = z[:, 3 * RW_DIM + LORA_W:]
        u = -(w0_ref[...] + jnp.dot(jnp.tanh(xw).astype(bf16), ww_ref[...], preferred_element_type=f32))
        softplus = jnp.maximum(u, 0.0) + jnp.log(1.0 + jnp.exp(-jnp.abs(u)))
        lw = -jnp.exp(-softplus - 0.5)
        a = jax.nn.sigmoid(a0_ref[...] + jnp.dot(xa.astype(bf16), wa_ref[...], preferred_element_type=f32))
        kk = k * kk_ref[...]
        ss = _dot_f32_lhs(kk * kk, seg_ref[...], 2)
        kk = kk * _dot_f32_lhs(lax.rsqrt(jnp.maximum(ss, 1e-24)), segt_ref[...], 2)
        bb = kk * a
        k2 = k * (1.0 + (a - 1.0) * ka_ref[...])
        G = _dot_f32_rhs(tril, lw, 3)
        g_end = G[C - 1:C, :]
        e_neg = jnp.exp(-G)
        e_end = jnp.exp(g_end - G)
        x1_l.append(heads(jnp.concatenate([kk * jnp.exp(G - lw), r * jnp.exp(G)], axis=0).astype(bf16)))
        x2_l.append(heads(jnp.concatenate([k2 * e_neg, bb * e_neg], axis=0).astype(bf16)))
        kb_l.append(heads(jnp.concatenate([k2 * e_end, -(bb * e_end)], axis=0).astype(bf16)))
        v_l.append(heads(v))
        rk_l.append(heads(r * k2 * rk_ref[...]))
        etot_l.append(heads(jnp.exp(g_end)))

    stack = lambda lst: jnp.stack([t for per_b in lst for t in per_b], axis=0)
    X1, X2, KB = stack(x1_l), stack(x2_l), stack(kb_l)
    V, RK, ETOT = stack(v_l), stack(rk_l), stack(etot_l)

    strict = (ci < ri)[None]
    incl = (ci <= ri)[None]
    S = state_ref[...]
    A = _bmm("hck,hdk->hcd", X1, X2)
    P = _bmm("hck,hvk->hcv", X1, S)
    a_kk = jnp.where(strict, A[:, :C, :C], 0.0)
    a_kb = jnp.where(strict, A[:, :C, C:], 0.0)
    rhs = P[:, :C] + _bmm("hcd,hdv->hcv", a_kk, V)
    sa = _unit_lower_solve(a_kb, rhs, C, bs)
    a_r = jnp.concatenate([jnp.where(incl, A[:, C:, :C], 0.0), jnp.where(incl, -A[:, C:, C:], 0.0)], axis=2)
    vs = jnp.concatenate([V, sa], axis=1)
    y = P[:, C:] + _bmm("hcd,hdv->hcv", a_r, vs)
    state_ref[...] = S * ETOT + _bmm("hcv,hck->hvk", vs, KB)
    mean = jnp.mean(y, axis=-1, keepdims=True)
    var = jnp.mean(jnp.square(y - mean), axis=-1, keepdims=True)
    yn = (y - mean) * lax.rsqrt(var + LN_X_EPS)
    bonus = jnp.sum(RK, axis=-1, keepdims=True) * V
    for n in range(nbat):
        for h in range(H):
            sl = slice(h * RW_HEAD, (h + 1) * RW_HEAD)
            i = n * H + h
            y_ref[n, :, sl] = (yn[i] * lnw_ref[:, sl] + lnb_ref[:, sl] + bonus[i]).astype(y_ref.dtype)

    so_ref[...] = state_ref[...].reshape(so_ref.shape)


def _rwkv(p_shift, shift_prev, s0, rw):
    mu, w0, w_lora_w, a0, w_lora_a, k_k, k_a, r_k, ln_w, ln_b = rw
    B, T, _ = p_shift.shape
    C = _pick_tile(T, 64, 8)
    bs = min(16, C)
    nbat = _pick_tile(B, RWKV_BATCH_PER_STEP, 1)
    seg_np = (np.arange(RW_DIM)[:, None] // RW_HEAD == np.arange(RW_HEADS)[None, :]).astype(np.float32)
    vec = lambda n: pl.BlockSpec((1, n), lambda b, c: (0, 0))
    row = lambda t: t.reshape(1, -1)
    y, s_new = pl.pallas_call(
        functools.partial(_rwkv_body, C=C, bs=bs),
        grid=(B // nbat, T // C),
        in_specs=[pl.BlockSpec((nbat, C, SHIFT_COLS), lambda b, c: (b, c, 0)),
                  pl.BlockSpec((nbat, 1, SHIFT_COLS), lambda b, c: (b, 0, 0)),
                  pl.BlockSpec((nbat, RW_HEADS, RW_HEAD, RW_HEAD), lambda b, c: (b, 0, 0, 0)),
                  vec(SHIFT_COLS), vec(RW_DIM),
                  pl.BlockSpec((LORA_W, RW_DIM), lambda b, c: (0, 0)),
                  vec(RW_DIM),
                  pl.BlockSpec((LORA_A, RW_DIM), lambda b, c: (0, 0)),
                  vec(RW_DIM), vec(RW_DIM), vec(RW_DIM), vec(RW_DIM), vec(RW_DIM),
                  pl.BlockSpec((RW_DIM, RW_HEADS), lambda b, c: (0, 0)),
                  pl.BlockSpec((RW_HEADS, RW_DIM), lambda b, c: (0, 0))],
        out_specs=[pl.BlockSpec((nbat, C, RW_DIM), lambda b, c: (b, c, 0)),
                   pl.BlockSpec((nbat, RW_HEADS, RW_HEAD, RW_HEAD), lambda b, c: (b, 0, 0, 0))],
        out_shape=[jax.ShapeDtypeStruct((B, T, RW_DIM), bf16),
                   jax.ShapeDtypeStruct((B, RW_HEADS, RW_HEAD, RW_HEAD), f32)],
        scratch_shapes=[pltpu.VMEM((nbat, 1, SHIFT_COLS), f32),
                        pltpu.VMEM((nbat * RW_HEADS, RW_HEAD, RW_HEAD), f32)],
        compiler_params=_cparams(("parallel", "arbitrary")),
        name="rwkv",
    )(p_shift, shift_prev.reshape(B, 1, SHIFT_COLS), s0, row(mu), row(w0), w_lora_w.astype(bf16), row(a0),
      w_lora_a.astype(bf16), row(k_k), row(k_a), row(r_k), row(ln_w), row(ln_b),
      jnp.asarray(seg_np, bf16), jnp.asarray(seg_np.T, bf16))
    return y, s_new


def _merge_body(x_ref, oa_ref, yr_ref, ga_ref, gb_ref, wa_ref, wb_ref, wo_ref, h_ref):
    ma = jnp.dot(oa_ref[...], wa_ref[...], preferred_element_type=f32)
    mb = jnp.dot(yr_ref[...], wb_ref[...], preferred_element_type=f32)
    m = jax.nn.sigmoid(ga_ref[...]) * ma + jax.nn.sigmoid(gb_ref[...]) * mb
    h_ref[...] = x_ref[...] + jnp.dot(m.astype(bf16), wo_ref[...], preferred_element_type=f32)


def _merge(x, o_att, y_rw, p_merge, w_a, w_b, w_o):
    M, D = x.shape
    tm = _pick_tile(M, 256, 8)
    row = lambda i: (i, 0)
    const = lambda i: (0, 0)
    return pl.pallas_call(
        _merge_body,
        grid=(M // tm,),
        in_specs=[pl.BlockSpec((tm, D), row),
                  pl.BlockSpec((tm, ATT_DIM), row),
                  pl.BlockSpec((tm, RW_DIM), row),
                  pl.BlockSpec((tm, D), lambda i: (i, 0)),
                  pl.BlockSpec((tm, D), lambda i: (i, 1)),
                  pl.BlockSpec((ATT_DIM, D), const),
                  pl.BlockSpec((RW_DIM, D), const),
                  pl.BlockSpec((D, D), const)],
        out_specs=pl.BlockSpec((tm, D), row),
        out_shape=jax.ShapeDtypeStruct((M, D), f32),
        compiler_params=_cparams(("parallel",)),
        name="merge",
    )(x, o_att, y_rw, p_merge, p_merge, w_a, w_b, w_o)


def _conv_ffn_body(h_ref, g_ref, wug_ref, wuv_ref, cwg_ref, cwv_ref, cbg_ref, cbv_ref, wd_ref,
                   pg_ref, pv_ref, y_ref, tg_ref, tv_ref, hn_ref, cg_ref, cv_ref, *, tm, seq_rows, tail, nsub):
    i = pl.program_id(1)
    j = pl.program_id(2)
    carried = tm <= seq_rows

    @pl.when(j == 0)
    def _():
        h = h_ref[0]
        ms = jnp.mean(h * h, axis=-1, keepdims=True)
        hn_ref[...] = (h * lax.rsqrt(ms + NORM_EPS) * g_ref[...]).astype(bf16)
        y_ref[0] = h

    def taps(cw_ref, cb_ref, u2, u1, u):
        return cb_ref[...] + cw_ref[0:1, :] * u2 + cw_ref[1:2, :] * u1 + cw_ref[2:3, :] * u

    def finish(rows, gate, val):
        act = (gate * jax.nn.sigmoid(gate) * val).astype(bf16)
        y_ref[0, rows, :] += jnp.dot(act, wd_ref[...], preferred_element_type=f32)

    if not carried:
        t_in = lax.broadcasted_iota(jnp.int32, (tm, 1), 0) % seq_rows
        hn = hn_ref[...]

        def conv(u, cw_ref, cb_ref, prev_ref):
            pr = prev_ref[0]
            u1 = jnp.where(t_in == 0, pltpu.roll(pr, tm - 1, axis=0), pltpu.roll(u, 1, axis=0))
            u2 = jnp.where(t_in == 0, pr, jnp.where(t_in == 1, pr, pltpu.roll(u, 2, axis=0)))
            return taps(cw_ref, cb_ref, u2, u1, u)

        ug = jnp.dot(hn, wug_ref[...], preferred_element_type=f32)
        uv = jnp.dot(hn, wuv_ref[...], preferred_element_type=f32)
        tg_ref[0, 0] = ug
        tv_ref[0, 0] = uv
        finish(slice(None), conv(ug, cwg_ref, cbg_ref, pg_ref), conv(uv, cwv_ref, cbv_ref, pv_ref))
        return

    @pl.when(i == 0)
    def _():
        cg_ref[j] = pg_ref[0]
        cv_ref[j] = pv_ref[0]

    ts = tm // nsub
    rowi = lax.broadcasted_iota(jnp.int32, (ts, 1), 0)
    prev_g = (cg_ref[j, 0:1, :], cg_ref[j, 1:2, :])
    prev_v = (cv_ref[j, 0:1, :], cv_ref[j, 1:2, :])

    def conv(u, cw_ref, cb_ref, prev):
        p2, p1 = prev
        u1 = jnp.where(rowi == 0, p1, pltpu.roll(u, 1, axis=0))
        u2 = jnp.where(rowi == 0, p2, jnp.where(rowi == 1, p1, pltpu.roll(u, 2, axis=0)))
        return taps(cw_ref, cb_ref, u2, u1, u), (u[ts - 2:ts - 1, :], u[ts - 1:ts, :])

    for sb in range(nsub):
        rows = slice(sb * ts, (sb + 1) * ts)
        hn = hn_ref[rows, :]
        ug = jnp.dot(hn, wug_ref[...], preferred_element_type=f32)
        uv = jnp.dot(hn, wuv_ref[...], preferred_element_type=f32)
        gate, prev_g = conv(ug, cwg_ref, cbg_ref, prev_g)
        val, prev_v = conv(uv, cwv_ref, cbv_ref, prev_v)
        finish(rows, gate, val)
    cg_ref[j] = jnp.concatenate(prev_g, axis=0)
    cv_ref[j] = jnp.concatenate(prev_v, axis=0)
    tg_ref[0, 0] = ug[ts - tail:ts, :]
    tv_ref[0, 0] = uv[ts - tail:ts, :]


def _conv_ffn(h, conv_prev, norm_g, w_up, conv_w, conv_b, w_down, *, fold):
    B, T, D = h.shape
    dff = w_down.shape[0]
    tf = _pick_tile(dff, 512, LANE)
    nf = dff // tf
    if not fold:
        nb_, tm = B, _pick_tile(T, 1024, 8)
        tail = 8
        hh = h
        prev = conv_prev
        prev_spec_g = pl.BlockSpec((1, CONV_W - 1, tf), lambda b, i, j: (b, 0, j))
        prev_spec_v = pl.BlockSpec((1, CONV_W - 1, tf), lambda b, i, j: (b, 0, nf + j))
    else:
        nb_, tm = 1, B * T
        tail = tm
        hh = h.reshape(1, B * T, D)
        assert T >= CONV_W - 1
        prev = jnp.concatenate([conv_prev, jnp.zeros((B, T - (CONV_W - 1), 2 * dff), f32)],
                               axis=1).reshape(1, B * T, 2 * dff)
        prev_spec_g = pl.BlockSpec((1, tm, tf), lambda b, i, j: (0, 0, j))
        prev_spec_v = pl.BlockSpec((1, tm, tf), lambda b, i, j: (0, 0, nf + j))
    nt = hh.shape[1] // tm
    nsub = FFN_ROW_SUBBLOCKS if (not fold and tm % (8 * FFN_ROW_SUBBLOCKS) == 0) else 1
    body = functools.partial(_conv_ffn_body, tm=tm, seq_rows=T, tail=tail, nsub=nsub)
    cw = conv_w
    cb = conv_b.reshape(1, 2 * dff)
    tail_spec = pl.BlockSpec((1, 1, tail, tf), lambda b, i, j: (b, i, 0, j))
    tail_shape = jax.ShapeDtypeStruct((nb_, nt, tail, dff), f32)
    y, ug, uv = pl.pallas_call(
        body,
        grid=(nb_, nt, nf),
        in_specs=[pl.BlockSpec((1, tm, D), lambda b, i, j: (b, i, 0)),
                  pl.BlockSpec((1, D), lambda b, i, j: (0, 0)),
                  pl.BlockSpec((D, tf), lambda b, i, j: (0, j)),
                  pl.BlockSpec((D, tf), lambda b, i, j: (0, nf + j)),
                  pl.BlockSpec((CONV_W, tf), lambda b, i, j: (0, j)),
                  pl.BlockSpec((CONV_W, tf), lambda b, i, j: (0, nf + j)),
                  pl.BlockSpec((1, tf), lambda b, i, j: (0, j)),
                  pl.BlockSpec((1, tf), lambda b, i, j: (0, nf + j)),
                  pl.BlockSpec((tf, D), lambda b, i, j: (j, 0)),
                  prev_spec_g, prev_spec_v],
        out_specs=[pl.BlockSpec((1, tm, D), lambda b, i, j: (b, i, 0)), tail_spec, tail_spec],
        out_shape=[jax.ShapeDtypeStruct(hh.shape, f32), tail_shape, tail_shape],
        scratch_shapes=[pltpu.VMEM((tm, D), bf16),
                        pltpu.VMEM((nf, CONV_W - 1, tf), f32),
                        pltpu.VMEM((nf, CONV_W - 1, tf), f32)],
        compiler_params=_cparams(("parallel", "arbitrary", "arbitrary")),
        name="conv_ffn",
    )(hh, norm_g.reshape(1, D), w_up, w_up, cw, cw, cb, cb, w_down, prev, prev)
    return y, ug[:, -1], uv[:, -1]


def _split_w_in(w_in):
    o = 0
    parts = []
    for n in (ATT_DIM, KV_COLS, KV_COLS, KV_COLS, 3 * N_HEADS, SHIFT_COLS, 2 * w_in.shape[0]):
        parts.append(w_in[:, o:o + n])
        o += n
    wq, wc, ws, ww, wg, wsh, wm = parts
    wg = jnp.pad(wg, ((0, 0), (0, LANE - 3 * N_HEADS)))
    w_att = jnp.concatenate([wq, wc, ws, ww, wg], axis=1).astype(bf16)
    return w_att, wsh.astype(bf16), wm.astype(bf16)


def _head_major(x, n):
    B, T, _ = x.shape
    return x.reshape(B, T, n, HEAD_DIM).transpose(0, 2, 1, 3)


def _mixer_inputs(x2d, pos_tab, norm_g, w_parts, consts, wc, with_summ):
    w_att, w_sh, w_mg = w_parts
    p_att = _norm_matmul(x2d, norm_g, w_att)
    p_shift = _norm_matmul(x2d, norm_g, w_sh)
    p_merge = _norm_matmul(x2d, norm_g, w_mg)
    post = _qk_post(p_att, pos_tab, consts, wc, with_summ)
    return post, p_shift, p_merge


def kernel(x_prompt, x_sample, cache_kv_cmp, cache_kv_sel, page_table, cache_kv_win, state_wkv, state_shift, state_conv, norm1_g, w_in, q_gain, k_gains, w_cmp, mu_shift, w0, w_lora_w, a0, w_lora_a, k_k, k_a, r_k, ln_x_w, ln_x_b, w_branch_a, w_branch_b, w_out, norm2_g, w_up, conv_w, conv_b, w_down):
    B, T, D = x_prompt.shape
    DB, TS, _ = x_sample.shape
    depth = w_in.shape[0]
    assert depth == 1, "single-layer trunk"
    l = 0
    page = cache_kv_cmp.shape[2]
    n_pages = page_table.shape[1]
    past_len = n_pages * page
    assert past_len % BLOCK == 0 and TS <= BLOCK and page % BLOCK == 0 and T % BLOCK == 0
    dff = w_down.shape[1]

    w_parts = _split_w_in(w_in[l])
    consts = _qk_consts(q_gain[l], k_gains[l])
    wc = _compress_weights(w_cmp[l])
    rw = (mu_shift[l], w0[l], w_lora_w[l], a0[l], w_lora_a[l], k_k[l], k_a[l], r_k[l], ln_x_w[l], ln_x_b[l])
    w_a = w_branch_a[l].astype(bf16)
    w_b = w_branch_b[l].astype(bf16)
    w_o = w_out[l].astype(bf16)
    w_u = w_up[l].astype(bf16)
    w_d = w_down[l].astype(bf16)

    xp = x_prompt.reshape(B * T, D)
    tabs_p = _rope_tables(jnp.arange(T, dtype=jnp.int32))
    ((q_pad, kvc, kvs, kvw, gates, summ, ks_aug, vs_pad, kw_pad, vw_pad),
     p_shift, p_merge) = _mixer_inputs(xp, tabs_p, norm1_g[l], w_parts, consts, wc, True)
    nb = T // BLOCK
    summ = summ.reshape(B, nb, KV_COLS)
    lane_pad = lambda x: jnp.pad(x, ((0, 0), (0, 0), (0, 0), (0, LANE - HEAD_DIM))).astype(bf16)
    kvw3 = kvw.reshape(B, T, KV_COLS)
    o_att = _nsa_prompt(q_pad,
                        lane_pad(_head_major(summ[:, :, :K_COLS], N_KV)),
                        lane_pad(_head_major(summ[:, :, K_COLS:], N_KV)),
                        ks_aug, vs_pad, kw_pad, vw_pad, gates)
    p_shift3 = p_shift.reshape(B, T, SHIFT_COLS)
    y_rw, wkv_p = _rwkv(p_shift3, jnp.zeros((B, SHIFT_COLS), f32),
                        jnp.zeros((B, RW_HEADS, RW_HEAD, RW_HEAD), f32), rw)
    h_p = _merge(xp, o_att.reshape(B * T, ATT_DIM), y_rw.reshape(B * T, RW_DIM), p_merge, w_a, w_b, w_o)
    y_p, ug, uv = _conv_ffn(h_p.reshape(B, T, D), jnp.zeros((B, CONV_W - 1, 2 * dff), f32), norm2_g[l],
                            w_u, conv_w[l], conv_b[l], w_d, fold=False)
    assert T >= CONV_W - 1
    conv_p = jnp.concatenate([ug[:, -(CONV_W - 1):], uv[:, -(CONV_W - 1):]], axis=-1)
    kv_shape_p = (1, B, T, 2, N_KV, HEAD_DIM)
    keep_p = min(WINDOW, T)
    outs_p = (y_p,
              kvc.reshape(kv_shape_p), kvs.reshape(kv_shape_p),
              kvw3[:, T - keep_p:].reshape(1, B, keep_p, 2, N_KV, HEAD_DIM),
              wkv_p[None], p_shift3[:, -1][None], conv_p[None])

    xs = x_sample.reshape(DB * TS, D)
    pos_s = past_len + jnp.arange(TS, dtype=jnp.int32)
    tabs_s = tuple(jnp.tile(t, (DB, 1)) for t in _rope_tables(pos_s))
    (q, kvc_s, kvs_s, kvw_s, gates), p_shift, p_merge = _mixer_inputs(xs, tabs_s, norm1_g[l], w_parts, consts, wc, False)
    pps = _pick_tile(n_pages, PAGES_PER_STEP, 1)
    summ_s = _compress_pool(_pages_row_minor(cache_kv_cmp[l]), page_table, w_cmp[l], pps)
    summ_s = summ_s.reshape(DB, past_len // BLOCK, KV_COLS)
    R = HPG * TS
    q_g = _head_major(q.reshape(DB, TS, ATT_DIM), N_HEADS).reshape(DB, N_KV, R, HEAD_DIM)
    gates_g = (gates[:, :3 * N_HEADS].reshape(DB, TS, N_HEADS, 3).transpose(0, 2, 1, 3)
               .reshape(DB, N_KV, R, 3))
    keep = cache_kv_win.shape[2]
    kvw_s3 = kvw_s.reshape(DB, TS, KV_COLS)
    o_g = _nsa_sample(q_g, gates_g, summ_s, kvs_s.reshape(DB, TS, KV_COLS), _pages_row_minor(cache_kv_win[l]),
                      kvw_s3, _pages_row_minor(cache_kv_sel[l]), page_table, pps, past_len)
    o_att_s = (o_g.reshape(DB, N_HEADS, TS, HEAD_DIM).transpose(0, 2, 1, 3)
               .reshape(DB * TS, ATT_DIM).astype(bf16))
    p_shift3s = p_shift.reshape(DB, TS, SHIFT_COLS)
    y_rw_s, wkv_s = _rwkv(p_shift3s, state_shift[l], state_wkv[l], rw)
    h_s = _merge(xs, o_att_s, y_rw_s.reshape(DB * TS, RW_DIM), p_merge, w_a, w_b, w_o)
    y_s, ug, uv = _conv_ffn(h_s.reshape(DB, TS, D), state_conv[l], norm2_g[l],
                            w_u, conv_w[l], conv_b[l], w_d, fold=True)
    up_s = jnp.concatenate([state_conv[l],
                            jnp.concatenate([ug.reshape(DB, TS, dff), uv.reshape(DB, TS, dff)], axis=-1)], axis=1)
    conv_s = up_s[:, TS:]
    win_s = jnp.concatenate([cache_kv_win[l], kvw_s.reshape(DB, TS, 2, N_KV, HEAD_DIM)], axis=1)[:, TS:]
    kv_shape_s = (1, DB, TS, 2, N_KV, HEAD_DIM)

    return (outs_p[0], y_s.reshape(DB, TS, D),
            outs_p[1], kvc_s.reshape(kv_shape_s),
            outs_p[2], kvs_s.reshape(kv_shape_s),
            outs_p[3], win_s.reshape(1, DB, keep, 2, N_KV, HEAD_DIM),
            outs_p[4], wkv_s[None],
            outs_p[5], p_shift3s[:, -1][None],
            outs_p[6], conv_s[None])
```

```python
import functools

import numpy as np
import jax
import jax.numpy as jnp
from jax import lax
from jax.experimental import pallas as pl
from jax.experimental.pallas import tpu as pltpu

f32 = jnp.float32
bf16 = jnp.bfloat16

N_HEADS = 16
N_KV = 4
HPG = N_HEADS // N_KV
HEAD_DIM = 64
ROPE_DIM = HEAD_DIM // 4
ROPE_THETA = 500000.0
BLOCK = 64
N_SEL = 16
WINDOW = 512
RW_HEADS = 16
RW_HEAD = 64
RW_DIM = RW_HEADS * RW_HEAD
LORA_W = 64
LORA_A = 64
LN_X_EPS = 64e-5
CONV_W = 3
NORM_EPS = 1e-6
ATT_DIM = N_HEADS * HEAD_DIM
KV_COLS = 2 * N_KV * HEAD_DIM
K_COLS = N_KV * HEAD_DIM
SHIFT_COLS = 3 * RW_DIM + LORA_W + LORA_A
NEG = -1e30
LOG2E = 1.4426950408889634

LANE = 128
VMEM_LIMIT = 56 * 1024 * 1024
NORM_MATMUL_VMEM_BUDGET = 50 * 1024 * 1024
QKV_COLS = ATT_DIM + 3 * KV_COLS
ATT_PROJ_COLS = QKV_COLS + LANE
N_NORM_HEADS = QKV_COLS // HEAD_DIM
RWKV_BATCH_PER_STEP = 2
FFN_ROW_SUBBLOCKS = 2
NSA_GROUPS_PER_STEP = 1
PAGES_PER_STEP = 32


def _cparams(sem):
    return pltpu.CompilerParams(dimension_semantics=sem, vmem_limit_bytes=VMEM_LIMIT)


def _pick_tile(n, cap, mult):
    best = None
    for t in range(mult, min(n, cap) + 1, mult):
        if n % t == 0:
            best = t
    assert best is not None, (n, cap, mult)
    return best


def _norm_matmul_body(x_ref, g_ref, w_ref, o_ref, xn_ref):
    @pl.when(pl.program_id(1) == 0)
    def _():
        x = x_ref[...]
        ms = jnp.mean(x * x, axis=-1, keepdims=True)
        xn_ref[...] = (x * lax.rsqrt(ms + NORM_EPS) * g_ref[...]).astype(bf16)

    o_ref[...] = jnp.dot(xn_ref[...], w_ref[...], preferred_element_type=f32)


def _norm_matmul(x, gain, w):
    M, D = x.shape
    N = w.shape[1]
    tm = _pick_tile(M, 512, 8)
    tn = N
    w_spec = pl.BlockSpec((D, tn), lambda i, j: (0, j), pipeline_mode=pl.Buffered(1))
    if 2 * tm * D * 4 + tm * D * 2 + D * tn * 2 + 2 * tm * tn * 4 > NORM_MATMUL_VMEM_BUDGET:
        tm = _pick_tile(M, 1024, 8)
        tn = _pick_tile(N, 1024, 2 * LANE)
        w_spec = pl.BlockSpec((D, tn), lambda i, j: (0, j))
    return pl.pallas_call(
        _norm_matmul_body,
        grid=(M // tm, N // tn),
        in_specs=[pl.BlockSpec((tm, D), lambda i, j: (i, 0)),
                  pl.BlockSpec((1, D), lambda i, j: (0, 0)),
                  w_spec],
        out_specs=pl.BlockSpec((tm, tn), lambda i, j: (i, j)),
        out_shape=jax.ShapeDtypeStruct((M, N), f32),
        scratch_shapes=[pltpu.VMEM((tm, D), bf16)],
        compiler_params=_cparams(("parallel", "arbitrary")),
        name="norm_matmul",
    )(x, gain.reshape(1, D), w)


def _qk_post_body(p_ref, cos_ref, sa_ref, sb_ref, gvec_ref, isk_ref, seg_ref, segt_ref, wc_ref,
                  q_ref, kvc_ref, kvs_ref, kvw_ref, gate_ref, *prompt_refs, nt):
    y = p_ref[:, :QKV_COLS]
    isk = isk_ref[...] > 0.5
    sq_hi, sq_lo = _split_bf16(y * y)
    ss = (jnp.dot(sq_hi, seg_ref[...], preferred_element_type=f32)
          + jnp.dot(sq_lo, seg_ref[...], preferred_element_type=f32)) * (1.0 / HEAD_DIM)
    rs_hi, rs_lo = _split_bf16(lax.rsqrt(ss + NORM_EPS))
    rb = (jnp.dot(rs_hi, segt_ref[...], preferred_element_type=f32)
          + jnp.dot(rs_lo, segt_ref[...], preferred_element_type=f32))
    yn = jnp.where(isk, y * rb * gvec_ref[...], y)
    reps = QKV_COLS // LANE
    cos = jnp.where(isk, jnp.concatenate([cos_ref[...]] * reps, axis=1), 1.0)
    sa = jnp.where(isk, jnp.concatenate([sa_ref[...]] * reps, axis=1), 0.0)
    sb = jnp.where(isk, jnp.concatenate([sb_ref[...]] * reps, axis=1), 0.0)
    half = ROPE_DIM // 2
    out = (yn * cos + pltpu.roll(yn, QKV_COLS - half, axis=1) * sa + pltpu.roll(yn, half, axis=1) * sb)
    qs = out[:, :ATT_DIM] * (HEAD_DIM ** -0.5 * (LOG2E if prompt_refs else 1.0))
    kvc = out[:, ATT_DIM:ATT_DIM + KV_COLS]
    kvs = out[:, ATT_DIM + KV_COLS:ATT_DIM + 2 * KV_COLS]
    kvw = out[:, ATT_DIM + 2 * KV_COLS:]
    kvc_ref[...] = kvc
    kvs_ref[...] = kvs
    kvw_ref[...] = kvw
    gates = jax.nn.sigmoid(p_ref[:, QKV_COLS:])
    if not prompt_refs:
        gate_ref[...] = gates
        q_ref[...] = qs.astype(bf16)
        return
    for g in range(N_KV):
        gate_ref[0, g] = gates if g == 0 else pltpu.roll(gates, LANE - g * 3 * HPG, axis=1)
    summ_ref, ksa_ref, vsp_ref, kwp_ref, vwp_ref = prompt_refs
    tm = kvc.shape[0]
    blk = kvc.reshape(tm // BLOCK, BLOCK, KV_COLS) * wc_ref[...][None]
    summ_ref[0] = jnp.sum(blk, axis=1)
    zeros = jnp.zeros((tm, LANE - HEAD_DIM), f32)
    t0 = (pl.program_id(0) % nt) * tm
    blk_of_row = (t0 + lax.broadcasted_iota(jnp.int32, zeros.shape, 0)) // BLOCK
    onehot = jnp.where(lax.broadcasted_iota(jnp.int32, zeros.shape, 1) == blk_of_row, 1.0, 0.0)
    hd = lambda x, h: x[:, h * HEAD_DIM:(h + 1) * HEAD_DIM]
    pad = lambda x, tail: jnp.concatenate([x, tail], axis=1).astype(bf16)
    for h in range(N_HEADS):
        q_ref[0, h] = pad(hd(qs, h), zeros)
    for g in range(N_KV):
        ksa_ref[0, g] = pad(hd(kvs, g), onehot)
        vsp_ref[0, g] = pad(hd(kvs, N_KV + g), zeros)
        kwp_ref[0, g] = pad(hd(kvw, g), zeros)
        vwp_ref[0, g] = pad(hd(kvw, N_KV + g), zeros)


def _qk_post(p_att, tabs, consts, wc, prompt):
    M = p_att.shape[0]
    cos_t, sa_t, sb_t = tabs
    Tt = cos_t.shape[0]
    tm = _pick_tile(Tt, 256, BLOCK if prompt else 8)
    nt = Tt // tm
    gvec, isk, seg, segt = consts
    row = lambda i: (i, 0)
    tab = lambda i: (i % nt, 0)
    const = lambda i: (0, 0)
    kv_shape = jax.ShapeDtypeStruct((M, KV_COLS), f32)
    kv_spec = pl.BlockSpec((tm, KV_COLS), row)
    out_shape = [jax.ShapeDtypeStruct((M, ATT_DIM), bf16), kv_shape, kv_shape, kv_shape,
                 jax.ShapeDtypeStruct((M, LANE), f32)]
    out_specs = [pl.BlockSpec((tm, ATT_DIM), row), kv_spec, kv_spec, kv_spec, pl.BlockSpec((tm, LANE), row)]
    if prompt:
        assert Tt // BLOCK <= LANE - HEAD_DIM, "one-hot block lanes"
        B = M // Tt
        hm = lambda n: jax.ShapeDtypeStruct((B, n, Tt, LANE), bf16)
        hm_spec = lambda n: pl.BlockSpec((1, n, tm, LANE), lambda i: (i // nt, 0, i % nt, 0))
        out_shape[0], out_specs[0] = hm(N_HEADS), hm_spec(N_HEADS)
        out_shape[4] = jax.ShapeDtypeStruct((B, N_KV, Tt, LANE), f32)
        out_specs[4] = hm_spec(N_KV)
        out_shape +=[jax.ShapeDtypeStruct((M // tm, tm // BLOCK, KV_COLS), f32)] + [hm(N_KV)] * 4
        out_specs += [pl.BlockSpec((1, tm // BLOCK, KV_COLS), lambda i: (i, 0, 0))] + [hm_spec(N_KV)] * 4
    return pl.pallas_call(
        functools.partial(_qk_post_body, nt=nt),
        grid=(M // tm,),
        in_specs=[pl.BlockSpec((tm, ATT_PROJ_COLS), row),
                  pl.BlockSpec((tm, LANE), tab), pl.BlockSpec((tm, LANE), tab), pl.BlockSpec((tm, LANE), tab),
                  pl.BlockSpec((1, QKV_COLS), const), pl.BlockSpec((1, QKV_COLS), const),
                  pl.BlockSpec((QKV_COLS, N_NORM_HEADS), const), pl.BlockSpec((N_NORM_HEADS, QKV_COLS), const),
                  pl.BlockSpec((BLOCK, KV_COLS), const)],
        out_specs=out_specs,
        out_shape=out_shape,
        compiler_params=_cparams(("parallel",)),
        name="qk_post",
    )(p_att, cos_t, sa_t, sb_t, gvec, isk, seg, segt, wc)


def _rope_tables(pos):
    half = ROPE_DIM // 2
    inv = ROPE_THETA ** (-jnp.arange(half, dtype=f32) * 2.0 / ROPE_DIM)
    ang = pos.astype(f32)[:, None] * inv[None, :]
    cos, sin = jnp.cos(ang), jnp.sin(ang)
    n = pos.shape[0]
    ones = jnp.ones((n, HEAD_DIM - ROPE_DIM), f32)
    zeros = jnp.zeros((n, HEAD_DIM - half), f32)
    cos_h = jnp.concatenate([cos, cos, ones], axis=1)
    sa_h = jnp.concatenate([-sin, zeros], axis=1)
    sb_h = jnp.concatenate([jnp.zeros((n, half), f32), sin, jnp.zeros((n, HEAD_DIM - ROPE_DIM), f32)], axis=1)
    rep = LANE // HEAD_DIM
    return tuple(jnp.concatenate([t] * rep, axis=1) for t in (cos_h, sa_h, sb_h))


def _qk_consts(q_gain, k_gains):
    ones_v = jnp.ones((K_COLS,), f32)
    gvec = jnp.concatenate([jnp.tile(q_gain, N_HEADS)]
                           + [t for s in range(3) for t in (jnp.tile(k_gains[s], N_KV), ones_v)])
    isk_np = np.concatenate([np.ones(ATT_DIM)] + [np.ones(K_COLS), np.zeros(K_COLS)] * 3).astype(np.float32)
    seg_np = (np.arange(QKV_COLS)[:, None] // HEAD_DIM == np.arange(N_NORM_HEADS)[None, :]).astype(np.float32)
    seg_np = seg_np * isk_np[:, None]
    return (gvec.reshape(1, QKV_COLS), jnp.asarray(isk_np).reshape(1, QKV_COLS),
            jnp.asarray(seg_np, bf16), jnp.asarray(seg_np.T, bf16))


def _compress_weights(w_cmp):
    return jnp.concatenate([jnp.tile(w_cmp[c], (1, N_KV)) for c in range(2)], axis=1)


def _select_blocks(imp, n_pick):
    nb = imp.shape[-1]
    lane = lax.broadcasted_iota(jnp.int32, imp.shape, imp.ndim - 1).astype(f32)
    sel = jnp.zeros(imp.shape, f32)
    for _ in range(min(n_pick, nb)):
        mx = jnp.max(imp, axis=-1, keepdims=True)
        idx = jnp.min(jnp.where(imp == mx, lane, float(nb)), axis=-1, keepdims=True)
        hit = (lane == idx) & (mx >= 0.0)
        sel = jnp.where(hit, 1.0, sel)
        imp = jnp.where(lane == idx, -2.0, imp)
    return sel


def _softmax_parts(s, mask):
    s = jnp.where(mask, s, NEG)
    m = jnp.max(s, axis=-1, keepdims=True)
    e = jnp.where(mask, jnp.exp(s - m), 0.0)
    return m, e


def _nt_dot(a, b):
    return lax.dot_general(a, b, (((1,), (1,)), ((), ())), preferred_element_type=f32)


def _nsa_prompt_body(q_ref, kc_ref, vc_ref, ks_ref, vs_ref, kw_ref, vw_ref, g_ref, place_ref,
                     sb_ref, wb_ref, o_ref, *, tq, T, kc_tile, slab):
    i = pl.program_id(2)
    q0 = i * tq
    R = HPG * tq
    nb = T // BLOCK
    gps = kc_ref.shape[1]
    tn_dims = (((0,), (0,)), ((), ()))
    tpos = q0 + lax.broadcasted_iota(jnp.int32, (1, R), 1) % tq
    blk = lax.broadcasted_iota(jnp.int32, (nb, R), 0)
    vis = (blk + 1) * BLOCK - 1 <= tpos
    blk_q = lax.broadcasted_iota(jnp.int32, (nb, tq), 0)
    cur_q = (q0 + lax.broadcasted_iota(jnp.int32, (1, tq), 1)) // BLOCK
    cand = blk_q < cur_q
    lane = lax.broadcasted_iota(jnp.int32, (1, LANE), 1)
    in_blk_lanes = jnp.where((lane >= HEAD_DIM) & (lane < HEAD_DIM + nb), 1.0, 0.0)

    def compressed_and_selection(gi):
        q = q_ref[0, gi * HPG:(gi + 1) * HPG].reshape(R, LANE)
        s_c = jnp.where(vis, _nt_dot(kc_ref[0, gi], q), NEG)
        e_c = jnp.where(vis, jnp.exp2(s_c - jnp.max(s_c, axis=0, keepdims=True)), 0.0)
        p_c = e_c / jnp.maximum(jnp.sum(e_c, axis=0, keepdims=True), 1e-30)
        o_c = lax.dot_general(p_c.astype(bf16), vc_ref[0, gi], tn_dims, preferred_element_type=f32)
        imp = p_c[:, 0:tq]
        for hh in range(1, HPG):
            imp = imp + p_c[:, hh * tq:(hh + 1) * tq]
        imp = jnp.where(cand, imp, -1.0)
        ahead = jnp.zeros((nb, tq), f32)
        for m in range(nb):
            row_m = imp[m:m + 1, :]
            tie = jnp.where(blk_q > m, 1.0, 0.0)
            ahead = ahead + jnp.where(row_m > imp, 1.0, jnp.where(row_m == imp, tie, 0.0))
        sel = jnp.where(cand, jnp.where(ahead < N_SEL - 1, 1.0, 0.0), jnp.where(blk_q == cur_q, 1.0, 0.0))
        sel_l = lax.dot_general(sel.astype(bf16), place_ref[...], tn_dims, preferred_element_type=f32)
        q_off = ((in_blk_lanes - sel_l) * NEG).astype(bf16)
        return q, q + jnp.concatenate([q_off] * HPG, axis=0), o_c

    def attend(carry, qq, k, v, bias):
        m, l, acc = carry
        s = _nt_dot(qq, k)
        if bias is not None:
            s = (s.reshape(HPG, tq, s.shape[-1]) + bias[None]).reshape(s.shape)
        m_new = jnp.maximum(m, jnp.max(s, axis=-1, keepdims=True))
        alpha = jnp.exp2(m - m_new)
        e = jnp.exp2(s - m_new)
        l = alpha * l + jnp.sum(e, axis=-1, keepdims=True)
        acc = alpha * acc + jnp.dot(e.astype(bf16), v, preferred_element_type=f32)
        return m_new, l, acc

    init = (jnp.full((R, 1), NEG, f32), jnp.zeros((R, 1), f32), jnp.zeros((R, LANE), f32))

    groups = [compressed_and_selection(gi) for gi in range(gps)]

    def sel_step(c, carries):
        k0 = pl.multiple_of(c * kc_tile, kc_tile)
        return tuple(attend(carries[gi], groups[gi][1], ks_ref[0, gi, pl.ds(k0, kc_tile), :],
                            vs_ref[0, gi, pl.ds(k0, kc_tile), :], None) for gi in range(gps))

    n_full = q0 // kc_tile
    carries = lax.fori_loop(0, n_full, sel_step, (init,) * gps)
    kd = pl.multiple_of(n_full * kc_tile, kc_tile)
    w0 = pl.multiple_of(jnp.clip(q0 + tq - slab, 0, T - slab), tq)
    outs = []
    for gi in range(gps):
        q, q_sel, o_c = groups[gi]
        _, l_s, acc_s = attend(carries[gi], q_sel, ks_ref[0, gi, pl.ds(kd, kc_tile), :],
                               vs_ref[0, gi, pl.ds(kd, kc_tile), :], sb_ref[0])
        o_s = acc_s / jnp.maximum(l_s, 1e-30)
        _, l_w, acc_w = attend(init, q, kw_ref[0, gi, pl.ds(w0, slab), :], vw_ref[0, gi, pl.ds(w0, slab), :],
                               wb_ref[0])
        o_w = acc_w / jnp.maximum(l_w, 1e-30)
        gt = g_ref[0, gi]
        for hh in range(HPG):
            rs = slice(hh * tq, (hh + 1) * tq)
            gcol = lambda j: gt[:, hh * 3 + j:hh * 3 + j + 1]
            o_h = gcol(0) * o_c[rs] + gcol(1) * o_s[rs] + gcol(2) * o_w[rs]
            outs.append(o_h[:, :HEAD_DIM])
    o_ref[0] = jnp.concatenate(outs, axis=1).astype(o_ref.dtype)


def _nsa_prompt(q_pad, kc_pad, vc_pad, ks_aug, vs_pad, kw_pad, vw_pad, gates):
    B, _, T, _ = q_pad.shape
    tq = _pick_tile(T, 256, 16)
    kc_tile = _pick_tile(T, 512, tq)
    slab = min(T, WINDOW + tq)
    nb = T // BLOCK
    nq = T // tq
    place = np.zeros((nb, LANE), np.float32)
    place[np.arange(nb), HEAD_DIM + np.arange(nb)] = 1.0
    r = np.arange(tq)[:, None]
    nrel = kc_tile // tq
    sel_bias = np.stack([np.where(np.arange(kc_tile)[None, :] <= rel * tq + r, 0.0, NEG) for rel in range(nrel)])
    n_wb = min(nq, WINDOW // tq + 1) if slab == WINDOW + tq else nq
    win_bias = []
    for i in range(n_wb):
        w0 = min(max(i * tq + tq - slab, 0), T - slab)
        dist = (i * tq + r) - (w0 + np.arange(slab)[None, :])
        win_bias.append(np.where((dist >= 0) & (dist <= WINDOW), 0.0, NEG))
    win_bias = np.stack(win_bias)
    gps = NSA_GROUPS_PER_STEP
    assert N_KV % gps == 0
    kv_spec = pl.BlockSpec((1, gps, T, LANE), lambda b, g, i: (b, g, 0, 0))
    c_spec = pl.BlockSpec((1, gps, nb, LANE), lambda b, g, i: (b, g, 0, 0))
    return pl.pallas_call(
        functools.partial(_nsa_prompt_body, tq=tq, T=T, kc_tile=kc_tile, slab=slab),
        grid=(B, N_KV // gps, nq),
        in_specs=[pl.BlockSpec((1, gps * HPG, tq, LANE), lambda b, g, i: (b, g, i, 0)),
                  c_spec, c_spec, kv_spec, kv_spec, kv_spec, kv_spec,
                  pl.BlockSpec((1, gps, tq, LANE), lambda b, g, i: (b, g, i, 0)),
                  pl.BlockSpec((nb, LANE), lambda b, g, i: (0, 0)),
                  pl.BlockSpec((1, tq, kc_tile), lambda b, g, i: (i % nrel, 0, 0)),
                  pl.BlockSpec((1, tq, slab), lambda b, g, i: (jnp.minimum(i, n_wb - 1), 0, 0))],
        out_specs=pl.BlockSpec((1, tq, gps * HPG * HEAD_DIM), lambda b, g, i: (b, i, g)),
        out_shape=jax.ShapeDtypeStruct((B, T, ATT_DIM), bf16),
        compiler_params=_cparams(("parallel", "parallel", "arbitrary")),
        name="nsa_prompt",
    )(q_pad, kc_pad, vc_pad, ks_aug, vs_pad, kw_pad, vw_pad, gates,
      jnp.asarray(place, bf16), jnp.asarray(sel_bias, f32), jnp.asarray(win_bias, f32))


def _pages_row_minor(pool):
    return jnp.transpose(pool, (0, 2, 3, 4, 1))


def _split_bf16(x):
    hi = x.astype(bf16)
    return hi, (x - hi.astype(f32)).astype(bf16)


def _dot_f32_rhs(w01, x, terms):
    acc = None
    for _ in range(terms):
        part = x.astype(bf16)
        x = x - part.astype(f32)
        d = jnp.dot(w01, part, preferred_element_type=f32)
        acc = d if acc is None else acc + d
    return acc


def _dot_f32_lhs(x, w01, terms):
    acc = None
    for _ in range(terms):
        part = x.astype(bf16)
        x = x - part.astype(f32)
        d = jnp.dot(part, w01, preferred_element_type=f32)
        acc = d if acc is None else acc + d
    return acc


def _compress_pool_body(pt_ref, *refs, pps):
    wt = refs[pps][...]
    seg = refs[pps + 1][...]
    out_ref = refs[pps + 2]
    for p in range(pps):
        page = refs[p][0]
        x = (page * wt).reshape(KV_COLS, page.shape[-1])
        out_ref[0, p] = _nt_dot(seg, x.astype(bf16))


def _compress_pool(pool_t, page_table, w_cmp, pps):
    DB, NP = page_table.shape
    page = pool_t.shape[-1]
    bpp = page // BLOCK
    wt = jnp.tile(jnp.transpose(w_cmp, (0, 2, 1)), (1, 1, bpp))[:, None]
    seg = jnp.asarray(np.arange(page)[None, :] // BLOCK == np.arange(bpp)[:, None], bf16)

    def page_spec(p):
        return pl.BlockSpec((1, 2, N_KV, HEAD_DIM, page), lambda b, s, pt: (pt[b, s * pps + p], 0, 0, 0, 0))

    return pl.pallas_call(
        functools.partial(_compress_pool_body, pps=pps),
        grid_spec=pltpu.PrefetchScalarGridSpec(
            num_scalar_prefetch=1,
            grid=(DB, NP // pps),
            in_specs=[page_spec(p) for p in range(pps)]
            + [pl.BlockSpec((2, 1, HEAD_DIM, page), lambda b, s, pt: (0, 0, 0, 0)),
               pl.BlockSpec((bpp, page), lambda b, s, pt: (0, 0))],
            out_specs=pl.BlockSpec((1, pps, bpp, KV_COLS), lambda b, s, pt: (b, s, 0, 0))),
        out_shape=jax.ShapeDtypeStruct((DB, NP, bpp, KV_COLS), f32),
        compiler_params=_cparams(("parallel", "arbitrary")),
        name="compress_pool",
    )(page_table, *([pool_t] * pps), wt, seg)


def _nsa_sample_body(pt_ref, *refs, pps, past_len, ts):
    pages = refs[:pps]
    (q_ref, g_ref, summ_ref, ns_ref, wb_ref, nw_ref, ex_ref, o_ref,
     selq_ref, m_ref, l_ref, acc_ref, oc_ref) = refs[pps:]
    s_id = pl.program_id(1)
    n_steps = pl.num_programs(1)
    R = HPG * ts
    nbp = summ_ref.shape[1]
    page = pages[0].shape[-1]
    kt = pps * page
    nbs = kt // BLOCK
    row = lax.broadcasted_iota(jnp.int32, (R, 1), 0)
    tpos = past_len + row % ts

    @pl.when(s_id == 0)
    def _():
        blk = lax.broadcasted_iota(jnp.int32, (R, nbp), 1)
        vis = (blk + 1) * BLOCK - 1 <= tpos
        cols = lambda g, c: summ_ref[0, :, c * K_COLS + g * HEAD_DIM:c * K_COLS + (g + 1) * HEAD_DIM].astype(bf16)
        s_c = jnp.concatenate([jnp.where(vis, _nt_dot(q_ref[0, g], cols(g, 0)), NEG) for g in range(N_KV)], axis=0)
        e_c = jnp.exp(s_c - jnp.max(s_c, axis=-1, keepdims=True))
        e_c = jnp.where(jnp.concatenate([vis] * N_KV, axis=0), e_c, 0.0)
        p_c = e_c / jnp.maximum(jnp.sum(e_c, axis=-1, keepdims=True), 1e-30)
        imps = []
        for g in range(N_KV):
            p_g = p_c[g * R:(g + 1) * R]
            oc_ref[g] = jnp.dot(p_g.astype(bf16), cols(g, 1), preferred_element_type=f32)
            imp = p_g[0:ts]
            for hh in range(1, HPG):
                imp = imp + p_g[hh * ts:(hh + 1) * ts]
            imps.append(imp)
        blk_q = lax.broadcasted_iota(jnp.int32, (N_KV * ts, nbp), 1)
        cur_q = (past_len + lax.broadcasted_iota(jnp.int32, (N_KV * ts, 1), 0) % ts) // BLOCK
        imp = jnp.where(blk_q < cur_q, jnp.concatenate(imps, axis=0), -1.0)
        off = ((1.0 - _select_blocks(imp, N_SEL - 1)) * NEG).astype(bf16)
        for g in range(N_KV):
            off_g = jnp.concatenate([off[g * ts:(g + 1) * ts]] * HPG, axis=0)
            for s in range(nbp // nbs):
                selq_ref[s, g] = off_g[:, s * nbs:(s + 1) * nbs]
        m_ref[...] = jnp.full(m_ref.shape, NEG, f32)
        l_ref[...] = jnp.zeros(l_ref.shape, f32)
        acc_ref[...] = jnp.zeros(acc_ref.shape, f32)

    def online_update(s_groups, pv_of_group):
        s = jnp.concatenate(s_groups, axis=0)
        m_old = m_ref[...]
        m_new = jnp.maximum(m_old, jnp.max(s, axis=-1, keepdims=True))
        alpha = jnp.exp(m_old - m_new)
        e = jnp.exp(s - m_new)
        l_ref[...] = alpha * l_ref[...] + jnp.sum(e, axis=-1, keepdims=True)
        e = e.astype(bf16)
        pv = jnp.concatenate([pv_of_group(g, e[g * R:(g + 1) * R]) for g in range(N_KV)], axis=0)
        acc_ref[...] = alpha * acc_ref[...] + pv
        m_ref[...] = m_new

    k0 = s_id * kt
    kpos = k0 + lax.broadcasted_iota(jnp.int32, (1, kt), 1)
    causal = jnp.where(kpos <= tpos, 0.0, NEG)
    page_rows = lambda c, g: jnp.concatenate([pages[p][0, c, g] for p in range(pps)], axis=1).astype(bf16)
    online_update(
        [jnp.dot(q_ref[0, g], page_rows(0, g), preferred_element_type=f32)
         + jnp.dot(selq_ref[s_id, g], ex_ref[...], preferred_element_type=f32) + causal for g in range(N_KV)],
        lambda g, e: _nt_dot(e, page_rows(1, g)))

    @pl.when(s_id == n_steps - 1)
    def _():
        npos = past_len + lax.broadcasted_iota(jnp.int32, (1, ts), 1)
        keep = wb_ref.shape[-1]
        wpos = past_len - keep + lax.broadcasted_iota(jnp.int32, (1, keep), 1)
        d_old = tpos - wpos
        d_new = tpos - npos
        mk_old = (d_old >= 0) & (d_old <= WINDOW) & (wpos >= 0)
        mk_new = (d_new >= 0) & (d_new <= WINDOW)
        ksl_of = lambda g: slice(g * HEAD_DIM, (g + 1) * HEAD_DIM)
        vsl_of = lambda g: slice(K_COLS + g * HEAD_DIM, K_COLS + (g + 1) * HEAD_DIM)
        online_update(
            [jnp.where(npos <= tpos, _nt_dot(q_ref[0, g], ns_ref[0, :, ksl_of(g)].astype(bf16)), NEG)
             for g in range(N_KV)],
            lambda g, e: jnp.dot(e, ns_ref[0, :, vsl_of(g)].astype(bf16), preferred_element_type=f32))
        o_s_all = acc_ref[...] / jnp.maximum(l_ref[...], 1e-30)
        for g in range(N_KV):
            q = q_ref[0, g]
            ksl, vsl = ksl_of(g), vsl_of(g)
            o_s = o_s_all[g * R:(g + 1) * R]
            s_old = jnp.where(mk_old, jnp.dot(q, wb_ref[0, 0, g].astype(bf16), preferred_element_type=f32), NEG)
            s_new = jnp.where(mk_new, _nt_dot(q, nw_ref[0, :, ksl].astype(bf16)), NEG)
            m = jnp.maximum(jnp.max(s_old, axis=-1, keepdims=True), jnp.max(s_new, axis=-1, keepdims=True))
            e_old = jnp.where(mk_old, jnp.exp(s_old - m), 0.0)
            e_new = jnp.where(mk_new, jnp.exp(s_new - m), 0.0)
            den = jnp.sum(e_old, axis=-1, keepdims=True) + jnp.sum(e_new, axis=-1, keepdims=True)
            o_w = (_nt_dot(e_old.astype(bf16), wb_ref[0, 1, g].astype(bf16))
                   + jnp.dot(e_new.astype(bf16), nw_ref[0, :, vsl].astype(bf16), preferred_element_type=f32)
                   ) / jnp.maximum(den, 1e-30)
            gt = g_ref[0, g]
            o_ref[0, g] = gt[:, 0:1] * oc_ref[g] + gt[:, 1:2] * o_s + gt[:, 2:3] * o_w


def _nsa_sample(q_g, gates_g, summ, new_sel, win_t, new_win, pool_t, page_table, pps, past_len):
    DB, NP = page_table.shape
    ts = new_sel.shape[1]
    R = HPG * ts
    nbp = summ.shape[1]
    page = pool_t.shape[-1]
    keep = win_t.shape[-1]
    kt = pps * page
    nbs = kt // BLOCK
    expand = jnp.asarray(np.arange(kt)[None, :] // BLOCK == np.arange(nbs)[:, None], bf16)

    def page_spec(p):
        return pl.BlockSpec((1, 2, N_KV, HEAD_DIM, page), lambda b, s, pt: (pt[b, s * pps + p], 0, 0, 0, 0))

    per_b4 = lambda b, s, pt: (b, 0, 0, 0)
    per_b3 = lambda b, s, pt: (b, 0, 0)
    return pl.pallas_call(
        functools.partial(_nsa_sample_body, pps=pps, past_len=past_len, ts=ts),
        grid_spec=pltpu.PrefetchScalarGridSpec(
            num_scalar_prefetch=1,
            grid=(DB, NP // pps),
            in_specs=[page_spec(p) for p in range(pps)]
            + [pl.BlockSpec((1, N_KV, R, HEAD_DIM), per_b4),
               pl.BlockSpec((1, N_KV, R, 3), per_b4),
               pl.BlockSpec((1, nbp, KV_COLS), per_b3),
               pl.BlockSpec((1, ts, KV_COLS), per_b3),
               pl.BlockSpec((1, 2, N_KV, HEAD_DIM, keep), lambda b, s, pt: (b, 0, 0, 0, 0)),
               pl.BlockSpec((1, ts, KV_COLS), per_b3),
               pl.BlockSpec((nbs, kt), lambda b, s, pt: (0, 0))],
            out_specs=pl.BlockSpec((1, N_KV, R, HEAD_DIM), per_b4),
            scratch_shapes=[pltpu.VMEM((NP // pps, N_KV, R, nbs), bf16),
                            pltpu.VMEM((N_KV * R, 1), f32),
                            pltpu.VMEM((N_KV * R, 1), f32),
                            pltpu.VMEM((N_KV * R, HEAD_DIM), f32),
                            pltpu.VMEM((N_KV, R, HEAD_DIM), f32)]),
        out_shape=jax.ShapeDtypeStruct((DB, N_KV, R, HEAD_DIM), f32),
        compiler_params=_cparams(("parallel", "arbitrary")),
        name="nsa_sample",
    )(page_table, *([pool_t] * pps), q_g, gates_g, summ, new_sel, win_t, new_win, expand)


def _bmm(spec, a, b):
    return jnp.einsum(spec, a.astype(bf16), b.astype(bf16), preferred_element_type=f32)


def _unit_lower_solve(L, rhs, C, bs):
    _mm = functools.partial(_bmm, "hij,hjk->hik")
    ri = lax.broadcasted_iota(jnp.int32, (1, C, C), 1)
    ci = lax.broadcasted_iota(jnp.int32, (1, C, C), 2)
    same = (ri // bs) == (ci // bs)
    eye = jnp.where(ri == ci, 1.0, 0.0)
    D = jnp.where(same, L, 0.0)
    T = eye - D
    P = D
    n = 2
    while n < bs:
        P = _mm(P, P)
        T = T + _mm(T, P)
        n *= 2
    x = _mm(T, rhs)
    nblk = C // bs
    if nblk == 1:
        return x
    Mb = _mm(T, jnp.where(same, 0.0, L))
    factors = []
    Pm = Mb
    n = 2
    while n < nblk:
        Pm = _mm(Pm, Pm)
        factors.append(Pm)
        n *= 2
    for Pm in reversed(factors):
        x = x + _mm(Pm, x)
    return x - _mm(Mb, x)


def _rwkv_body(ps_ref, sp_ref, s0_ref, mu_ref, w0_ref, ww_ref, a0_ref, wa_ref, kk_ref, ka_ref, rk_ref,
               lnw_ref, lnb_ref, seg_ref, segt_ref, y_ref, so_ref, carry_ref, state_ref, *, C, bs):
    c = pl.program_id(1)
    nbat = ps_ref.shape[0]
    H = RW_HEADS

    @pl.when(c == 0)
    def _():
        carry_ref[...] = sp_ref[...]
        state_ref[...] = s0_ref[...].reshape(state_ref.shape)

    def heads(x):
        return [x[:, h * RW_HEAD:(h + 1) * RW_HEAD] for h in range(H)]

    ri = lax.broadcasted_iota(jnp.int32, (C, C), 0)
    ci = lax.broadcasted_iota(jnp.int32, (C, C), 1)
    tril = jnp.where(ci <= ri, 1.0, 0.0).astype(bf16)
    rowi = lax.broadcasted_iota(jnp.int32, (C, 1), 0)

    x1_l, x2_l, kb_l, v_l, rk_l, etot_l = [], [], [], [], [], []
    for n in range(nbat):
        ps = ps_ref[n]
        prev = jnp.where(rowi == 0, carry_ref[n], pltpu.roll(ps, 1, axis=0))
        carry_ref[n] = ps[C - 1:C, :]
        z = ps + (prev - ps) * mu_ref[...]
        r = z[:, 0:RW_DIM]
        k = z[:, RW_DIM:2 * RW_DIM]
        v = z[:, 2 * RW_DIM:3 * RW_DIM]
        xw = z[:, 3 * RW_DIM:3 * RW_DIM + LORA_W]
        xa = z[:, 3 * RW_DIM + LORA_W:]
        u = -(w0_ref[...] + jnp.dot(jnp.tanh(xw).astype(bf16), ww_ref[...], preferred_element_type=f32))
        softplus = jnp.maximum(u, 0.0) + jnp.log(1.0 + jnp.exp(-jnp.abs(u)))
        lw = -jnp.exp(-softplus - 0.5)
        a = jax.nn.sigmoid(a0_ref[...] + jnp.dot(xa.astype(bf16), wa_ref[...], preferred_element_type=f32))
        kk = k * kk_ref[...]
        ss = _dot_f32_lhs(kk * kk, seg_ref[...], 2)
        kk = kk * _dot_f32_lhs(lax.rsqrt(jnp.maximum(ss, 1e-24)), segt_ref[...], 2)
        bb = kk * a
        k2 = k * (1.0 + (a - 1.0) * ka_ref[...])
        G = _dot_f32_rhs(tril, lw, 3)
        g_end = G[C - 1:C, :]
        e_neg = jnp.exp(-G)
        e_end = jnp.exp(g_end - G)
        x1_l.append(heads(jnp.concatenate([kk * jnp.exp(G - lw), r * jnp.exp(G)], axis=0).astype(bf16)))
        x2_l.append(heads(jnp.concatenate([k2 * e_neg, bb * e_neg], axis=0).astype(bf16)))
        kb_l.append(heads(jnp.concatenate([k2 * e_end, -(bb * e_end)], axis=0).astype(bf16)))
        v_l.append(heads(v))
        rk_l.append(heads(r * k2 * rk_ref[...]))
        etot_l.append(heads(jnp.exp(g_end)))

    stack = lambda lst: jnp.stack([t for per_b in lst for t in per_b], axis=0)
    X1, X2, KB = stack(x1_l), stack(x2_l), stack(kb_l)
    V, RK, ETOT = stack(v_l), stack(rk_l), stack(etot_l)

    strict = (ci < ri)[None]
    incl = (ci <= ri)[None]
    S = state_ref[...]
    A = _bmm("hck,hdk->hcd", X1, X2)
    P = _bmm("hck,hvk->hcv", X1, S)
    a_kk = jnp.where(strict, A[:, :C, :C], 0.0)
    a_kb = jnp.where(strict, A[:, :C, C:], 0.0)
    rhs = P[:, :C] + _bmm("hcd,hdv->hcv", a_kk, V)
    sa = _unit_lower_solve(a_kb, rhs, C, bs)
    a_r = jnp.concatenate([jnp.where(incl, A[:, C:, :C], 0.0), jnp.where(incl, -A[:, C:, C:], 0.0)], axis=2)
    vs = jnp.concatenate([V, sa], axis=1)
    y = P[:, C:] + _bmm("hcd,hdv->hcv", a_r, vs)
    state_ref[...] = S * ETOT + _bmm("hcv,hck->hvk", vs, KB)
    mean = jnp.mean(y, axis=-1, keepdims=True)
    var = jnp.mean(jnp.square(y - mean), axis=-1, keepdims=True)
    yn = (y - mean) * lax.rsqrt(var + LN_X_EPS)
    bonus = jnp.sum(RK, axis=-1, keepdims=True) * V
    for n in range(nbat):
        for h in range(H):
            sl = slice(h * RW_HEAD, (h + 1) * RW_HEAD)
            i = n * H + h
            y_ref[n, :, sl] = (yn[i] * lnw_ref[:, sl] + lnb_ref[:, sl] + bonus[i]).astype(y_ref.dtype)

    so_ref[...] = state_ref[...].reshape(so_ref.shape)


def _rwkv(p_shift, shift_prev, s0, rw):
    mu, w0, w_lora_w, a0, w_lora_a, k_k, k_a, r_k, ln_w, ln_b = rw
    B, T, _ = p_shift.shape
    C = _pick_tile(T, 64, 8)
    bs = min(16, C)
    nbat = _pick_tile(B, RWKV_BATCH_PER_STEP, 1)
    seg_np = (np.arange(RW_DIM)[:, None] // RW_HEAD == np.arange(RW_HEADS)[None, :]).astype(np.float32)
    vec = lambda n: pl.BlockSpec((1, n), lambda b, c: (0, 0))
    row = lambda t: t.reshape(1, -1)
    y, s_new = pl.pallas_call(
        functools.partial(_rwkv_body, C=C, bs=bs),
        grid=(B // nbat, T // C),
        in_specs=[pl.BlockSpec((nbat, C, SHIFT_COLS), lambda b, c: (b, c, 0)),
                  pl.BlockSpec((nbat, 1, SHIFT_COLS), lambda b, c: (b, 0, 0)),
                  pl.BlockSpec((nbat, RW_HEADS, RW_HEAD, RW_HEAD), lambda b, c: (b, 0, 0, 0)),
                  vec(SHIFT_COLS), vec(RW_DIM),
                  pl.BlockSpec((LORA_W, RW_DIM), lambda b, c: (0, 0)),
                  vec(RW_DIM),
                  pl.BlockSpec((LORA_A, RW_DIM), lambda b, c: (0, 0)),
                  vec(RW_DIM), vec(RW_DIM), vec(RW_DIM), vec(RW_DIM), vec(RW_DIM),
                  pl.BlockSpec((RW_DIM, RW_HEADS), lambda b, c: (0, 0)),
                  pl.BlockSpec((RW_HEADS, RW_DIM), lambda b, c: (0, 0))],
        out_specs=[pl.BlockSpec((nbat, C, RW_DIM), lambda b, c: (b, c, 0)),
                   pl.BlockSpec((nbat, RW_HEADS, RW_HEAD, RW_HEAD), lambda b, c: (b, 0, 0, 0))],
        out_shape=[jax.ShapeDtypeStruct((B, T, RW_DIM), bf16),
                   jax.ShapeDtypeStruct((B, RW_HEADS, RW_HEAD, RW_HEAD), f32)],
        scratch_shapes=[pltpu.VMEM((nbat, 1, SHIFT_COLS), f32),
                        pltpu.VMEM((nbat * RW_HEADS, RW_HEAD, RW_HEAD), f32)],
        compiler_params=_cparams(("parallel", "arbitrary")),
        name="rwkv",
    )(p_shift, shift_prev.reshape(B, 1, SHIFT_COLS), s0, row(mu), row(w0), w_lora_w.astype(bf16), row(a0),
      w_lora_a.astype(bf16), row(k_k), row(k_a), row(r_k), row(ln_w), row(ln_b),
      jnp.asarray(seg_np, bf16), jnp.asarray(seg_np.T, bf16))
    return y, s_new


def _merge_body(x_ref, oa_ref, yr_ref, ga_ref, gb_ref, wa_ref, wb_ref, wo_ref, h_ref):
    ma = jnp.dot(oa_ref[...], wa_ref[...], preferred_element_type=f32)
    mb = jnp.dot(yr_ref[...], wb_ref[...], preferred_element_type=f32)
    m = jax.nn.sigmoid(ga_ref[...]) * ma + jax.nn.sigmoid(gb_ref[...]) * mb
    h_ref[...] = x_ref[...] + jnp.dot(m.astype(bf16), wo_ref[...], preferred_element_type=f32)


def _merge(x, o_att, y_rw, p_merge, w_a, w_b, w_o):
    M, D = x.shape
    tm = _pick_tile(M, 256, 8)
    row = lambda i: (i, 0)
    const = lambda i: (0, 0)
    return pl.pallas_call(
        _merge_body,
        grid=(M // tm,),
        in_specs=[pl.BlockSpec((tm, D), row),
                  pl.BlockSpec((tm, ATT_DIM), row),
                  pl.BlockSpec((tm, RW_DIM), row),
                  pl.BlockSpec((tm, D), lambda i: (i, 0)),
                  pl.BlockSpec((tm, D), lambda i: (i, 1)),
                  pl.BlockSpec((ATT_DIM, D), const),
                  pl.BlockSpec((RW_DIM, D), const),
                  pl.BlockSpec((D, D), const)],
        out_specs=pl.BlockSpec((tm, D), row),
        out_shape=jax.ShapeDtypeStruct((M, D), f32),
        compiler_params=_cparams(("parallel",)),
        name="merge",
    )(x, o_att, y_rw, p_merge, p_merge, w_a, w_b, w_o)


def _conv_ffn_body(h_ref, g_ref, wug_ref, wuv_ref, cwg_ref, cwv_ref, cbg_ref, cbv_ref, wd_ref,
                   pg_ref, pv_ref, y_ref, tg_ref, tv_ref, hn_ref, cg_ref, cv_ref, *, tm, seq_rows, tail, nsub):
    i = pl.program_id(1)
    j = pl.program_id(2)
    carried = tm <= seq_rows

    @pl.when(j == 0)
    def _():
        h = h_ref[0]
        ms = jnp.mean(h * h, axis=-1, keepdims=True)
        hn_ref[...] = (h * lax.rsqrt(ms + NORM_EPS) * g_ref[...]).astype(bf16)
        y_ref[0] = h

    def taps(cw_ref, cb_ref, u2, u1, u):
        return cb_ref[...] + cw_ref[0:1, :] * u2 + cw_ref[1:2, :] * u1 + cw_ref[2:3, :] * u

    def finish(rows, gate, val):
        act = (gate * jax.nn.sigmoid(gate) * val).astype(bf16)
        y_ref[0, rows, :] += jnp.dot(act, wd_ref[...], preferred_element_type=f32)

    if not carried:
        t_in = lax.broadcasted_iota(jnp.int32, (tm, 1), 0) % seq_rows
        hn = hn_ref[...]

        def conv(u, cw_ref, cb_ref, prev_ref):
            pr = prev_ref[0]
            u1 = jnp.where(t_in == 0, pltpu.roll(pr, tm - 1, axis=0), pltpu.roll(u, 1, axis=0))
            u2 = jnp.where(t_in == 0, pr, jnp.where(t_in == 1, pr, pltpu.roll(u, 2, axis=0)))
            return taps(cw_ref, cb_ref, u2, u1, u)

        ug = jnp.dot(hn, wug_ref[...], preferred_element_type=f32)
        uv = jnp.dot(hn, wuv_ref[...], preferred_element_type=f32)
        tg_ref[0, 0] = ug
        tv_ref[0, 0] = uv
        finish(slice(None), conv(ug, cwg_ref, cbg_ref, pg_ref), conv(uv, cwv_ref, cbv_ref, pv_ref))
        return

    @pl.when(i == 0)
    def _():
        cg_ref[j] = pg_ref[0]
        cv_ref[j] = pv_ref[0]

    ts = tm // nsub
    rowi = lax.broadcasted_iota(jnp.int32, (ts, 1), 0)
    prev_g = (cg_ref[j, 0:1, :], cg_ref[j, 1:2, :])
    prev_v = (cv_ref[j, 0:1, :], cv_ref[j, 1:2, :])

    def conv(u, cw_ref, cb_ref, prev):
        p2, p1 = prev
        u1 = jnp.where(rowi == 0, p1, pltpu.roll(u, 1, axis=0))
        u2 = jnp.where(rowi == 0, p2, jnp.where(rowi == 1, p1, pltpu.roll(u, 2, axis=0)))
        return taps(cw_ref, cb_ref, u2, u1, u), (u[ts - 2:ts - 1, :], u[ts - 1:ts, :])

    for sb in range(nsub):
        rows = slice(sb * ts, (sb + 1) * ts)
        hn = hn_ref[rows, :]
        ug = jnp.dot(hn, wug_ref[...], preferred_element_type=f32)
        uv = jnp.dot(hn, wuv_ref[...], preferred_element_type=f32)
        gate, prev_g = conv(ug, cwg_ref, cbg_ref, prev_g)
        val, prev_v = conv(uv, cwv_ref, cbv_ref, prev_v)
        finish(rows, gate, val)
    cg_ref[j] = jnp.concatenate(prev_g, axis=0)
    cv_ref[j] = jnp.concatenate(prev_v, axis=0)
    tg_ref[0, 0] = ug[ts - tail:ts, :]
    tv_ref[0, 0] = uv[ts - tail:ts, :]


def _conv_ffn(h, conv_prev, norm_g, w_up, conv_w, conv_b, w_down, *, fold):
    B, T, D = h.shape
    dff = w_down.shape[0]
    tf = _pick_tile(dff, 512, LANE)
    nf = dff // tf
    if not fold:
        nb_, tm = B, _pick_tile(T, 1024, 8)
        tail = 8
        hh = h
        prev = conv_prev
        prev_spec_g = pl.BlockSpec((1, CONV_W - 1, tf), lambda b, i, j: (b, 0, j))
        prev_spec_v = pl.BlockSpec((1, CONV_W - 1, tf), lambda b, i, j: (b, 0, nf + j))
    else:
        nb_, tm = 1, B * T
        tail = tm
        hh = h.reshape(1, B * T, D)
        assert T >= CONV_W - 1
        prev = jnp.concatenate([conv_prev, jnp.zeros((B, T - (CONV_W - 1), 2 * dff), f32)],
                               axis=1).reshape(1, B * T, 2 * dff)
        prev_spec_g = pl.BlockSpec((1, tm, tf), lambda b, i, j: (0, 0, j))
        prev_spec_v = pl.BlockSpec((1, tm, tf), lambda b, i, j: (0, 0, nf + j))
    nt = hh.shape[1] // tm
    nsub = FFN_ROW_SUBBLOCKS if (not fold and tm % (8 * FFN_ROW_SUBBLOCKS) == 0) else 1
    body = functools.partial(_conv_ffn_body, tm=tm, seq_rows=T, tail=tail, nsub=nsub)
    cw = conv_w
    cb = conv_b.reshape(1, 2 * dff)
    tail_spec = pl.BlockSpec((1, 1, tail, tf), lambda b, i, j: (b, i, 0, j))
    tail_shape = jax.ShapeDtypeStruct((nb_, nt, tail, dff), f32)
    y, ug, uv = pl.pallas_call(
        body,
        grid=(nb_, nt, nf),
        in_specs=[pl.BlockSpec((1, tm, D), lambda b, i, j: (b, i, 0)),
                  pl.BlockSpec((1, D), lambda b, i, j: (0, 0)),
                  pl.BlockSpec((D, tf), lambda b, i, j: (0, j)),
                  pl.BlockSpec((D, tf), lambda b, i, j: (0, nf + j)),
                  pl.BlockSpec((CONV_W, tf), lambda b, i, j: (0, j)),
                  pl.BlockSpec((CONV_W, tf), lambda b, i, j: (0, nf + j)),
                  pl.BlockSpec((1, tf), lambda b, i, j: (0, j)),
                  pl.BlockSpec((1, tf), lambda b, i, j: (0, nf + j)),
                  pl.BlockSpec((tf, D), lambda b, i, j: (j, 0)),
                  prev_spec_g, prev_spec_v],
        out_specs=[pl.BlockSpec((1, tm, D), lambda b, i, j: (b, i, 0)), tail_spec, tail_spec],
        out_shape=[jax.ShapeDtypeStruct(hh.shape, f32), tail_shape, tail_shape],
        scratch_shapes=[pltpu.VMEM((tm, D), bf16),
                        pltpu.VMEM((nf, CONV_W - 1, tf), f32),
                        pltpu.VMEM((nf, CONV_W - 1, tf), f32)],
        compiler_params=_cparams(("parallel", "arbitrary", "arbitrary")),
        name="conv_ffn",
    )(hh, norm_g.reshape(1, D), w_up, w_up, cw, cw, cb, cb, w_down, prev, prev)
    return y, ug[:, -1], uv[:, -1]


def _split_w_in(w_in):
    o = 0
    parts = []
    for n in (ATT_DIM, KV_COLS, KV_COLS, KV_COLS, 3 * N_HEADS, SHIFT_COLS, 2 * w_in.shape[0]):
        parts.append(w_in[:, o:o + n])
        o += n
    wq, wc, ws, ww, wg, wsh, wm = parts
    wg = jnp.pad(wg, ((0, 0), (0, LANE - 3 * N_HEADS)))
    w_att = jnp.concatenate([wq, wc, ws, ww, wg], axis=1).astype(bf16)
    return w_att, wsh.astype(bf16), wm.astype(bf16)


def _head_major(x, n):
    B, T, _ = x.shape
    return x.reshape(B, T, n, HEAD_DIM).transpose(0, 2, 1, 3)


def _mixer_inputs(x2d, pos_tab, norm_g, w_parts, consts, wc, with_summ):
    w_att, w_sh, w_mg = w_parts
    p_att = _norm_matmul(x2d, norm_g, w_att)
    p_shift = _norm_matmul(x2d, norm_g, w_sh)
    p_merge = _norm_matmul(x2d, norm_g, w_mg)
    post = _qk_post(p_att, pos_tab, consts, wc, with_summ)
    return post, p_shift, p_merge


def kernel(x_prompt, x_sample, cache_kv_cmp, cache_kv_sel, page_table, cache_kv_win, state_wkv, state_shift, state_conv, norm1_g, w_in, q_gain, k_gains, w_cmp, mu_shift, w0, w_lora_w, a0, w_lora_a, k_k, k_a, r_k, ln_x_w, ln_x_b, w_branch_a, w_branch_b, w_out, norm2_g, w_up, conv_w, conv_b, w_down):
    B, T, D = x_prompt.shape
    DB, TS, _ = x_sample.shape
    depth = w_in.shape[0]
    assert depth == 1, "single-layer trunk"
    l = 0
    page = cache_kv_cmp.shape[2]
    n_pages = page_table.shape[1]
    past_len = n_pages * page
    assert past_len % BLOCK == 0 and TS <= BLOCK and page % BLOCK == 0 and T % BLOCK == 0
    dff = w_down.shape[1]

    w_parts = _split_w_in(w_in[l])
    consts = _qk_consts(q_gain[l], k_gains[l])
    wc = _compress_weights(w_cmp[l])
    rw = (mu_shift[l], w0[l], w_lora_w[l], a0[l], w_lora_a[l], k_k[l], k_a[l], r_k[l], ln_x_w[l], ln_x_b[l])
    w_a = w_branch_a[l].astype(bf16)
    w_b = w_branch_b[l].astype(bf16)
    w_o = w_out[l].astype(bf16)
    w_u = w_up[l].astype(bf16)
    w_d = w_down[l].astype(bf16)

    xp = x_prompt.reshape(B * T, D)
    tabs_p = _rope_tables(jnp.arange(T, dtype=jnp.int32))
    ((q_pad, kvc, kvs, kvw, gates, summ, ks_aug, vs_pad, kw_pad, vw_pad),
     p_shift, p_merge) = _mixer_inputs(xp, tabs_p, norm1_g[l], w_parts, consts, wc, True)
    nb = T // BLOCK
    summ = summ.reshape(B, nb, KV_COLS)
    lane_pad = lambda x: jnp.pad(x, ((0, 0), (0, 0), (0, 0), (0, LANE - HEAD_DIM))).astype(bf16)
    kvw3 = kvw.reshape(B, T, KV_COLS)
    o_att = _nsa_prompt(q_pad,
                        lane_pad(_head_major(summ[:, :, :K_COLS], N_KV)),
                        lane_pad(_head_major(summ[:, :, K_COLS:], N_KV)),
                        ks_aug, vs_pad, kw_pad, vw_pad, gates)
    p_shift3 = p_shift.reshape(B, T, SHIFT_COLS)
    y_rw, wkv_p = _rwkv(p_shift3, jnp.zeros((B, SHIFT_COLS), f32),
                        jnp.zeros((B, RW_HEADS, RW_HEAD, RW_HEAD), f32), rw)
    h_p = _merge(xp, o_att.reshape(B * T, ATT_DIM), y_rw.reshape(B * T, RW_DIM), p_merge, w_a, w_b, w_o)
    y_p, ug, uv = _conv_ffn(h_p.reshape(B, T, D), jnp.zeros((B, CONV_W - 1, 2 * dff), f32), norm2_g[l],
                            w_u, conv_w[l], conv_b[l], w_d, fold=False)
    assert T >= CONV_W - 1
    conv_p = jnp.concatenate([ug[:, -(CONV_W - 1):], uv[:, -(CONV_W - 1):]], axis=-1)
    kv_shape_p = (1, B, T, 2, N_KV, HEAD_DIM)
    keep_p = min(WINDOW, T)
    outs_p = (y_p,
              kvc.reshape(kv_shape_p), kvs.reshape(kv_shape_p),
              kvw3[:, T - keep_p:].reshape(1, B, keep_p, 2, N_KV, HEAD_DIM),
              wkv_p[None], p_shift3[:, -1][None], conv_p[None])

    xs = x_sample.reshape(DB * TS, D)
    pos_s = past_len + jnp.arange(TS, dtype=jnp.int32)
    tabs_s = tuple(jnp.tile(t, (DB, 1)) for t in _rope_tables(pos_s))
    (q, kvc_s, kvs_s, kvw_s, gates), p_shift, p_merge = _mixer_inputs(xs, tabs_s, norm1_g[l], w_parts, consts, wc, False)
    pps = _pick_tile(n_pages, PAGES_PER_STEP, 1)
    summ_s = _compress_pool(_pages_row_minor(cache_kv_cmp[l]), page_table, w_cmp[l], pps)
    summ_s = summ_s.reshape(DB, past_len // BLOCK, KV_COLS)
    R = HPG * TS
    q_g = _head_major(q.reshape(DB, TS, ATT_DIM), N_HEADS).reshape(DB, N_KV, R, HEAD_DIM)
    gates_g = (gates[:, :3 * N_HEADS].reshape(DB, TS, N_HEADS, 3).transpose(0, 2, 1, 3)
               .reshape(DB, N_KV, R, 3))
    keep = cache_kv_win.shape[2]
    kvw_s3 = kvw_s.reshape(DB, TS, KV_COLS)
    o_g = _nsa_sample(q_g, gates_g, summ_s, kvs_s.reshape(DB, TS, KV_COLS), _pages_row_minor(cache_kv_win[l]),
                      kvw_s3, _pages_row_minor(cache_kv_sel[l]), page_table, pps, past_len)
    o_att_s = (o_g.reshape(DB, N_HEADS, TS, HEAD_DIM).transpose(0, 2, 1, 3)
               .reshape(DB * TS, ATT_DIM).astype(bf16))
    p_shift3s = p_shift.reshape(DB, TS, SHIFT_COLS)
    y_rw_s, wkv_s = _rwkv(p_shift3s, state_shift[l], state_wkv[l], rw)
    h_s = _merge(xs, o_att_s, y_rw_s.reshape(DB * TS, RW_DIM), p_merge, w_a, w_b, w_o)
    y_s, ug, uv = _conv_ffn(h_s.reshape(DB, TS, D), state_conv[l], norm2_g[l],
                            w_u, conv_w[l], conv_b[l], w_d, fold=True)
    up_s = jnp.concatenate([state_conv[l],
                            jnp.concatenate([ug.reshape(DB, TS, dff), uv.reshape(DB, TS, dff)], axis=-1)], axis=1)
    conv_s = up_s[:, TS:]
    win_s = jnp.concatenate([cache_kv_win[l], kvw_s.reshape(DB, TS, 2, N_KV, HEAD_DIM)], axis=1)[:, TS:]
    kv_shape_s = (1, DB, TS, 2, N_KV, HEAD_DIM)

    return (outs_p[0], y_s.reshape(DB, TS, D),
            outs_p[1], kvc_s.reshape(kv_shape_s),
            outs_p[2], kvs_s.reshape(kv_shape_s),
            outs_p[3], win_s.reshape(1, DB, keep, 2, N_KV, HEAD_DIM),
            outs_p[4], wkv_s[None],
            outs_p[5], p_shift3s[:, -1][None],
            outs_p[6], conv_s[None])
```
